```python
import jax, jax.numpy as jnp
from jax import lax
import numpy as np

D_MODEL = 1024
BATCH = 4
SEQ = 4096
DEPTH = 1
DEC_BATCH = 32
DEC_SEQ = 4
PAST_LEN = 8192
PAGE_SIZE = 128

HEAD_DIM = 64
N_HEADS_A = 8
N_KV_A = 2
GROUP_A = N_HEADS_A // N_KV_A
N_IDX_HEADS = 8
IDX_DIM = 64
TOPK_MAX = 256
N_HEADS_B = 8
Q_A_W = N_HEADS_A * HEAD_DIM
KV_A_W = 2 * N_KV_A * HEAD_DIM
Q_I_W = N_IDX_HEADS * IDX_DIM
Q_B_W = N_HEADS_B * HEAD_DIM
KV_B_W = 2 * N_HEADS_B * HEAD_DIM
IN_W = Q_A_W + KV_A_W + Q_I_W + IDX_DIM + N_IDX_HEADS + Q_B_W + KV_B_W + 2 * D_MODEL
D_FF = -(-8 * D_MODEL // (3 * 256)) * 256
ROPE_THETA = 10000.0
QBLK = 128
RMS_EPS = 1e-6

kernel_name = 'dsa_stickbreak_gated_hybrid_step'


def _rmsnorm(x, g):
    x32 = x.astype(jnp.float32)
    y = x32 * lax.rsqrt(jnp.mean(x32 * x32, axis=-1, keepdims=True) + RMS_EPS)
    return (y * g.astype(jnp.float32)).astype(x.dtype)


def _rope(x, pos):
    half = x.shape[-1] // 2
    inv_freq = ROPE_THETA ** (-jnp.arange(half, dtype=jnp.float32) / half)
    ang = pos.astype(jnp.float32)[:, None] * inv_freq[None, :]
    cos = jnp.cos(ang)[:, None, :]
    sin = jnp.sin(ang)[:, None, :]
    x1 = x[..., :half].astype(jnp.float32)
    x2 = x[..., half:].astype(jnp.float32)
    return jnp.concatenate([x1 * cos - x2 * sin, x2 * cos + x1 * sin], axis=-1).astype(x.dtype)


def _project(xn, w_in, pos):
    b, t, _ = xn.shape
    y = xn @ w_in
    sizes = (Q_A_W, KV_A_W, Q_I_W, IDX_DIM, N_IDX_HEADS, Q_B_W, KV_B_W)
    offs = []
    acc = 0
    for s in sizes:
        acc += s
        offs.append(acc)
    p = jnp.split(y, offs, axis=-1)
    q_a = _rope(p[0].reshape(b, t, N_HEADS_A, HEAD_DIM), pos)
    kv = p[1].reshape(b, t, 2, N_KV_A, HEAD_DIM)
    kv_a = jnp.stack([_rope(kv[:, :, 0], pos), kv[:, :, 1]], axis=2)
    q_i = _rope(p[2].reshape(b, t, N_IDX_HEADS, IDX_DIM), pos)
    k_i = _rope(p[3].reshape(b, t, 1, IDX_DIM), pos)[:, :, 0]
    w_i = p[4] * (N_IDX_HEADS ** -0.5)
    q_b = p[5].reshape(b, t, N_HEADS_B, HEAD_DIM)
    kv_b = p[6].reshape(b, t, 2, N_HEADS_B, HEAD_DIM)
    gates = jax.nn.sigmoid(p[7].astype(jnp.float32)).astype(xn.dtype).reshape(b, t, 2, D_MODEL)
    return q_a, kv_a, q_i, k_i, w_i, q_b, kv_b, gates


def _dsa_block(qpos, q_a, q_i, w_i, kv_a, k_i):
    b, tb = q_a.shape[:2]
    n_keys = k_i.shape[1]
    n_sel = max(1, min(TOPK_MAX, n_keys // 4))
    rel = jax.nn.relu(jnp.einsum('bthe,bse->bths', q_i, k_i).astype(jnp.float32))
    score = jnp.einsum('bth,bths->bts', w_i.astype(jnp.float32), rel) * (IDX_DIM ** -0.5)
    kpos = jnp.arange(n_keys, dtype=jnp.int32)
    causal = kpos[None, :] <= qpos[:, None]
    score = jnp.where(causal[None], score, -jnp.inf)
    _, idx = lax.top_k(score, n_sel)
    valid = idx <= qpos[None, :, None]
    sel = jax.vmap(lambda rows, ii: rows[ii])(kv_a, idx)
    qg = q_a.reshape(b, tb, N_KV_A, GROUP_A, HEAD_DIM)
    s = jnp.einsum('btcgd,btncd->btcgn', qg, sel[:, :, :, 0]).astype(jnp.float32) * (HEAD_DIM ** -0.5)
    s = jnp.where(valid[:, :, None, None, :], s, -jnp.inf)
    p = jax.nn.softmax(s, axis=-1).astype(q_a.dtype)
    o = jnp.einsum('btcgn,btncd->btcgd', p, sel[:, :, :, 1])
    return o.reshape(b, tb, Q_A_W)


def _sb_block(qpos, q_b, kv_b):
    b, tb = q_b.shape[:2]
    n_keys = kv_b.shape[1]
    z = jnp.einsum('bthd,bshd->bhts', q_b, kv_b[:, :, 0]).astype(jnp.float32) * (HEAD_DIM ** -0.5)
    kpos = jnp.arange(n_keys, dtype=jnp.int32)
    strict = (kpos[None, :] < qpos[:, None])[None, None]
    log_beta = jax.nn.log_sigmoid(z)
    log_rest = jnp.where(strict, log_beta - z, 0.0)
    later = lax.cumsum(log_rest, axis=3, reverse=True) - log_rest
    a = jnp.where(strict, jnp.exp(log_beta + later), 0.0)
    o = jnp.einsum('bhts,bshd->bthd', a.astype(q_b.dtype), kv_b[:, :, 1])
    return o.reshape(b, tb, Q_B_W)


def _sweep(fn, pos, *qs):
    t = pos.shape[0]
    if t % QBLK != 0 or t <= QBLK:
        return fn(pos, *qs)
    nb = t // QBLK

    def split(a):
        return jnp.moveaxis(a.reshape(a.shape[0], nb, QBLK, *a.shape[2:]), 1, 0)

    out = lax.map(lambda args: fn(*args), (pos.reshape(nb, QBLK),) + tuple(split(a) for a in qs))
    out = jnp.moveaxis(out, 0, 1)
    return out.reshape(out.shape[0], t, *out.shape[3:])


def _mixers(pos, q_a, kv_a_ctx, q_i, k_i_ctx, w_i, q_b, kv_b_ctx, gates, w_br_a, w_br_b, w_o):
    o_a = _sweep(lambda qp, qa, qi, wi: _dsa_block(qp, qa, qi, wi, kv_a_ctx, k_i_ctx), pos, q_a, q_i, w_i)
    o_b = _sweep(lambda qp, qb: _sb_block(qp, qb, kv_b_ctx), pos, q_b)
    merged = gates[:, :, 0] * (o_a @ w_br_a) + gates[:, :, 1] * (o_b @ w_br_b)
    return merged @ w_o


def _ffn(h, w_gate, w_up, w_down):
    return (jax.nn.silu(h @ w_gate) * (h @ w_up)) @ w_down


def _gather_pages(pool, page_table):
    rows = pool[page_table]
    b, n_pages, page = rows.shape[:3]
    return rows.reshape(b, n_pages * page, *rows.shape[3:])


def setup_inputs(seed: int = 0) -> dict:
    key = jax.random.key(seed)
    ks = jax.random.split(key, 20)
    f32 = jnp.float32
    n_pages = PAST_LEN // PAGE_SIZE
    n_used = DEC_BATCH * n_pages
    n_pool = n_used + max(1, n_used // 4)

    def nrm(k, shape, scale=1.0):
        return jax.random.normal(k, shape, f32) * scale

    page_table = jax.random.permutation(ks[5], n_pool)[:n_used].reshape(DEC_BATCH, n_pages).astype(jnp.int32)
    return {
        'x_prompt': nrm(ks[0], (BATCH, SEQ, D_MODEL)),
        'x_sample': nrm(ks[1], (DEC_BATCH, DEC_SEQ, D_MODEL)),
        'cache_kv_a': nrm(ks[2], (DEPTH, n_pool, PAGE_SIZE, 2, N_KV_A, HEAD_DIM)),
        'cache_k_idx': nrm(ks[3], (DEPTH, n_pool, PAGE_SIZE, IDX_DIM)),
        'cache_kv_b': nrm(ks[4], (DEPTH, n_pool, PAGE_SIZE, 2, N_HEADS_B, HEAD_DIM)),
        'page_table': page_table,
        'w_in': nrm(ks[6], (DEPTH, D_MODEL, IN_W), D_MODEL ** -0.5),
        'w_br_a': nrm(ks[7], (DEPTH, Q_A_W, D_MODEL), Q_A_W ** -0.5),
        'w_br_b': nrm(ks[8], (DEPTH, Q_B_W, D_MODEL), Q_B_W ** -0.5),
        'w_o': nrm(ks[9], (DEPTH, D_MODEL, D_MODEL), D_MODEL ** -0.5),
        'norm_attn': 1.0 + nrm(ks[10], (DEPTH, D_MODEL), 0.01),
        'norm_ffn': 1.0 + nrm(ks[11], (DEPTH, D_MODEL), 0.01),
        'w_ffn_gate': nrm(ks[12], (DEPTH, D_MODEL, D_FF), D_MODEL ** -0.5),
        'w_ffn_up': nrm(ks[13], (DEPTH, D_MODEL, D_FF), D_MODEL ** -0.5),
        'w_ffn_down': nrm(ks[14], (DEPTH, D_FF, D_MODEL), D_FF ** -0.5),
        'norm_final': 1.0 + nrm(ks[15], (D_MODEL,), 0.01),
    }


def reference(x_prompt, x_sample, cache_kv_a, cache_k_idx, cache_kv_b, page_table, w_in, w_br_a, w_br_b,
              w_o, norm_attn, norm_ffn, w_ffn_gate, w_ffn_up, w_ffn_down, norm_final):
    pos_p = jnp.arange(x_prompt.shape[1], dtype=jnp.int32)
    past_len = page_table.shape[1] * cache_kv_a.shape[2]
    pos_s = past_len + jnp.arange(x_sample.shape[1], dtype=jnp.int32)
    hp, hs = x_prompt, x_sample
    kv_a_p, k_i_p, kv_b_p, kv_a_s, k_i_s, kv_b_s = [], [], [], [], [], []
    for layer in range(DEPTH):
        xn = _rmsnorm(hp, norm_attn[layer])
        q_a, kv_a, q_i, k_i, w_i, q_b, kv_b, gates = _project(xn, w_in[layer], pos_p)
        hp = hp + _mixers(pos_p, q_a, kv_a, q_i, k_i, w_i, q_b, kv_b, gates,
                          w_br_a[layer], w_br_b[layer], w_o[layer])
        hp = hp + _ffn(_rmsnorm(hp, norm_ffn[layer]), w_ffn_gate[layer], w_ffn_up[layer], w_ffn_down[layer])
        kv_a_p.append(kv_a)
        k_i_p.append(k_i)
        kv_b_p.append(kv_b)

        xn = _rmsnorm(hs, norm_attn[layer])
        q_a, kv_a, q_i, k_i, w_i, q_b, kv_b, gates = _project(xn, w_in[layer], pos_s)
        kv_a_ctx = jnp.concatenate([_gather_pages(cache_kv_a[layer], page_table), kv_a], axis=1)
        k_i_ctx = jnp.concatenate([_gather_pages(cache_k_idx[layer], page_table), k_i], axis=1)
        kv_b_ctx = jnp.concatenate([_gather_pages(cache_kv_b[layer], page_table), kv_b], axis=1)
        hs = hs + _mixers(pos_s, q_a, kv_a_ctx, q_i, k_i_ctx, w_i, q_b, kv_b_ctx, gates,
                          w_br_a[layer], w_br_b[layer], w_o[layer])
        hs = hs + _ffn(_rmsnorm(hs, norm_ffn[layer]), w_ffn_gate[layer], w_ffn_up[layer], w_ffn_down[layer])
        kv_a_s.append(kv_a)
        k_i_s.append(k_i)
        kv_b_s.append(kv_b)
    y_prompt = _rmsnorm(hp, norm_final)
    y_sample = _rmsnorm(hs, norm_final)
    return (y_prompt, y_sample, jnp.stack(kv_a_p), jnp.stack(k_i_p), jnp.stack(kv_b_p),
            jnp.stack(kv_a_s), jnp.stack(k_i_s), jnp.stack(kv_b_s))
```

```python
import functools

import jax
import jax.numpy as jnp
import numpy as np
from jax import lax
from jax.experimental import pallas as pl
from jax.experimental.pallas import tpu as pltpu

F32 = jnp.float32
BF16 = jnp.bfloat16
I32 = jnp.int32

D_MODEL = 1024
HEAD_DIM = 64
N_HEADS_A = 8
N_KV_A = 2
GROUP_A = N_HEADS_A // N_KV_A
N_IDX_HEADS = 8
IDX_DIM = 64
TOPK_MAX = 256
N_HEADS_B = 8
Q_A_W = N_HEADS_A * HEAD_DIM
KV_A_W = 2 * N_KV_A * HEAD_DIM
Q_I_W = N_IDX_HEADS * IDX_DIM
Q_B_W = N_HEADS_B * HEAD_DIM
KV_B_W = 2 * N_HEADS_B * HEAD_DIM
D_FF = -(-8 * D_MODEL // (3 * 256)) * 256
ROPE_THETA = 10000.0
RMS_EPS = 1e-6
QK_SCALE = HEAD_DIM ** -0.5
IDX_SCALE = IDX_DIM ** -0.5

LANES = 128
QBLK = 128
FFN_CHUNK = 256

PK_QA = 0
PK_KA = 512
PK_QI = 640
PK_KI = 1152
PK_VA = 1280
PK_QB = 1408
PK_KVB = 1920
PK_WI = 2944
PK_G = 3072
PK_W = 5120
QA_HEAD_ORDER = (0, 4, 1, 5, 2, 6, 3, 7)

INT_MIN = -2 ** 31
NEGINF_KEY = -2139095041
IDX_ALL = 1 << 30
SB_STOP = -105.0

VMEM_LIMIT = 60 * 1024 * 1024


def _cparams(*sem):
    return pltpu.CompilerParams(dimension_semantics=sem, vmem_limit_bytes=VMEM_LIMIT)


def _const_spec(shape):
    nd = len(shape)
    return pl.BlockSpec(shape, lambda *_: (0,) * nd, pipeline_mode=pl.Buffered(1))


def _project_kernel(x_ref, g_ref, w_ref, cs_ref, sn_ref,
                    kva_ref, kidx_ref, kvb_ref,
                    qa_ref, qi_ref, wi_ref, qb_ref, gate_ref,
                    kab_ref, vab_ref, kib_ref, kvbb_ref):
    x = x_ref[...]
    ms = jnp.mean(x * x, axis=-1, keepdims=True)
    xn = (x * lax.rsqrt(ms + RMS_EPS)) * g_ref[...]
    xb = xn.astype(BF16)
    tm = x.shape[0]
    cs = cs_ref[...]
    sn = sn_ref[...]
    lane = lax.broadcasted_iota(I32, (tm, LANES), 1)
    first_half = (lane & (HEAD_DIM // 2)) == 0

    def mm(c0, n):
        return jnp.dot(xb, w_ref[:, c0:c0 + n], preferred_element_type=F32)

    def rope(y):
        partner = jnp.where(first_half, pltpu.roll(y, LANES - HEAD_DIM // 2, 1),
                            pltpu.roll(y, HEAD_DIM // 2, 1))
        return y * cs + partner * sn

    ya = mm(PK_QA, Q_A_W)
    for c in range(Q_A_W // LANES):
        sl = slice(c * LANES, (c + 1) * LANES)
        qa_ref[:, sl] = (rope(ya[:, sl]) * QK_SCALE).astype(BF16)
    ka = rope(mm(PK_KA, LANES))
    va = mm(PK_VA, LANES)
    kva_ref[:, 0:LANES] = ka
    kva_ref[:, LANES:2 * LANES] = va
    kab_ref[...] = ka.astype(BF16)
    vab_ref[...] = va.astype(BF16)
    yi = mm(PK_QI, Q_I_W)
    for c in range(Q_I_W // LANES):
        sl = slice(c * LANES, (c + 1) * LANES)
        qi_ref[:, sl] = rope(yi[:, sl]).astype(BF16)
    ki = rope(mm(PK_KI, LANES))
    kidx_ref[...] = ki[:, 0:IDX_DIM]
    kib_ref[...] = ki.astype(BF16)
    wi_ref[...] = (mm(PK_WI, LANES) * (N_IDX_HEADS ** -0.5)) * IDX_SCALE
    qb_ref[...] = (mm(PK_QB, Q_B_W) * QK_SCALE).astype(BF16)
    for c in range(KV_B_W // 512):
        sl = slice(c * 512, (c + 1) * 512)
        y = mm(PK_KVB + c * 512, 512)
        kvb_ref[:, sl] = y
        kvbb_ref[:, sl] = y.astype(BF16)
    for c in range(2 * D_MODEL // 512):
        sl = slice(c * 512, (c + 1) * 512)
        y = mm(PK_G + c * 512, 512)
        gate_ref[:, sl] = (1.0 / (1.0 + jnp.exp(-y))).astype(BF16)


def _project(x2d, gain, w_packed, cs, sn, tm):
    rows = x2d.shape[0]
    n_tab = cs.shape[0] // tm
    row_spec = lambda n: pl.BlockSpec((tm, n), lambda i: (i, 0))
    tab_spec = pl.BlockSpec((tm, LANES), lambda i: (i % n_tab, 0))
    widths_dtypes = [
        (KV_A_W, F32), (IDX_DIM, F32), (KV_B_W, F32),
        (Q_A_W, BF16), (Q_I_W, BF16), (LANES, F32), (Q_B_W, BF16), (2 * D_MODEL, BF16),
        (LANES, BF16), (LANES, BF16), (LANES, BF16), (KV_B_W, BF16),
    ]
    return pl.pallas_call(
        _project_kernel,
        grid=(rows // tm,),
        in_specs=[row_spec(D_MODEL), _const_spec((1, D_MODEL)), _const_spec((D_MODEL, PK_W)),
                  tab_spec, tab_spec],
        out_specs=[row_spec(n) for n, _ in widths_dtypes],
        out_shape=[jax.ShapeDtypeStruct((rows, n), dt) for n, dt in widths_dtypes],
        compiler_params=_cparams("parallel"),
        name="project",
    )(x2d, gain, w_packed, cs, sn)


def _sort_key(x):
    b = lax.bitcast_convert_type(x, I32)
    return b ^ ((b >> 31) & jnp.int32(0x7FFFFFFF))


def _count(key_ref, n_tiles, ts, rows, pred):
    def body(j, acc):
        off = pl.multiple_of(j * ts, ts)
        kt = key_ref[:, pl.ds(off, ts)]
        kpos = j * ts + lax.broadcasted_iota(I32, (rows, ts), 1)
        m = jnp.where(pred(kt, kpos), 1.0, 0.0)
        s = m[:, 0:LANES]
        for c in range(1, ts // LANES):
            s = s + m[:, c * LANES:(c + 1) * LANES]
        return acc + s

    acc = lax.fori_loop(0, n_tiles, body, jnp.zeros((rows, LANES), F32))
    return jnp.sum(acc, axis=1, keepdims=True)


def _topk_threshold(key_ref, n_tiles, ts, rows, k_sel, idx_bits):
    k_f = float(k_sel)

    def bis(b, r):
        cand = r + lax.shift_left(jnp.int32(1), jnp.int32(31) - b)
        cnt = _count(key_ref, n_tiles, ts, rows, lambda kt, kp: kt >= cand)
        return jnp.where(cnt >= k_f, cand, r)

    tau = lax.fori_loop(0, 32, bis, jnp.full((rows, 1), INT_MIN, I32))
    cnt_gt = _count(key_ref, n_tiles, ts, rows, lambda kt, kp: kt > tau)
    cnt_eq = _count(key_ref, n_tiles, ts, rows, lambda kt, kp: kt == tau)
    need = k_f - cnt_gt
    tie = (cnt_eq > need) & (tau > NEGINF_KEY)

    def ib(b, x):
        c = x + lax.shift_left(jnp.int32(1), jnp.int32(idx_bits - 1) - b)
        cnt = _count(key_ref, n_tiles, ts, rows, lambda kt, kp: (kt == tau) & (kp < c))
        return jnp.where(cnt < need, c, x)

    x = lax.fori_loop(0, idx_bits, ib, jnp.zeros((rows, 1), I32))
    tau_i = jnp.where(tie, x, IDX_ALL)
    return tau, tau_i


DSA_TS = 256


def _dsa_prompt_kernel(k_sel, idx_bits, qi_ref, wi_ref, qa_ref, ki_ref, ka_ref, va_ref,
                       oa_ref, key_ref):
    i = pl.program_id(1)
    ts = DSA_TS
    n_tiles = (i * QBLK + QBLK + ts - 1) // ts
    lane = lax.broadcasted_iota(I32, (QBLK, LANES), 1)
    left = lane < HEAD_DIM
    row = lax.broadcasted_iota(I32, (QBLK, ts), 0)
    col = lax.broadcasted_iota(I32, (QBLK, ts), 1)
    qpos = i * QBLK + row

    zero = jnp.zeros((QBLK, LANES), BF16)
    qi_parts = []
    for p in range(Q_I_W // LANES):
        chunk = qi_ref[:, p * LANES:(p + 1) * LANES]
        qi_parts.append(jnp.where(left, chunk, zero))
        qi_parts.append(jnp.where(left, zero, chunk))
    qi_all = jnp.concatenate(qi_parts, axis=0)
    w = wi_ref[...]
    w_cols = [jnp.broadcast_to(w[:, h:h + 1], (QBLK, ts)) for h in range(N_IDX_HEADS)]

    def score_tile(j, _):
        off = pl.multiple_of(j * ts, ts)
        k = ki_ref[pl.ds(off, ts), :]
        rel = lax.dot_general(qi_all, k, (((1,), (1,)), ((), ())), preferred_element_type=F32)
        sc = jnp.zeros((QBLK, ts), F32)
        for h in range(N_IDX_HEADS):
            sc = sc + w_cols[h] * jnp.maximum(rel[h * QBLK:(h + 1) * QBLK, :], 0.0)
        sc = jnp.where(j * ts + col <= qpos, sc, -jnp.inf)
        key_ref[:, pl.ds(off, ts)] = _sort_key(sc)
        return 0

    lax.fori_loop(0, n_tiles, score_tile, 0)
    tau, tau_i = _topk_threshold(key_ref, n_tiles, ts, QBLK, k_sel, idx_bits)

    q4 = []
    for c in range(N_KV_A):
        parts = []
        for g in range(GROUP_A):
            chunk = qa_ref[:, g * LANES:(g + 1) * LANES]
            parts.append(jnp.where(left, chunk, zero) if c == 0 else jnp.where(left, zero, chunk))
        q4.append(jnp.concatenate(parts, axis=0))

    rows4 = GROUP_A * QBLK

    def attn_tile(j, carry):
        off = pl.multiple_of(j * ts, ts)
        kt = key_ref[:, pl.ds(off, ts)]
        kpos = j * ts + col
        sel = ((kt > tau) | ((kt == tau) & (kpos <= tau_i))) & (kpos <= qpos)
        bias = jnp.where(sel, 0.0, -jnp.inf)
        bias4 = jnp.concatenate([bias] * GROUP_A, axis=0)
        k = ka_ref[pl.ds(off, ts), :]
        v = va_ref[pl.ds(off, ts), :]
        out = []
        for c in range(N_KV_A):
            m, l, acc = carry[c]
            s = lax.dot_general(q4[c], k, (((1,), (1,)), ((), ())), preferred_element_type=F32) + bias4
            m_new = jnp.maximum(m, jnp.max(s, axis=1, keepdims=True))
            m_safe = jnp.where(m_new == -jnp.inf, 0.0, m_new)
            p = jnp.exp(s - m_safe)
            alpha = jnp.exp(m - m_safe)
            l = alpha * l + jnp.sum(p, axis=1, keepdims=True)
            acc = alpha * acc + jnp.dot(p.astype(BF16), v, preferred_element_type=F32)
            out.append((m_new, l, acc))
        return tuple(out)

    init = tuple((jnp.full((rows4, 1), -jnp.inf, F32), jnp.zeros((rows4, 1), F32),
                  jnp.zeros((rows4, LANES), F32)) for _ in range(N_KV_A))
    res = lax.fori_loop(0, n_tiles, attn_tile, init)
    o = [res[c][2] / res[c][1] for c in range(N_KV_A)]
    for g in range(GROUP_A):
        rs = slice(g * QBLK, (g + 1) * QBLK)
        oa_ref[:, g * LANES:(g + 1) * LANES] = jnp.where(left, o[0][rs, :], o[1][rs, :])


def _dsa_prompt(qi, wi, qa, kib, kab, vab, batch, seq):
    nq = seq // QBLK
    k_sel = max(1, min(TOPK_MAX, seq // 4))
    idx_bits = max(1, int(seq - 1).bit_length())
    qspec = lambda n: pl.BlockSpec((QBLK, n), lambda b, i: (b * nq + i, 0))
    kspec = pl.BlockSpec((seq, LANES), lambda b, i: (b, 0))
    return pl.pallas_call(
        functools.partial(_dsa_prompt_kernel, k_sel, idx_bits),
        grid=(batch, nq),
        in_specs=[qspec(Q_I_W), qspec(LANES), qspec(Q_A_W), kspec, kspec, kspec],
        out_specs=qspec(Q_A_W),
        out_shape=jax.ShapeDtypeStruct((batch * seq, Q_A_W), F32),
        scratch_shapes=[pltpu.VMEM((QBLK, seq), I32)],
        compiler_params=_cparams("parallel", "arbitrary"),
        name="dsa_prompt",
    )(qi, wi, qa, kib, kab, vab)


def _strict_upper(n):
    r = lax.broadcasted_iota(I32, (n, n), 0)
    c = lax.broadcasted_iota(I32, (n, n), 1)
    return jnp.where(r > c, 1.0, 0.0).astype(BF16)


def _sb_tile(z, strict, upper, carry):
    soft = jnp.log(1.0 + jnp.exp(-jnp.abs(z)))
    log_beta = jnp.minimum(z, 0.0) - soft
    log_rest = log_beta - z
    if strict is not None:
        log_rest = jnp.where(strict, log_rest, 0.0)
    hi = log_rest.astype(BF16)
    lo = (log_rest - hi.astype(F32)).astype(BF16)
    later = (jnp.dot(hi, upper, preferred_element_type=F32)
             + jnp.dot(lo, upper, preferred_element_type=F32))
    a = jnp.exp(log_beta + later + carry)
    if strict is not None:
        a = jnp.where(strict, a, 0.0)
    new_carry = carry + later[:, 0:1] + log_rest[:, 0:1]
    return a, new_carry


SB_TS = 128


def _sb_prompt_kernel(qb_ref, kvb_ref, ob_ref, acc_ref, carry_ref):
    i = pl.program_id(1)
    ts = SB_TS
    rows = 2 * QBLK
    lane = lax.broadcasted_iota(I32, (QBLK, LANES), 1)
    left = lane < HEAD_DIM
    zero = jnp.zeros((QBLK, LANES), BF16)
    upper = _strict_upper(ts)
    row = lax.broadcasted_iota(I32, (rows, ts), 0)
    col = lax.broadcasted_iota(I32, (rows, ts), 1)
    qloc = jnp.where(row >= QBLK, row - QBLK, row)

    for p in range(N_HEADS_B // 2):
        chunk = qb_ref[:, p * LANES:(p + 1) * LANES]
        q2 = jnp.concatenate([jnp.where(left, chunk, zero), jnp.where(left, zero, chunk)], axis=0)
        acc_ref[...] = jnp.zeros_like(acc_ref)
        carry_ref[...] = jnp.zeros_like(carry_ref)

        def cond(state):
            j, go = state
            return jnp.logical_and(j >= 0, go)

        def body(state):
            j, _ = state
            off = pl.multiple_of(j * ts, ts)
            k = kvb_ref[pl.ds(off, ts), p * LANES:(p + 1) * LANES]
            v = kvb_ref[pl.ds(off, ts), Q_B_W + p * LANES:Q_B_W + (p + 1) * LANES]
            z = lax.dot_general(q2, k, (((1,), (1,)), ((), ())), preferred_element_type=F32)
            strict = (j * ts + col) < (i * QBLK + qloc)
            a, new_carry = _sb_tile(z, strict, upper, carry_ref[:, 0:1])
            acc_ref[...] += jnp.dot(a.astype(BF16), v, preferred_element_type=F32)
            carry_ref[...] = jnp.broadcast_to(new_carry, carry_ref.shape)
            return j - 1, jnp.max(new_carry) >= SB_STOP

        lax.while_loop(cond, body, (i, jnp.bool_(True)))
        acc = acc_ref[...]
        ob_ref[:, p * LANES:(p + 1) * LANES] = jnp.where(left, acc[0:QBLK, :], acc[QBLK:, :])


def _sb_prompt(qb, kvbb, batch, seq):
    nq = seq // QBLK
    qspec = pl.BlockSpec((QBLK, Q_B_W), lambda b, i: (b * nq + i, 0))
    return pl.pallas_call(
        _sb_prompt_kernel,
        grid=(batch, nq),
        in_specs=[qspec, pl.BlockSpec((seq, KV_B_W), lambda b, i: (b, 0))],
        out_specs=qspec,
        out_shape=jax.ShapeDtypeStruct((batch * seq, Q_B_W), F32),
        scratch_shapes=[pltpu.VMEM((2 * QBLK, LANES), F32), pltpu.VMEM((2 * QBLK, LANES), F32)],
        compiler_params=_cparams("parallel", "arbitrary"),
        name="sb_prompt",
    )(qb, kvbb)


TSLOTS = 8
SROWS = TSLOTS * N_HEADS_A


def _count_tile(n_pages):
    d = max(x for x in range(1, 9) if (n_pages + 1) % x == 0)
    return d


def _dsa_sample_kernel(k_sel, idx_bits, n_pages, page, ts, pt_ref,
                       qi_ref, w_ref, qa_ref, kip_ref, kvp_ref, kin_ref, kvn_ref,
                       o_ref, key_ref, stash_ref):
    del pt_ref
    p = pl.program_id(1)
    nk = (n_pages + 1) * page
    n_tiles = nk // ts
    qi = qi_ref[0]
    w = jnp.broadcast_to(w_ref[0], (SROWS, page))

    def scores(k):
        rel = lax.dot_general(qi, k, (((1,), (1,)), ((), ())), preferred_element_type=F32)
        return jnp.sum((w * jnp.maximum(rel, 0.0)).reshape(TSLOTS, N_IDX_HEADS, page), axis=1)

    off = pl.multiple_of(p * page, page)
    key_ref[:, pl.ds(off, page)] = _sort_key(scores(kip_ref[0].astype(BF16)))
    stash_ref[pl.ds(off, page), :] = kvp_ref[0].astype(BF16)

    @pl.when(p == n_pages - 1)
    def _():
        past = n_pages * page
        slot = lax.broadcasted_iota(I32, (TSLOTS, page), 0)
        u = lax.broadcasted_iota(I32, (TSLOTS, page), 1)
        sc = jnp.where(u <= slot, scores(kin_ref[0]), -jnp.inf)
        key_ref[:, past:past + page] = _sort_key(sc)
        stash_ref[past:past + page, :] = kvn_ref[0]

        tau, tau_i = _topk_threshold(key_ref, n_tiles, ts, TSLOTS, k_sel, idx_bits)
        qa = qa_ref[0]
        qpos = past + lax.broadcasted_iota(I32, (TSLOTS, ts), 0)
        col = lax.broadcasted_iota(I32, (TSLOTS, ts), 1)

        def attn_tile(j, carry):
            m, l, acc = carry
            toff = pl.multiple_of(j * ts, ts)
            kt = key_ref[:, pl.ds(toff, ts)]
            kpos = j * ts + col
            sel = ((kt > tau) | ((kt == tau) & (kpos <= tau_i))) & (kpos <= qpos)
            bias = jnp.where(sel, 0.0, -jnp.inf)
            bias_rows = jnp.concatenate(
                [jnp.broadcast_to(bias[t:t + 1, :], (N_HEADS_A, ts)) for t in range(TSLOTS)], axis=0)
            k = stash_ref[pl.ds(toff, ts), 0:LANES]
            v = stash_ref[pl.ds(toff, ts), LANES:2 * LANES]
            s = lax.dot_general(qa, k, (((1,), (1,)), ((), ())), preferred_element_type=F32) + bias_rows
            m_new = jnp.maximum(m, jnp.max(s, axis=1, keepdims=True))
            m_safe = jnp.where(m_new == -jnp.inf, 0.0, m_new)
            pr = jnp.exp(s - m_safe)
            alpha = jnp.exp(m - m_safe)
            l = alpha * l + jnp.sum(pr, axis=1, keepdims=True)
            acc = alpha * acc + jnp.dot(pr.astype(BF16), v, preferred_element_type=F32)
            return m_new, l, acc

        init = (jnp.full((SROWS, 1), -jnp.inf, F32), jnp.zeros((SROWS, 1), F32),
                jnp.zeros((SROWS, LANES), F32))
        _, l, acc = lax.fori_loop(0, n_tiles, attn_tile, init)
        o_ref[0] = acc / jnp.where(l > 0.0, l, 1.0)


def _dsa_sample(qi_st, w_st, qa_st, kidx_pool, kva_pool, kin_pad, kvn_pad, page_table, dec_seq):
    n_seq, n_pages = page_table.shape
    page = kidx_pool.shape[1]
    nk = (n_pages + 1) * page
    ts = _count_tile(n_pages) * page
    k_sel = max(1, min(TOPK_MAX, (n_pages * page + dec_seq) // 4))
    idx_bits = max(1, int(nk - 1).bit_length())
    seq_spec = lambda shp: pl.BlockSpec((1,) + shp, lambda b, p, pt: (b, 0, 0))
    pool_spec = lambda n: pl.BlockSpec((1, page, n), lambda b, p, pt: (pt[b, p], 0, 0))
    grid_spec = pltpu.PrefetchScalarGridSpec(
        num_scalar_prefetch=1,
        grid=(n_seq, n_pages),
        in_specs=[seq_spec((SROWS, IDX_DIM)), seq_spec((SROWS, 1)), seq_spec((SROWS, LANES)),
                  pool_spec(IDX_DIM), pool_spec(KV_A_W),
                  seq_spec((page, IDX_DIM)), seq_spec((page, KV_A_W))],
        out_specs=seq_spec((SROWS, LANES)),
        scratch_shapes=[pltpu.VMEM((TSLOTS, nk), I32), pltpu.VMEM((nk, KV_A_W), BF16)],
    )
    return pl.pallas_call(
        functools.partial(_dsa_sample_kernel, k_sel, idx_bits, n_pages, page, ts),
        grid_spec=grid_spec,
        out_shape=jax.ShapeDtypeStruct((n_seq, SROWS, LANES), F32),
        compiler_params=_cparams("parallel", "arbitrary"),
        name="dsa_sample",
    )(page_table, qi_st, w_st, qa_st, kidx_pool, kva_pool, kin_pad, kvn_pad)


def _sb_sample_kernel(n_pages, page, pt_ref, q_ref, kvp_ref, kvn_ref, o_ref, acc_ref, carry_ref, done_ref):
    del pt_ref
    s = pl.program_id(1)
    q = q_ref[0]
    upper = _strict_upper(page)

    def tile(k, v, strict):
        z = lax.dot_general(q, k, (((1,), (1,)), ((), ())), preferred_element_type=F32)
        a, new_carry = _sb_tile(z, strict, upper, carry_ref[:, 0:1])
        acc_ref[...] += jnp.dot(a.astype(BF16), v, preferred_element_type=F32)
        carry_ref[...] = jnp.broadcast_to(new_carry, carry_ref.shape)
        done_ref[0] = (jnp.max(new_carry) < SB_STOP).astype(I32)

    @pl.when(s == 0)
    def _():
        acc_ref[...] = jnp.zeros_like(acc_ref)
        carry_ref[...] = jnp.zeros_like(carry_ref)
        slot = lax.broadcasted_iota(I32, (SROWS, page), 0) // N_HEADS_B
        u = lax.broadcasted_iota(I32, (SROWS, page), 1)
        tile(kvn_ref[0, :, 0:Q_B_W], kvn_ref[0, :, Q_B_W:2 * Q_B_W], u < slot)

    @pl.when(jnp.logical_and(s > 0, done_ref[0] == 0))
    def _():
        tile(kvp_ref[0, :, 0:Q_B_W].astype(BF16), kvp_ref[0, :, Q_B_W:2 * Q_B_W].astype(BF16), None)

    @pl.when(s == n_pages)
    def _():
        r = lax.broadcasted_iota(I32, (SROWS, Q_B_W), 0)
        c = lax.broadcasted_iota(I32, (SROWS, Q_B_W), 1)
        own = (r % N_HEADS_B) == (c // HEAD_DIM)
        diag = jnp.where(own, acc_ref[...], 0.0)
        o_ref[0] = jnp.sum(diag.reshape(TSLOTS, N_HEADS_B, Q_B_W), axis=1)


def _sb_sample(qb_bd, kvb_pool, kvn_pad, page_table):
    n_seq, n_pages = page_table.shape
    page = kvb_pool.shape[1]
    seq_spec = lambda shp: pl.BlockSpec((1,) + shp, lambda b, s, pt: (b, 0, 0))
    pool_spec = pl.BlockSpec((1, page, KV_B_W),
                             lambda b, s, pt: (pt[b, jnp.minimum(n_pages - s, n_pages - 1)], 0, 0))
    grid_spec = pltpu.PrefetchScalarGridSpec(
        num_scalar_prefetch=1,
        grid=(n_seq, n_pages + 1),
        in_specs=[seq_spec((SROWS, Q_B_W)), pool_spec, seq_spec((page, KV_B_W))],
        out_specs=seq_spec((TSLOTS, Q_B_W)),
        scratch_shapes=[pltpu.VMEM((SROWS, Q_B_W), F32), pltpu.VMEM((SROWS, LANES), F32),
                        pltpu.SMEM((1,), I32)],
    )
    return pl.pallas_call(
        functools.partial(_sb_sample_kernel, n_pages, page),
        grid_spec=grid_spec,
        out_shape=jax.ShapeDtypeStruct((n_seq, TSLOTS, Q_B_W), F32),
        compiler_params=_cparams("parallel", "arbitrary"),
        name="sb_sample",
    )(page_table, qb_bd, kvb_pool, kvn_pad)


def _out_kernel(oa_ref, ob_ref, gate_ref, x_ref, wa_ref, wb_ref, wo_ref, nf_ref,
                wg_ref, wu_ref, wd_ref, nfin_ref, y_ref):
    ba = jnp.dot(oa_ref[...].astype(BF16), wa_ref[...], preferred_element_type=F32)
    bb = jnp.dot(ob_ref[...].astype(BF16), wb_ref[...], preferred_element_type=F32)
    merged = (gate_ref[:, 0:D_MODEL].astype(F32) * ba
              + gate_ref[:, D_MODEL:2 * D_MODEL].astype(F32) * bb)
    h1 = x_ref[...] + jnp.dot(merged.astype(BF16), wo_ref[...], preferred_element_type=F32)
    ms = jnp.mean(h1 * h1, axis=-1, keepdims=True)
    hn = ((h1 * lax.rsqrt(ms + RMS_EPS)) * nf_ref[...]).astype(BF16)
    acc = jnp.zeros_like(h1)
    for c in range(D_FF // FFN_CHUNK):
        sl = slice(c * FFN_CHUNK, (c + 1) * FFN_CHUNK)
        gg = jnp.dot(hn, wg_ref[:, sl], preferred_element_type=F32)
        uu = jnp.dot(hn, wu_ref[:, sl], preferred_element_type=F32)
        act = (gg * (1.0 / (1.0 + jnp.exp(-gg)))) * uu
        acc = acc + jnp.dot(act.astype(BF16), wd_ref[sl, :], preferred_element_type=F32)
    h2 = h1 + acc
    ms2 = jnp.mean(h2 * h2, axis=-1, keepdims=True)
    y_ref[...] = (h2 * lax.rsqrt(ms2 + RMS_EPS)) * nfin_ref[...]


def _out(oa, ob, gates, x2d, wts, tm):
    rows = x2d.shape[0]
    row_spec = lambda n: pl.BlockSpec((tm, n), lambda i: (i, 0))
    return pl.pallas_call(
        _out_kernel,
        grid=(rows // tm,),
        in_specs=[row_spec(Q_A_W), row_spec(Q_B_W), row_spec(2 * D_MODEL), row_spec(D_MODEL)]
                 + [_const_spec(w.shape) for w in wts],
        out_specs=row_spec(D_MODEL),
        out_shape=jax.ShapeDtypeStruct((rows, D_MODEL), F32),
        compiler_params=_cparams("parallel"),
        name="out",
    )(oa, ob, gates, x2d, *wts)


def _rope_tables(pos):
    half = HEAD_DIM // 2
    inv_freq = ROPE_THETA ** (-jnp.arange(half, dtype=F32) / half)
    ang = pos.astype(F32)[:, None] * inv_freq[None, :]
    cos, sin = jnp.cos(ang), jnp.sin(ang)
    cs = jnp.concatenate([cos, cos, cos, cos], axis=1)
    sn = jnp.concatenate([-sin, sin, -sin, sin], axis=1)
    return cs, sn


def _pack_w_in(w):
    d = w.shape[0]
    o = np.cumsum([0, Q_A_W, KV_A_W, Q_I_W, IDX_DIM, N_IDX_HEADS, Q_B_W, KV_B_W, 2 * D_MODEL])
    qa = w[:, o[0]:o[1]].reshape(d, N_HEADS_A, HEAD_DIM)[:, np.array(QA_HEAD_ORDER), :].reshape(d, Q_A_W)
    ka = w[:, o[1]:o[1] + N_KV_A * HEAD_DIM]
    va = w[:, o[1] + N_KV_A * HEAD_DIM:o[2]]
    qi = w[:, o[2]:o[3]]
    ki = w[:, o[3]:o[4]]
    wi = jnp.pad(w[:, o[4]:o[5]], ((0, 0), (0, LANES - N_IDX_HEADS)))
    qb = w[:, o[5]:o[6]]
    kvb = w[:, o[6]:o[7]]
    g = w[:, o[7]:o[8]]
    return jnp.concatenate([qa, ka, qi, ki, ki, va, qb, kvb, wi, g], axis=1).astype(BF16)


def kernel(x_prompt, x_sample, cache_kv_a, cache_k_idx, cache_kv_b, page_table, w_in, w_br_a, w_br_b,
           w_o, norm_attn, norm_ffn, w_ffn_gate, w_ffn_up, w_ffn_down, norm_final):
    depth = w_in.shape[0]
    assert depth == 1, "single-layer step"
    batch, seq, d = x_prompt.shape
    assert d == D_MODEL and seq % 256 == 0
    layer = 0

    w_packed = _pack_w_in(w_in[layer])
    wa = w_br_a[layer].reshape(N_HEADS_A, HEAD_DIM, D_MODEL)[np.array(QA_HEAD_ORDER)].reshape(Q_A_W, D_MODEL)
    out_wts = (wa.astype(BF16), w_br_b[layer].astype(BF16), w_o[layer].astype(BF16),
               norm_ffn[layer].reshape(1, D_MODEL),
               w_ffn_gate[layer].astype(BF16), w_ffn_up[layer].astype(BF16), w_ffn_down[layer].astype(BF16),
               norm_final.reshape(1, D_MODEL))
    gain = norm_attn[layer].reshape(1, D_MODEL)

    xp = x_prompt.reshape(batch * seq, D_MODEL)
    cs_p, sn_p = _rope_tables(jnp.arange(seq, dtype=I32))
    (kva_p, kidx_p, kvb_p, qa, qi, wi, qb, gates, kab, vab, kib, kvbb) = _project(
        xp, gain, w_packed, cs_p, sn_p, 256)
    oa = _dsa_prompt(qi, wi, qa, kib, kab, vab, batch, seq)
    ob = _sb_prompt(qb, kvbb, batch, seq)
    y_prompt = _out(oa, ob, gates, xp, out_wts, 256).reshape(batch, seq, D_MODEL)

    new_kv_a_p = kva_p.reshape(depth, batch, seq, 2, N_KV_A, HEAD_DIM)
    new_k_idx_p = kidx_p.reshape(depth, batch, seq, IDX_DIM)
    new_kv_b_p = kvb_p.reshape(depth, batch, seq, 2, N_HEADS_B, HEAD_DIM)

    n_seq, dec_seq, _ = x_sample.shape
    n_pages = page_table.shape[1]
    n_pool, page = cache_kv_a.shape[1], cache_kv_a.shape[2]
    past = n_pages * page
    rows_s = n_seq * dec_seq
    assert dec_seq <= TSLOTS and rows_s % 8 == 0 and page == LANES
    xs = x_sample.reshape(rows_s, D_MODEL)
    cs_s, sn_s = _rope_tables(past + jnp.arange(dec_seq, dtype=I32))
    cs_s, sn_s = jnp.tile(cs_s, (n_seq, 1)), jnp.tile(sn_s, (n_seq, 1))
    (kva_s, kidx_s, kvb_s, qa_s, qi_s, wi_s, qb_s, gates_s, kab_s, vab_s, kib_s, kvbb_s) = _project(
        xs, gain, w_packed, cs_s, sn_s, rows_s)

    def slots(a):
        return jnp.pad(a, ((0, 0), (0, TSLOTS - dec_seq)) + ((0, 0),) * (a.ndim - 2))

    def pad_page(a):
        return jnp.pad(a, ((0, 0), (0, page - dec_seq), (0, 0)))

    qi_st = slots(qi_s.reshape(n_seq, dec_seq, N_IDX_HEADS, IDX_DIM)).reshape(n_seq, SROWS, IDX_DIM)
    w_st = slots(wi_s[:, :N_IDX_HEADS].reshape(n_seq, dec_seq, N_IDX_HEADS)).reshape(n_seq, SROWS, 1)
    qa5 = qa_s.reshape(n_seq, dec_seq, GROUP_A, N_KV_A, HEAD_DIM).transpose(0, 1, 3, 2, 4)
    eye2 = jnp.eye(N_KV_A, dtype=qa_s.dtype)
    qa_st = qa5[:, :, :, :, None, :] * eye2[None, None, :, None, :, None]
    qa_st = slots(qa_st.reshape(n_seq, dec_seq, N_HEADS_A, LANES)).reshape(n_seq, SROWS, LANES)
    kin_pad = pad_page(kib_s[:, :IDX_DIM].reshape(n_seq, dec_seq, IDX_DIM))
    kvn_pad = pad_page(jnp.concatenate([kab_s, vab_s], axis=1).reshape(n_seq, dec_seq, KV_A_W))
    o_st = _dsa_sample(qi_st, w_st, qa_st, cache_k_idx[layer],
                       cache_kv_a[layer].reshape(n_pool, page, KV_A_W), kin_pad, kvn_pad, page_table, dec_seq)
    o6 = o_st.reshape(n_seq, TSLOTS, N_KV_A, GROUP_A, N_KV_A, HEAD_DIM)[:, :dec_seq]
    oa_s = jnp.stack([o6[:, :, c, :, c, :] for c in range(N_KV_A)], axis=2).reshape(rows_s, Q_A_W)

    qb4 = qb_s.reshape(n_seq, dec_seq, N_HEADS_B, HEAD_DIM)
    eye8 = jnp.eye(N_HEADS_B, dtype=qb_s.dtype)
    qb_bd = qb4[:, :, :, None, :] * eye8[None, None, :, :, None]
    qb_bd = slots(qb_bd.reshape(n_seq, dec_seq, N_HEADS_B, Q_B_W)).reshape(n_seq, SROWS, Q_B_W)
    kvbn_pad = pad_page(kvbb_s.reshape(n_seq, dec_seq, KV_B_W))
    ob_st = _sb_sample(qb_bd, cache_kv_b[layer].reshape(n_pool, page, KV_B_W), kvbn_pad, page_table)
    ob_s = ob_st[:, :dec_seq].reshape(rows_s, Q_B_W)

    out_wts_s = (w_br_a[layer].astype(BF16),) + out_wts[1:]
    y_sample = _out(oa_s, ob_s, gates_s, xs, out_wts_s, rows_s).reshape(n_seq, dec_seq, D_MODEL)

    return (y_prompt, y_sample, new_kv_a_p, new_k_idx_p, new_kv_b_p,
            kva_s.reshape(depth, n_seq, dec_seq, 2, N_KV_A, HEAD_DIM),
            kidx_s.reshape(depth, n_seq, dec_seq, IDX_DIM),
            kvb_s.reshape(depth, n_seq, dec_seq, 2, N_HEADS_B, HEAD_DIM))
```

```python
import functools

import jax
import jax.numpy as jnp
import numpy as np
from jax import lax
from jax.experimental import pallas as pl
from jax.experimental.pallas import tpu as pltpu

F32 = jnp.float32
BF16 = jnp.bfloat16
I32 = jnp.int32

D_MODEL = 1024
HEAD_DIM = 64
N_HEADS_A = 8
N_KV_A = 2
GROUP_A = N_HEADS_A // N_KV_A
N_IDX_HEADS = 8
IDX_DIM = 64
TOPK_MAX = 256
N_HEADS_B = 8
Q_A_W = N_HEADS_A * HEAD_DIM
KV_A_W = 2 * N_KV_A * HEAD_DIM
Q_I_W = N_IDX_HEADS * IDX_DIM
Q_B_W = N_HEADS_B * HEAD_DIM
KV_B_W = 2 * N_HEADS_B * HEAD_DIM
D_FF = -(-8 * D_MODEL // (3 * 256)) * 256
ROPE_THETA = 10000.0
RMS_EPS = 1e-6
QK_SCALE = HEAD_DIM ** -0.5
IDX_SCALE = IDX_DIM ** -0.5
HALF = HEAD_DIM // 2

LANES = 128
SUBLANES = 8
QBLK = 128
KBLK = 128
FFN_CHUNK = 256

TK_QA = 0
TK_QI = 512
TK_KA = 1024
TK_KI = 1152
TK_QB = 1280
TK_KB = 1792
TK_G = 2304
TK_W = 4352
FM_KA = 0
FM_VA = 128
FM_KI = 256
FM_WI = 320
FM_KB = 384
FM_VB = 896
FM_W = 1408
QA_HEAD_ORDER = (0, 4, 1, 5, 2, 6, 3, 7)

INT_MIN = -2 ** 31
NEGINF_KEY = -2139095041
IDX_ALL = 1 << 30
SB_STOP = -105.0

VMEM_LIMIT = 60 * 1024 * 1024

NT_DIMS = (((1,), (1,)), ((), ()))


def _cparams(*sem):
    return pltpu.CompilerParams(dimension_semantics=sem, vmem_limit_bytes=VMEM_LIMIT)


def _const_spec(shape):
    nd = len(shape)
    return pl.BlockSpec(shape, lambda *_: (0,) * nd, pipeline_mode=pl.Buffered(1))


def _dot_nt(a, b):
    return lax.dot_general(a, b, NT_DIMS, preferred_element_type=F32)


def _dot(a, b):
    return jnp.dot(a, b, preferred_element_type=F32)


def _project_kernel(x_ref, g_ref, wt_ref, wf_ref, cs_ref, sn_ref, csf_ref, snf_ref,
                    kva_ref, kidx_ref, kvb_ref,
                    qa_ref, qi_ref, wit_ref, qb_ref, gate_ref,
                    kab_ref, kib_ref, kbb_ref, vat_ref, vbt_ref):
    x = x_ref[...]
    ms = jnp.mean(x * x, axis=-1, keepdims=True)
    xn = (x * lax.rsqrt(ms + RMS_EPS)) * g_ref[...]
    xb = xn.astype(BF16)
    tm = x.shape[0]
    cs = cs_ref[...]
    sn = sn_ref[...]
    lane = lax.broadcasted_iota(I32, (tm, LANES), 1)
    first_half = (lane & HALF) == 0

    def mm(c0, n):
        return _dot(xb, wt_ref[:, c0:c0 + n])

    def rope(y):
        partner = jnp.where(first_half, pltpu.roll(y, LANES - HALF, 1), pltpu.roll(y, HALF, 1))
        return y * cs + partner * sn

    ya = mm(TK_QA, Q_A_W)
    for c in range(Q_A_W // LANES):
        sl = slice(c * LANES, (c + 1) * LANES)
        qa_ref[:, sl] = (rope(ya[:, sl]) * QK_SCALE).astype(BF16)
    yi = mm(TK_QI, Q_I_W)
    for c in range(Q_I_W // LANES):
        sl = slice(c * LANES, (c + 1) * LANES)
        qi_ref[:, sl] = rope(yi[:, sl]).astype(BF16)
    kab_ref[...] = rope(mm(TK_KA, LANES)).astype(BF16)
    kib_ref[...] = rope(mm(TK_KI, LANES)).astype(BF16)
    qb_ref[...] = (mm(TK_QB, Q_B_W) * QK_SCALE).astype(BF16)
    kbb_ref[...] = mm(TK_KB, Q_B_W).astype(BF16)
    for c in range(2 * D_MODEL // 512):
        sl = slice(c * 512, (c + 1) * 512)
        y = mm(TK_G + c * 512, 512)
        gate_ref[:, sl] = (1.0 / (1.0 + jnp.exp(-y))).astype(BF16)

    csf = csf_ref[...]
    snf = snf_ref[...]

    def fm(r0, n):
        return _dot_nt(wf_ref[r0:r0 + n, :], xb)

    def rope_fm(y):
        lo, hi = y[0:HALF], y[HALF:HEAD_DIM]
        return jnp.concatenate([lo * csf - hi * snf, hi * csf + lo * snf], axis=0)

    ka = fm(FM_KA, N_KV_A * HEAD_DIM)
    for c in range(N_KV_A):
        kva_ref[0, c * HEAD_DIM:(c + 1) * HEAD_DIM, :] = rope_fm(ka[c * HEAD_DIM:(c + 1) * HEAD_DIM])
    va = fm(FM_VA, N_KV_A * HEAD_DIM)
    kva_ref[0, N_KV_A * HEAD_DIM:2 * N_KV_A * HEAD_DIM, :] = va
    kidx_ref[0] = rope_fm(fm(FM_KI, IDX_DIM))
    wi = fm(FM_WI, HEAD_DIM)
    wit_ref[0] = (wi[0:N_IDX_HEADS] * (N_IDX_HEADS ** -0.5)) * IDX_SCALE
    kvb_ref[0, 0:Q_B_W, :] = fm(FM_KB, Q_B_W)
    vb = fm(FM_VB, Q_B_W)
    kvb_ref[0, Q_B_W:2 * Q_B_W, :] = vb
    for q in range(tm // KBLK):
        sl = slice(q * KBLK, (q + 1) * KBLK)
        vat_ref[0, q] = va[:, sl].astype(BF16)
        vbt_ref[0, q] = vb[:, sl].astype(BF16)


def _project(x2d, gain, w_tok, w_feat, cs, sn, csf, snf, groups, tm):
    rows = x2d.shape[0]
    t = rows // groups
    nb = t // tm
    nkb = tm // KBLK
    row_spec = lambda n: pl.BlockSpec((tm, n), lambda i: (i, 0))
    tab_spec = pl.BlockSpec((tm, LANES), lambda i: (i % nb, 0))
    tabf_spec = pl.BlockSpec((HALF, tm), lambda i: (0, i % nb))
    fm_spec = lambda n: pl.BlockSpec((1, n, tm), lambda i: (i // nb, 0, i % nb))
    blk_spec = lambda n: pl.BlockSpec((1, nkb, n, KBLK), lambda i: (i // nb, i % nb, 0, 0))
    out_specs = [fm_spec(KV_A_W), fm_spec(IDX_DIM), fm_spec(KV_B_W),
                 row_spec(Q_A_W), row_spec(Q_I_W), fm_spec(N_IDX_HEADS), row_spec(Q_B_W), row_spec(2 * D_MODEL),
                 row_spec(LANES), row_spec(LANES), row_spec(Q_B_W),
                 blk_spec(N_KV_A * HEAD_DIM), blk_spec(Q_B_W)]
    fm_shape = lambda n, dt: jax.ShapeDtypeStruct((groups, n, t), dt)
    tk_shape = lambda n, dt: jax.ShapeDtypeStruct((rows, n), dt)
    blk_shape = lambda n: jax.ShapeDtypeStruct((groups, t // KBLK, n, KBLK), BF16)
    out_shape = [fm_shape(KV_A_W, F32), fm_shape(IDX_DIM, F32), fm_shape(KV_B_W, F32),
                 tk_shape(Q_A_W, BF16), tk_shape(Q_I_W, BF16), fm_shape(N_IDX_HEADS, F32),
                 tk_shape(Q_B_W, BF16), tk_shape(2 * D_MODEL, BF16),
                 tk_shape(LANES, BF16), tk_shape(LANES, BF16), tk_shape(Q_B_W, BF16),
                 blk_shape(N_KV_A * HEAD_DIM), blk_shape(Q_B_W)]
    return pl.pallas_call(
        _project_kernel,
        grid=(rows // tm,),
        in_specs=[row_spec(D_MODEL), _const_spec((1, D_MODEL)), _const_spec((D_MODEL, TK_W)),
                  _const_spec((FM_W, D_MODEL)), tab_spec, tab_spec, tabf_spec, tabf_spec],
        out_specs=out_specs,
        out_shape=out_shape,
        compiler_params=_cparams("parallel"),
        name="project",
    )(x2d, gain, w_tok, w_feat, cs, sn, csf, snf)


def _sort_key(x):
    b = lax.bitcast_convert_type(x, I32)
    return b ^ ((b >> 31) & jnp.int32(0x7FFFFFFF))


def _count(key_ref, n_tiles, ts, key_axis, width, pred):
    if key_axis == 1:
        def body(j, acc):
            off = pl.multiple_of(j * ts, ts)
            kt = key_ref[:, pl.ds(off, ts)]
            kpos = j * ts + lax.broadcasted_iota(I32, (width, ts), 1)
            m = jnp.where(pred(kt, kpos), 1.0, 0.0)
            s = m[:, 0:LANES]
            for c in range(1, ts // LANES):
                s = s + m[:, c * LANES:(c + 1) * LANES]
            return acc + s

        acc = lax.fori_loop(0, n_tiles, body, jnp.zeros((width, LANES), F32))
        return jnp.sum(acc, axis=1, keepdims=True)

    def body(j, acc):
        off = pl.multiple_of(j * ts, ts)
        kt = key_ref[pl.ds(off, ts), :]
        kpos = j * ts + lax.broadcasted_iota(I32, (ts, width), 0)
        m = jnp.where(pred(kt, kpos), 1.0, 0.0)
        return acc + jnp.sum(m.reshape(ts // SUBLANES, SUBLANES, width), axis=0)

    acc = lax.fori_loop(0, n_tiles, body, jnp.zeros((SUBLANES, width), F32))
    return jnp.sum(acc, axis=0, keepdims=True)


def _topk_threshold(key_ref, n_tiles, ts, key_axis, width, k_sel, idx_bits):
    k_f = float(k_sel)
    vec = (width, 1) if key_axis == 1 else (1, width)
    count = functools.partial(_count, key_ref, n_tiles, ts, key_axis, width)

    def bis(b, r):
        cand = r + lax.shift_left(jnp.int32(1), jnp.int32(31) - b)
        return jnp.where(count(lambda kt, kp: kt >= cand) >= k_f, cand, r)

    tau = lax.fori_loop(0, 32, bis, jnp.full(vec, INT_MIN, I32))
    need = k_f - count(lambda kt, kp: kt > tau)
    tie = (count(lambda kt, kp: kt == tau) > need) & (tau > NEGINF_KEY)

    def resolve():
        def ib(b, x):
            c = x + lax.shift_left(jnp.int32(1), jnp.int32(idx_bits - 1) - b)
            cnt = count(lambda kt, kp: (kt == tau) & (kp < c))
            return jnp.where(cnt < need, c, x)

        x = lax.fori_loop(0, idx_bits, ib, jnp.zeros(vec, I32))
        return jnp.where(tie, x, IDX_ALL)

    any_tie = jnp.max(jnp.where(tie, 1.0, 0.0)) > 0.5
    tau_i = lax.cond(any_tie, resolve, lambda: jnp.full(vec, IDX_ALL, I32))
    return tau, tau_i


DSA_TS = 256


def _dsa_prompt_kernel(k_sel, idx_bits, qi_ref, wit_ref, qa_ref, ki_ref, ka_ref, vat_ref,
                       oa_ref, key_ref):
    i = pl.program_id(1)
    ts = DSA_TS
    n_tiles = (i * QBLK + QBLK + ts - 1) // ts
    lane = lax.broadcasted_iota(I32, (QBLK, LANES), 1)
    left = lane < HEAD_DIM
    krow = lax.broadcasted_iota(I32, (ts, QBLK), 0)
    qpos = i * QBLK + lax.broadcasted_iota(I32, (ts, QBLK), 1)

    zero = jnp.zeros((QBLK, LANES), BF16)
    qi_parts = []
    for p in range(Q_I_W // LANES):
        chunk = qi_ref[:, p * LANES:(p + 1) * LANES]
        qi_parts.append(jnp.where(left, chunk, zero))
        qi_parts.append(jnp.where(left, zero, chunk))
    qi_all = jnp.concatenate(qi_parts, axis=0)
    wt = wit_ref[0]

    def score_tile(j, _):
        off = pl.multiple_of(j * ts, ts)
        rel = _dot_nt(ki_ref[pl.ds(off, ts), :], qi_all)
        sc = jnp.zeros((ts, QBLK), F32)
        for h in range(N_IDX_HEADS):
            sc = sc + wt[h:h + 1, :] * jnp.maximum(rel[:, h * QBLK:(h + 1) * QBLK], 0.0)
        sc = jnp.where(j * ts + krow <= qpos, sc, -jnp.inf)
        key_ref[pl.ds(off, ts), :] = _sort_key(sc)
        return 0

    lax.fori_loop(0, n_tiles, score_tile, 0)
    tau, tau_i = _topk_threshold(key_ref, n_tiles, ts, 0, QBLK, k_sel, idx_bits)

    q4 = []
    for c in range(N_KV_A):
        parts = []
        for g in range(GROUP_A):
            chunk = qa_ref[:, g * LANES:(g + 1) * LANES]
            parts.append(jnp.where(left, chunk, zero) if c == 0 else jnp.where(left, zero, chunk))
        q4.append(jnp.concatenate(parts, axis=0))

    cols4 = GROUP_A * QBLK
    kb_per_tile = ts // KBLK

    def attn_tile(j, carry):
        off = pl.multiple_of(j * ts, ts)
        kt = key_ref[pl.ds(off, ts), :]
        kpos = j * ts + krow
        sel = ((kt > tau) | ((kt == tau) & (kpos <= tau_i))) & (kpos <= qpos)
        bias = jnp.where(sel, 0.0, -jnp.inf)
        bias4 = jnp.concatenate([bias] * GROUP_A, axis=1)
        k = ka_ref[pl.ds(off, ts), :]
        vts = [vat_ref[0, j * kb_per_tile + q] for q in range(kb_per_tile)]
        out = []
        for c in range(N_KV_A):
            m, l, acc = carry[c]
            s = _dot_nt(k, q4[c]) + bias4
            m_new = jnp.maximum(m, jnp.max(s, axis=0, keepdims=True))
            m_safe = jnp.where(m_new == -jnp.inf, 0.0, m_new)
            p = jnp.exp(s - m_safe)
            alpha = jnp.exp(m - m_safe)
            l = alpha * l + jnp.sum(p, axis=0, keepdims=True)
            pb = p.astype(BF16)
            acc = alpha * acc
            for q in range(kb_per_tile):
                acc = acc + _dot(vts[q], pb[q * KBLK:(q + 1) * KBLK, :])
            out.append((m_new, l, acc))
        return tuple(out)

    init = tuple((jnp.full((1, cols4), -jnp.inf, F32), jnp.zeros((1, cols4), F32),
                  jnp.zeros((LANES, cols4), F32)) for _ in range(N_KV_A))
    res = lax.fori_loop(0, n_tiles, attn_tile, init)
    o = [res[c][2] / res[c][1] for c in range(N_KV_A)]
    top = lax.broadcasted_iota(I32, (LANES, QBLK), 0) < HEAD_DIM
    for g in range(GROUP_A):
        cs = slice(g * QBLK, (g + 1) * QBLK)
        oa_ref[:, g * LANES:(g + 1) * LANES] = jnp.where(top, o[0][:, cs], o[1][:, cs]).T


def _dsa_prompt(qi, wit, qa, kib, kab, vat, batch, seq):
    nq = seq // QBLK
    k_sel = max(1, min(TOPK_MAX, seq // 4))
    idx_bits = max(1, int(seq - 1).bit_length())
    qspec = lambda n: pl.BlockSpec((QBLK, n), lambda b, i: (b * nq + i, 0))
    kspec = pl.BlockSpec((seq, LANES), lambda b, i: (b, 0))
    return pl.pallas_call(
        functools.partial(_dsa_prompt_kernel, k_sel, idx_bits),
        grid=(batch, nq),
        in_specs=[qspec(Q_I_W), pl.BlockSpec((1, N_IDX_HEADS, QBLK), lambda b, i: (b, 0, i)), qspec(Q_A_W),
                  kspec, kspec,
                  pl.BlockSpec((1, seq // KBLK, N_KV_A * HEAD_DIM, KBLK), lambda b, i: (b, 0, 0, 0))],
        out_specs=qspec(Q_A_W),
        out_shape=jax.ShapeDtypeStruct((batch * seq, Q_A_W), F32),
        scratch_shapes=[pltpu.VMEM((seq, QBLK), I32)],
        compiler_params=_cparams("parallel", "arbitrary"),
        name="dsa_prompt",
    )(qi, wit, qa, kib, kab, vat)


def _later_matrix(n, key_axis):
    r = lax.broadcasted_iota(I32, (n, n), 0)
    c = lax.broadcasted_iota(I32, (n, n), 1)
    return jnp.where(r > c if key_axis == 1 else c > r, 1.0, 0.0).astype(BF16)


def _sb_tile(z, strict, later_m, carry, key_axis):
    soft = jnp.log(1.0 + jnp.exp(-jnp.abs(z)))
    log_beta = jnp.minimum(z, 0.0) - soft
    log_rest = log_beta - z
    if strict is not None:
        log_rest = jnp.where(strict, log_rest, 0.0)
    hi = log_rest.astype(BF16)
    lo = (log_rest - hi.astype(F32)).astype(BF16)
    if key_axis == 1:
        later = _dot(hi, later_m) + _dot(lo, later_m)
        first_l, first_r = later[:, 0:1], log_rest[:, 0:1]
    else:
        later = _dot(later_m, hi) + _dot(later_m, lo)
        first_l, first_r = later[0:1, :], log_rest[0:1, :]
    a = jnp.exp(log_beta + later + carry)
    if strict is not None:
        a = jnp.where(strict, a, 0.0)
    return a, carry + first_l + first_r


SB_TS = 128


def _sb_prompt_kernel(qb_ref, kb_ref, vbt_ref, ob_ref, acc_ref, carry_ref):
    i = pl.program_id(1)
    ts = SB_TS
    cols = 2 * QBLK
    lane = lax.broadcasted_iota(I32, (QBLK, LANES), 1)
    left = lane < HEAD_DIM
    zero = jnp.zeros((QBLK, LANES), BF16)
    later_m = _later_matrix(ts, 0)
    krow = lax.broadcasted_iota(I32, (ts, cols), 0)
    qcol = lax.broadcasted_iota(I32, (ts, cols), 1)
    qpos = i * QBLK + jnp.where(qcol >= QBLK, qcol - QBLK, qcol)
    top = lax.broadcasted_iota(I32, (LANES, QBLK), 0) < HEAD_DIM

    for p in range(N_HEADS_B // 2):
        ps = slice(p * LANES, (p + 1) * LANES)
        chunk = qb_ref[:, ps]
        q2 = jnp.concatenate([jnp.where(left, chunk, zero), jnp.where(left, zero, chunk)], axis=0)
        acc_ref[...] = jnp.zeros_like(acc_ref)
        carry_ref[...] = jnp.zeros_like(carry_ref)

        def cond(state):
            j, go = state
            return jnp.logical_and(j >= 0, go)

        def body(state):
            j, _ = state
            off = pl.multiple_of(j * ts, ts)
            z = _dot_nt(kb_ref[pl.ds(off, ts), ps], q2)
            strict = (j * ts + krow) < qpos
            a, new_carry = _sb_tile(z, strict, later_m, carry_ref[0:1, :], 0)
            acc_ref[...] += _dot(vbt_ref[0, j, ps, :], a.astype(BF16))
            carry_ref[...] = jnp.broadcast_to(new_carry, carry_ref.shape)
            return j - 1, jnp.max(new_carry) >= SB_STOP

        lax.while_loop(cond, body, (i, jnp.bool_(True)))
        acc = acc_ref[...]
        ob_ref[:, ps] = jnp.where(top, acc[:, 0:QBLK], acc[:, QBLK:cols]).T


def _sb_prompt(qb, kbb, vbt, batch, seq):
    nq = seq // QBLK
    qspec = pl.BlockSpec((QBLK, Q_B_W), lambda b, i: (b * nq + i, 0))
    return pl.pallas_call(
        _sb_prompt_kernel,
        grid=(batch, nq),
        in_specs=[qspec, pl.BlockSpec((seq, Q_B_W), lambda b, i: (b, 0)),
                  pl.BlockSpec((1, seq // KBLK, Q_B_W, KBLK), lambda b, i: (b, 0, 0, 0))],
        out_specs=qspec,
        out_shape=jax.ShapeDtypeStruct((batch * seq, Q_B_W), F32),
        scratch_shapes=[pltpu.VMEM((LANES, 2 * QBLK), F32), pltpu.VMEM((SUBLANES, 2 * QBLK), F32)],
        compiler_params=_cparams("parallel", "arbitrary"),
        name="sb_prompt",
    )(qb, kbb, vbt)


TSLOTS = 8
SROWS = TSLOTS * N_HEADS_A


def _pages_per_tile(n_pages):
    return max(x for x in range(1, 9) if (n_pages + 1) % x == 0)


def _dsa_sample_kernel(k_sel, idx_bits, n_pages, page, ppt, pt_ref,
                       qi_ref, w_ref, qa_ref, kip_ref, kvp_ref, kin_ref, kvn_ref,
                       o_ref, key_ref, stash_ref):
    del pt_ref
    p = pl.program_id(1)
    ts = ppt * page
    n_tiles = (n_pages + 1) // ppt
    qi = qi_ref[0]
    w = jnp.broadcast_to(w_ref[0], (SROWS, page))

    def scores(kt):
        rel = _dot(qi, kt)
        return jnp.sum((w * jnp.maximum(rel, 0.0)).reshape(TSLOTS, N_IDX_HEADS, page), axis=1)

    off = pl.multiple_of(p * page, page)
    key_ref[:, pl.ds(off, page)] = _sort_key(scores(kip_ref[0].astype(BF16)))
    stash_ref[p] = kvp_ref[0].astype(BF16)

    @pl.when(p == n_pages - 1)
    def _():
        past = n_pages * page
        slot = lax.broadcasted_iota(I32, (TSLOTS, page), 0)
        u = lax.broadcasted_iota(I32, (TSLOTS, page), 1)
        sc = jnp.where(u <= slot, scores(kin_ref[0]), -jnp.inf)
        key_ref[:, past:past + page] = _sort_key(sc)
        stash_ref[n_pages] = kvn_ref[0]

        tau, tau_i = _topk_threshold(key_ref, n_tiles, ts, 1, TSLOTS, k_sel, idx_bits)
        qa = qa_ref[0]
        qpos = past + lax.broadcasted_iota(I32, (TSLOTS, ts), 0)
        col = lax.broadcasted_iota(I32, (TSLOTS, ts), 1)
        kfeat = N_KV_A * HEAD_DIM

        def attn_tile(j, carry):
            m, l, acc = carry
            toff = pl.multiple_of(j * ts, ts)
            kt = key_ref[:, pl.ds(toff, ts)]
            kpos = j * ts + col
            sel = ((kt > tau) | ((kt == tau) & (kpos <= tau_i))) & (kpos <= qpos)
            bias = jnp.where(sel, 0.0, -jnp.inf)
            bias_rows = jnp.concatenate(
                [jnp.broadcast_to(bias[t:t + 1, :], (N_HEADS_A, ts)) for t in range(TSLOTS)], axis=0)
            k = jnp.concatenate([stash_ref[j * ppt + q, 0:kfeat, :] for q in range(ppt)], axis=1)
            v = jnp.concatenate([stash_ref[j * ppt + q, kfeat:2 * kfeat, :] for q in range(ppt)], axis=1)
            s = _dot(qa, k) + bias_rows
            m_new = jnp.maximum(m, jnp.max(s, axis=1, keepdims=True))
            m_safe = jnp.where(m_new == -jnp.inf, 0.0, m_new)
            pr = jnp.exp(s - m_safe)
            alpha = jnp.exp(m - m_safe)
            l = alpha * l + jnp.sum(pr, axis=1, keepdims=True)
            acc = alpha * acc + _dot_nt(pr.astype(BF16), v)
            return m_new, l, acc

        init = (jnp.full((SROWS, 1), -jnp.inf, F32), jnp.zeros((SROWS, 1), F32),
                jnp.zeros((SROWS, LANES), F32))
        _, l, acc = lax.fori_loop(0, n_tiles, attn_tile, init)
        o_ref[0] = acc / jnp.where(l > 0.0, l, 1.0)


def _dsa_sample(qi_st, w_st, qa_st, kidx_pool, kva_pool, kin_t, kvn_t, page_table, dec_seq):
    n_seq, n_pages = page_table.shape
    page = kidx_pool.shape[2]
    nk = (n_pages + 1) * page
    ppt = _pages_per_tile(n_pages)
    k_sel = max(1, min(TOPK_MAX, (n_pages * page + dec_seq) // 4))
    idx_bits = max(1, int(nk - 1).bit_length())
    seq_spec = lambda shp: pl.BlockSpec((1,) + shp, lambda b, p, pt: (b, 0, 0))
    pool_spec = lambda n: pl.BlockSpec((1, n, page), lambda b, p, pt: (pt[b, p], 0, 0))
    grid_spec = pltpu.PrefetchScalarGridSpec(
        num_scalar_prefetch=1,
        grid=(n_seq, n_pages),
        in_specs=[seq_spec((SROWS, IDX_DIM)), seq_spec((SROWS, 1)), seq_spec((SROWS, LANES)),
                  pool_spec(IDX_DIM), pool_spec(KV_A_W),
                  seq_spec((IDX_DIM, page)), seq_spec((KV_A_W, page))],
        out_specs=seq_spec((SROWS, LANES)),
        scratch_shapes=[pltpu.VMEM((TSLOTS, nk), I32), pltpu.VMEM((n_pages + 1, KV_A_W, page), BF16)],
    )
    return pl.pallas_call(
        functools.partial(_dsa_sample_kernel, k_sel, idx_bits, n_pages, page, ppt),
        grid_spec=grid_spec,
        out_shape=jax.ShapeDtypeStruct((n_seq, SROWS, LANES), F32),
        compiler_params=_cparams("parallel", "arbitrary"),
        name="dsa_sample",
    )(page_table, qi_st, w_st, qa_st, kidx_pool, kva_pool, kin_t, kvn_t)


def _sb_sample_kernel(n_pages, page, pt_ref, q_ref, kvp_ref, kvn_ref, o_ref, acc_ref, carry_ref, done_ref):
    del pt_ref
    s = pl.program_id(1)
    q = q_ref[0]
    later_m = _later_matrix(page, 1)

    def tile(kt, vt, strict):
        a, new_carry = _sb_tile(_dot(q, kt), strict, later_m, carry_ref[:, 0:1], 1)
        acc_ref[...] += _dot_nt(a.astype(BF16), vt)
        carry_ref[...] = jnp.broadcast_to(new_carry, carry_ref.shape)
        done_ref[0] = (jnp.max(new_carry) < SB_STOP).astype(I32)

    @pl.when(s == 0)
    def _():
        acc_ref[...] = jnp.zeros_like(acc_ref)
        carry_ref[...] = jnp.zeros_like(carry_ref)
        slot = lax.broadcasted_iota(I32, (SROWS, page), 0) // N_HEADS_B
        u = lax.broadcasted_iota(I32, (SROWS, page), 1)
        tile(kvn_ref[0, 0:Q_B_W, :], kvn_ref[0, Q_B_W:2 * Q_B_W, :], u < slot)

    @pl.when(jnp.logical_and(s > 0, done_ref[0] == 0))
    def _():
        tile(kvp_ref[0, 0:Q_B_W, :].astype(BF16), kvp_ref[0, Q_B_W:2 * Q_B_W, :].astype(BF16), None)

    @pl.when(s == n_pages)
    def _():
        r = lax.broadcasted_iota(I32, (SROWS, Q_B_W), 0)
        c = lax.broadcasted_iota(I32, (SROWS, Q_B_W), 1)
        own = (r % N_HEADS_B) == (c // HEAD_DIM)
        diag = jnp.where(own, acc_ref[...], 0.0)
        o_ref[0] = jnp.sum(diag.reshape(TSLOTS, N_HEADS_B, Q_B_W), axis=1)


def _sb_sample(qb_bd, kvb_pool, kvn_t, page_table):
    n_seq, n_pages = page_table.shape
    page = kvb_pool.shape[2]
    seq_spec = lambda shp: pl.BlockSpec((1,) + shp, lambda b, s, pt: (b, 0, 0))
    pool_spec = pl.BlockSpec((1, KV_B_W, page),
                             lambda b, s, pt: (pt[b, jnp.minimum(n_pages - s, n_pages - 1)], 0, 0))
    grid_spec = pltpu.PrefetchScalarGridSpec(
        num_scalar_prefetch=1,
        grid=(n_seq, n_pages + 1),
        in_specs=[seq_spec((SROWS, Q_B_W)), pool_spec, seq_spec((KV_B_W, page))],
        out_specs=seq_spec((TSLOTS, Q_B_W)),
        scratch_shapes=[pltpu.VMEM((SROWS, Q_B_W), F32), pltpu.VMEM((SROWS, LANES), F32),
                        pltpu.SMEM((1,), I32)],
    )
    return pl.pallas_call(
        functools.partial(_sb_sample_kernel, n_pages, page),
        grid_spec=grid_spec,
        out_shape=jax.ShapeDtypeStruct((n_seq, TSLOTS, Q_B_W), F32),
        compiler_params=_cparams("parallel", "arbitrary"),
        name="sb_sample",
    )(page_table, qb_bd, kvb_pool, kvn_t)


def _out_kernel(oa_ref, ob_ref, gate_ref, x_ref, wa_ref, wb_ref, wo_ref, nf_ref,
                wg_ref, wu_ref, wd_ref, nfin_ref, y_ref):
    ba = _dot(oa_ref[...].astype(BF16), wa_ref[...])
    bb = _dot(ob_ref[...].astype(BF16), wb_ref[...])
    merged = (gate_ref[:, 0:D_MODEL].astype(F32) * ba
              + gate_ref[:, D_MODEL:2 * D_MODEL].astype(F32) * bb)
    h1 = x_ref[...] + _dot(merged.astype(BF16), wo_ref[...])
    ms = jnp.mean(h1 * h1, axis=-1, keepdims=True)
    hn = ((h1 * lax.rsqrt(ms + RMS_EPS)) * nf_ref[...]).astype(BF16)
    acc = jnp.zeros_like(h1)
    for c in range(D_FF // FFN_CHUNK):
        sl = slice(c * FFN_CHUNK, (c + 1) * FFN_CHUNK)
        gg = _dot(hn, wg_ref[:, sl])
        uu = _dot(hn, wu_ref[:, sl])
        act = (gg * (1.0 / (1.0 + jnp.exp(-gg)))) * uu
        acc = acc + _dot(act.astype(BF16), wd_ref[sl, :])
    h2 = h1 + acc
    ms2 = jnp.mean(h2 * h2, axis=-1, keepdims=True)
    y_ref[...] = (h2 * lax.rsqrt(ms2 + RMS_EPS)) * nfin_ref[...]


def _out(oa, ob, gates, x2d, wts, tm):
    rows = x2d.shape[0]
    row_spec = lambda n: pl.BlockSpec((tm, n), lambda i: (i, 0))
    return pl.pallas_call(
        _out_kernel,
        grid=(rows // tm,),
        in_specs=[row_spec(Q_A_W), row_spec(Q_B_W), row_spec(2 * D_MODEL), row_spec(D_MODEL)]
                 + [_const_spec(w.shape) for w in wts],
        out_specs=row_spec(D_MODEL),
        out_shape=jax.ShapeDtypeStruct((rows, D_MODEL), F32),
        compiler_params=_cparams("parallel"),
        name="out",
    )(oa, ob, gates, x2d, *wts)


def _rope_tables(pos):
    inv_freq = ROPE_THETA ** (-jnp.arange(HALF, dtype=F32) / HALF)
    ang = pos.astype(F32)[:, None] * inv_freq[None, :]
    cos, sin = jnp.cos(ang), jnp.sin(ang)
    cs = jnp.concatenate([cos, cos, cos, cos], axis=1)
    sn = jnp.concatenate([-sin, sin, -sin, sin], axis=1)
    return cs, sn, cos.T, sin.T


def _pack_w_in(w):
    d = w.shape[0]
    o = np.cumsum([0, Q_A_W, KV_A_W, Q_I_W, IDX_DIM, N_IDX_HEADS, Q_B_W, KV_B_W, 2 * D_MODEL])
    kw = N_KV_A * HEAD_DIM
    qa = w[:, o[0]:o[1]].reshape(d, N_HEADS_A, HEAD_DIM)[:, np.array(QA_HEAD_ORDER), :].reshape(d, Q_A_W)
    ka, va = w[:, o[1]:o[1] + kw], w[:, o[1] + kw:o[2]]
    qi, ki, wi, qb = w[:, o[2]:o[3]], w[:, o[3]:o[4]], w[:, o[4]:o[5]], w[:, o[5]:o[6]]
    kb, vb = w[:, o[6]:o[6] + Q_B_W], w[:, o[6] + Q_B_W:o[7]]
    g = w[:, o[7]:o[8]]
    w_tok = jnp.concatenate([qa, qi, ka, ki, ki, qb, kb, g], axis=1).astype(BF16)
    wi_pad = jnp.pad(wi, ((0, 0), (0, HEAD_DIM - N_IDX_HEADS)))
    w_feat = jnp.concatenate([ka, va, ki, wi_pad, kb, vb], axis=1).T.astype(BF16)
    return w_tok, w_feat


def kernel(x_prompt, x_sample, cache_kv_a, cache_k_idx, cache_kv_b, page_table, w_in, w_br_a, w_br_b,
           w_o, norm_attn, norm_ffn, w_ffn_gate, w_ffn_up, w_ffn_down, norm_final):
    depth = w_in.shape[0]
    assert depth == 1, "single-layer step"
    batch, seq, d = x_prompt.shape
    assert d == D_MODEL and seq % 256 == 0
    layer = 0

    w_tok, w_feat = _pack_w_in(w_in[layer])
    wa = w_br_a[layer].reshape(N_HEADS_A, HEAD_DIM, D_MODEL)[np.array(QA_HEAD_ORDER)].reshape(Q_A_W, D_MODEL)
    out_wts = (wa.astype(BF16), w_br_b[layer].astype(BF16), w_o[layer].astype(BF16),
               norm_ffn[layer].reshape(1, D_MODEL),
               w_ffn_gate[layer].astype(BF16), w_ffn_up[layer].astype(BF16), w_ffn_down[layer].astype(BF16),
               norm_final.reshape(1, D_MODEL))
    gain = norm_attn[layer].reshape(1, D_MODEL)

    xp = x_prompt.reshape(batch * seq, D_MODEL)
    (kva_p, kidx_p, kvb_p, qa, qi, wit, qb, gates, kab, kib, kbb, vat, vbt) = _project(
        xp, gain, w_tok, w_feat, *_rope_tables(jnp.arange(seq, dtype=I32)), batch, 256)
    oa = _dsa_prompt(qi, wit, qa, kib, kab, vat, batch, seq)
    ob = _sb_prompt(qb, kbb, vbt, batch, seq)
    y_prompt = _out(oa, ob, gates, xp, out_wts, 256).reshape(batch, seq, D_MODEL)

    new_kv_a_p = kva_p.reshape(depth, batch, 2, N_KV_A, HEAD_DIM, seq).transpose(0, 1, 5, 2, 3, 4)
    new_k_idx_p = kidx_p.reshape(depth, batch, IDX_DIM, seq).transpose(0, 1, 3, 2)
    new_kv_b_p = kvb_p.reshape(depth, batch, 2, N_HEADS_B, HEAD_DIM, seq).transpose(0, 1, 5, 2, 3, 4)

    n_seq, dec_seq, _ = x_sample.shape
    n_pages = page_table.shape[1]
    n_pool, page = cache_kv_a.shape[1], cache_kv_a.shape[2]
    past = n_pages * page
    rows_s = n_seq * dec_seq
    assert dec_seq <= TSLOTS and rows_s % KBLK == 0 and page == LANES
    xs = x_sample.reshape(rows_s, D_MODEL)
    cs_s, sn_s, csf_s, snf_s = _rope_tables(past + jnp.arange(dec_seq, dtype=I32))
    tabs_s = (jnp.tile(cs_s, (n_seq, 1)), jnp.tile(sn_s, (n_seq, 1)),
              jnp.tile(csf_s, (1, n_seq)), jnp.tile(snf_s, (1, n_seq)))
    (kva_s, kidx_s, kvb_s, qa_s, qi_s, wit_s, qb_s, gates_s, _, _, _, _, _) = _project(
        xs, gain, w_tok, w_feat, *tabs_s, 1, rows_s)

    def slots(a):
        return jnp.pad(a, ((0, 0), (0, TSLOTS - dec_seq)) + ((0, 0),) * (a.ndim - 2))

    def new_tile(a):
        t = a[0].reshape(a.shape[1], n_seq, dec_seq).transpose(1, 0, 2)
        return jnp.pad(t, ((0, 0), (0, 0), (0, page - dec_seq))).astype(BF16)

    qi_st = slots(qi_s.reshape(n_seq, dec_seq, N_IDX_HEADS, IDX_DIM)).reshape(n_seq, SROWS, IDX_DIM)
    w_st = slots(wit_s[0].T.reshape(n_seq, dec_seq, N_IDX_HEADS)).reshape(n_seq, SROWS, 1)
    qa5 = qa_s.reshape(n_seq, dec_seq, GROUP_A, N_KV_A, HEAD_DIM).transpose(0, 1, 3, 2, 4)
    eye2 = jnp.eye(N_KV_A, dtype=qa_s.dtype)
    qa_st = qa5[:, :, :, :, None, :] * eye2[None, None, :, None, :, None]
    qa_st = slots(qa_st.reshape(n_seq, dec_seq, N_HEADS_A, LANES)).reshape(n_seq, SROWS, LANES)
    kidx_pool = cache_k_idx[layer].transpose(0, 2, 1)
    kva_pool = cache_kv_a[layer].transpose(0, 2, 3, 4, 1).reshape(n_pool, KV_A_W, page)
    kvb_pool = cache_kv_b[layer].transpose(0, 2, 3, 4, 1).reshape(n_pool, KV_B_W, page)
    o_st = _dsa_sample(qi_st, w_st, qa_st, kidx_pool, kva_pool, new_tile(kidx_s), new_tile(kva_s),
                       page_table, dec_seq)
    o6 = o_st.reshape(n_seq, TSLOTS, N_KV_A, GROUP_A, N_KV_A, HEAD_DIM)[:, :dec_seq]
    oa_s = jnp.stack([o6[:, :, c, :, c, :] for c in range(N_KV_A)], axis=2).reshape(rows_s, Q_A_W)

    qb4 = qb_s.reshape(n_seq, dec_seq, N_HEADS_B, HEAD_DIM)
    eye8 = jnp.eye(N_HEADS_B, dtype=qb_s.dtype)
    qb_bd = qb4[:, :, :, None, :] * eye8[None, None, :, :, None]
    qb_bd = slots(qb_bd.reshape(n_seq, dec_seq, N_HEADS_B, Q_B_W)).reshape(n_seq, SROWS, Q_B_W)
    ob_st = _sb_sample(qb_bd, kvb_pool, new_tile(kvb_s), page_table)
    ob_s = ob_st[:, :dec_seq].reshape(rows_s, Q_B_W)

    out_wts_s = (w_br_a[layer].astype(BF16),) + out_wts[1:]
    y_sample = _out(oa_s, ob_s, gates_s, xs, out_wts_s, rows_s).reshape(n_seq, dec_seq, D_MODEL)

    return (y_prompt, y_sample, new_kv_a_p, new_k_idx_p, new_kv_b_p,
            kva_s[0].T.reshape(depth, n_seq, dec_seq, 2, N_KV_A, HEAD_DIM),
            kidx_s[0].T.reshape(depth, n_seq, dec_seq, IDX_DIM),
            kvb_s[0].T.reshape(depth, n_seq, dec_seq, 2, N_HEADS_B, HEAD_DIM))
```

```python
import functools

import jax
import jax.numpy as jnp
import numpy as np
from jax import lax
from jax.experimental import pallas as pl
from jax.experimental.pallas import tpu as pltpu

F32 = jnp.float32
BF16 = jnp.bfloat16
I32 = jnp.int32

D_MODEL = 1024
HEAD_DIM = 64
N_HEADS_A = 8
N_KV_A = 2
GROUP_A = N_HEADS_A // N_KV_A
N_IDX_HEADS = 8
IDX_DIM = 64
TOPK_MAX = 256
N_HEADS_B = 8
Q_A_W = N_HEADS_A * HEAD_DIM
KV_A_W = 2 * N_KV_A * HEAD_DIM
Q_I_W = N_IDX_HEADS * IDX_DIM
Q_B_W = N_HEADS_B * HEAD_DIM
KV_B_W = 2 * N_HEADS_B * HEAD_DIM
D_FF = -(-8 * D_MODEL // (3 * 256)) * 256
ROPE_THETA = 10000.0
RMS_EPS = 1e-6
QK_SCALE = HEAD_DIM ** -0.5
IDX_SCALE = IDX_DIM ** -0.5
HALF = HEAD_DIM // 2

LANES = 128
SUBLANES = 8
QBLK = 128
KBLK = 128
FFN_CHUNK = 256
ROW_TILE = 512

TK_QA = 0
TK_QI = 512
TK_KA = 1024
TK_KI = 1152
TK_QB = 1280
TK_KB = 1792
TK_G = 2304
TK_W = 4352
FM_KA = 0
FM_VA = 128
FM_KI = 256
FM_WI = 320
FM_KB = 384
FM_VB = 896
FM_W = 1408
QA_HEAD_ORDER = (0, 4, 1, 5, 2, 6, 3, 7)

NEGINF_KEY = -2139095041
MIN_NORMAL = 2.0 ** -126
IDX_ALL = 1 << 30
SB_STOP = -105.0

VMEM_LIMIT = 60 * 1024 * 1024

NT_DIMS = (((1,), (1,)), ((), ()))


def _cparams(*sem):
    return pltpu.CompilerParams(dimension_semantics=sem, vmem_limit_bytes=VMEM_LIMIT)


def _const_spec(shape):
    nd = len(shape)
    return pl.BlockSpec(shape, lambda *_: (0,) * nd, pipeline_mode=pl.Buffered(1))


def _dot_nt(a, b):
    return lax.dot_general(a, b, NT_DIMS, preferred_element_type=F32)


def _dot(a, b):
    return jnp.dot(a, b, preferred_element_type=F32)


def _project_kernel(x_ref, g_ref, wt_ref, wf_ref, cs_ref, sn_ref, csf_ref, snf_ref,
                    kva_ref, kidx_ref, kvb_ref,
                    qa_ref, qi_ref, wit_ref, qb_ref, gate_ref,
                    kab_ref, kib_ref, kbb_ref, vat_ref, vbt_ref):
    x = x_ref[...]
    ms = jnp.mean(x * x, axis=-1, keepdims=True)
    xn = (x * lax.rsqrt(ms + RMS_EPS)) * g_ref[...]
    xb = xn.astype(BF16)
    tm = x.shape[0]
    cs = cs_ref[...]
    sn = sn_ref[...]
    lane = lax.broadcasted_iota(I32, (tm, LANES), 1)
    first_half = (lane & HALF) == 0

    def mm(c0, n):
        return _dot(xb, wt_ref[:, c0:c0 + n])

    def rope(y):
        partner = jnp.where(first_half, pltpu.roll(y, LANES - HALF, 1), pltpu.roll(y, HALF, 1))
        return y * cs + partner * sn

    ya = mm(TK_QA, Q_A_W)
    for c in range(Q_A_W // LANES):
        sl = slice(c * LANES, (c + 1) * LANES)
        qa_ref[:, sl] = (rope(ya[:, sl]) * QK_SCALE).astype(BF16)
    yi = mm(TK_QI, Q_I_W)
    for c in range(Q_I_W // LANES):
        sl = slice(c * LANES, (c + 1) * LANES)
        qi_ref[:, sl] = rope(yi[:, sl]).astype(BF16)
    kab_ref[...] = rope(mm(TK_KA, LANES)).astype(BF16)
    kib_ref[...] = rope(mm(TK_KI, LANES)).astype(BF16)
    qb_ref[...] = (mm(TK_QB, Q_B_W) * QK_SCALE).astype(BF16)
    kbb_ref[...] = mm(TK_KB, Q_B_W).astype(BF16)
    for c in range(2 * D_MODEL // 512):
        sl = slice(c * 512, (c + 1) * 512)
        y = mm(TK_G + c * 512, 512)
        gate_ref[:, sl] = (1.0 / (1.0 + jnp.exp(-y))).astype(BF16)

    csf = csf_ref[...]
    snf = snf_ref[...]

    def fm(r0, n):
        return _dot_nt(wf_ref[r0:r0 + n, :], xb)

    def rope_fm(y):
        lo, hi = y[0:HALF], y[HALF:HEAD_DIM]
        return jnp.concatenate([lo * csf - hi * snf, hi * csf + lo * snf], axis=0)

    ka = fm(FM_KA, N_KV_A * HEAD_DIM)
    for c in range(N_KV_A):
        kva_ref[0, c * HEAD_DIM:(c + 1) * HEAD_DIM, :] = rope_fm(ka[c * HEAD_DIM:(c + 1) * HEAD_DIM])
    va = fm(FM_VA, N_KV_A * HEAD_DIM)
    kva_ref[0, N_KV_A * HEAD_DIM:2 * N_KV_A * HEAD_DIM, :] = va
    kidx_ref[0] = rope_fm(fm(FM_KI, IDX_DIM))
    wi = fm(FM_WI, HEAD_DIM)
    wit_ref[0] = (wi[0:N_IDX_HEADS] * (N_IDX_HEADS ** -0.5)) * IDX_SCALE
    kvb_ref[0, 0:Q_B_W, :] = fm(FM_KB, Q_B_W)
    vb = fm(FM_VB, Q_B_W)
    kvb_ref[0, Q_B_W:2 * Q_B_W, :] = vb
    for q in range(tm // KBLK):
        sl = slice(q * KBLK, (q + 1) * KBLK)
        vat_ref[0, q] = va[:, sl].astype(BF16)
        vbt_ref[0, q] = vb[:, sl].astype(BF16)


def _project(x2d, gain, w_tok, w_feat, cs, sn, csf, snf, groups, tm):
    rows = x2d.shape[0]
    t = rows // groups
    nb = t // tm
    nkb = tm // KBLK
    row_spec = lambda n: pl.BlockSpec((tm, n), lambda i: (i, 0))
    tab_spec = pl.BlockSpec((tm, LANES), lambda i: (i % nb, 0))
    tabf_spec = pl.BlockSpec((HALF, tm), lambda i: (0, i % nb))
    fm_spec = lambda n: pl.BlockSpec((1, n, tm), lambda i: (i // nb, 0, i % nb))
    blk_spec = lambda n: pl.BlockSpec((1, nkb, n, KBLK), lambda i: (i // nb, i % nb, 0, 0))
    out_specs = [fm_spec(KV_A_W), fm_spec(IDX_DIM), fm_spec(KV_B_W),
                 row_spec(Q_A_W), row_spec(Q_I_W), fm_spec(N_IDX_HEADS), row_spec(Q_B_W), row_spec(2 * D_MODEL),
                 row_spec(LANES), row_spec(LANES), row_spec(Q_B_W),
                 blk_spec(N_KV_A * HEAD_DIM), blk_spec(Q_B_W)]
    fm_shape = lambda n, dt: jax.ShapeDtypeStruct((groups, n, t), dt)
    tk_shape = lambda n, dt: jax.ShapeDtypeStruct((rows, n), dt)
    blk_shape = lambda n: jax.ShapeDtypeStruct((groups, t // KBLK, n, KBLK), BF16)
    out_shape = [fm_shape(KV_A_W, F32), fm_shape(IDX_DIM, F32), fm_shape(KV_B_W, F32),
                 tk_shape(Q_A_W, BF16), tk_shape(Q_I_W, BF16), fm_shape(N_IDX_HEADS, F32),
                 tk_shape(Q_B_W, BF16), tk_shape(2 * D_MODEL, BF16),
                 tk_shape(LANES, BF16), tk_shape(LANES, BF16), tk_shape(Q_B_W, BF16),
                 blk_shape(N_KV_A * HEAD_DIM), blk_shape(Q_B_W)]
    return pl.pallas_call(
        _project_kernel,
        grid=(rows // tm,),
        in_specs=[row_spec(D_MODEL), _const_spec((1, D_MODEL)), _const_spec((D_MODEL, TK_W)),
                  _const_spec((FM_W, D_MODEL)), tab_spec, tab_spec, tabf_spec, tabf_spec],
        out_specs=out_specs,
        out_shape=out_shape,
        compiler_params=_cparams("parallel"),
        name="project",
    )(x2d, gain, w_tok, w_feat, cs, sn, csf, snf)


def _key_to_float(k):
    return lax.bitcast_convert_type(k ^ ((k >> 31) & jnp.int32(0x7FFFFFFF)), F32)


def _fold_rows(x, op):
    n = x.shape[0]
    assert n % SUBLANES == 0 and (n // SUBLANES) & (n // SUBLANES - 1) == 0
    while n > SUBLANES:
        n //= 2
        x = op(x[:n], x[n:])
    return x


def _count(key_ref, n_tiles, ts, key_axis, width, pred):
    if key_axis == 1:
        def body(j, acc):
            off = pl.multiple_of(j * ts, ts)
            kt = key_ref[:, pl.ds(off, ts)]
            kpos = j * ts + lax.broadcasted_iota(I32, (width, ts), 1)
            m = jnp.where(pred(kt, kpos), 1.0, 0.0)
            s = m[:, 0:LANES]
            for c in range(1, ts // LANES):
                s = s + m[:, c * LANES:(c + 1) * LANES]
            return acc + s

        acc = lax.fori_loop(0, n_tiles, body, jnp.zeros((width, LANES), F32))
        return jnp.sum(acc, axis=1, keepdims=True)

    def body(j, acc):
        off = pl.multiple_of(j * ts, ts)
        kt = key_ref[pl.ds(off, ts), :]
        kpos = j * ts + lax.broadcasted_iota(I32, (ts, width), 0)
        return acc + _fold_rows(jnp.where(pred(kt, kpos), 1.0, 0.0), jnp.add)

    acc = lax.fori_loop(0, n_tiles, body, jnp.zeros((SUBLANES, width), F32))
    return jnp.sum(acc, axis=0, keepdims=True)


def _topk_threshold(key_ref, n_tiles, ts, key_axis, width, k_sel, idx_bits):
    k_f = float(k_sel)
    vec = (width, 1) if key_axis == 1 else (1, width)
    count = functools.partial(_count, key_ref, n_tiles, ts, key_axis, width)

    def bis(b, r):
        cand = r + lax.shift_left(jnp.int32(1), jnp.int32(31) - b)
        cand_f = _key_to_float(cand)
        return jnp.where(count(lambda st, kp: st >= cand_f) >= k_f, cand, r)

    tau = _key_to_float(lax.fori_loop(0, 32, bis, jnp.full(vec, NEGINF_KEY, I32)))
    tau = jnp.where(jnp.abs(tau) < MIN_NORMAL, 0.0, tau)
    need = k_f - count(lambda kt, kp: kt > tau)
    tie = (count(lambda kt, kp: kt == tau) > need) & (tau > -jnp.inf)

    def resolve():
        def ib(b, x):
            c = x + lax.shift_left(jnp.int32(1), jnp.int32(idx_bits - 1) - b)
            cnt = count(lambda kt, kp: (kt == tau) & (kp < c))
            return jnp.where(cnt < need, c, x)

        x = lax.fori_loop(0, idx_bits, ib, jnp.zeros(vec, I32))
        return jnp.where(tie, x, IDX_ALL)

    any_tie = jnp.max(jnp.where(tie, 1.0, 0.0)) > 0.5
    tau_i = lax.cond(any_tie, resolve, lambda: jnp.full(vec, IDX_ALL, I32))
    return tau, tau_i


DSA_TS = 256


def _dsa_prompt_kernel(k_sel, idx_bits, qi_ref, wit_ref, qa_ref, ki_ref, ka_ref, vat_ref,
                       oa_ref, key_ref, rel0_ref, rel1_ref, s0_ref, s1_ref, p0_ref, p1_ref):
    i = pl.program_id(1)
    ts = DSA_TS
    n_tiles = (i * QBLK + QBLK + ts - 1) // ts
    lane = lax.broadcasted_iota(I32, (QBLK, LANES), 1)
    left = lane < HEAD_DIM
    krow = lax.broadcasted_iota(I32, (ts, QBLK), 0)
    qpos = i * QBLK + lax.broadcasted_iota(I32, (ts, QBLK), 1)

    zero = jnp.zeros((QBLK, LANES), BF16)
    qi_parts = []
    for p in range(Q_I_W // LANES):
        chunk = qi_ref[:, p * LANES:(p + 1) * LANES]
        qi_parts.append(jnp.where(left, chunk, zero))
        qi_parts.append(jnp.where(left, zero, chunk))
    qi_all = jnp.concatenate(qi_parts, axis=0)
    wt = wit_ref[0]

    last = n_tiles - 1
    n_pairs = (n_tiles + 1) // 2

    def tile_off(j):
        return pl.multiple_of(jnp.minimum(j, last) * ts, ts)

    def rel_tile(j):
        return _dot_nt(ki_ref[pl.ds(tile_off(j), ts), :], qi_all)

    def score_step(j, cur_ref, nxt_ref):
        j = jnp.minimum(j, last)
        rel = cur_ref[...]
        nxt_ref[...] = rel_tile(j + 1)
        sc = jnp.zeros((ts, QBLK), F32)
        for h in range(N_IDX_HEADS):
            sc = sc + wt[h:h + 1, :] * jnp.maximum(rel[:, h * QBLK:(h + 1) * QBLK], 0.0)
        sc = jnp.where(j * ts + krow <= qpos, sc, -jnp.inf)
        key_ref[pl.ds(tile_off(j), ts), :] = sc

    def score_pair(jj, _):
        score_step(2 * jj, rel0_ref, rel1_ref)
        score_step(2 * jj + 1, rel1_ref, rel0_ref)
        return 0

    rel0_ref[...] = rel_tile(0)
    lax.fori_loop(0, n_pairs, score_pair, 0)
    tau, tau_i = _topk_threshold(key_ref, n_tiles, ts, 0, QBLK, k_sel, idx_bits)

    q4 = []
    for c in range(N_KV_A):
        parts = []
        for g in range(GROUP_A):
            chunk = qa_ref[:, g * LANES:(g + 1) * LANES]
            parts.append(jnp.where(left, chunk, zero) if c == 0 else jnp.where(left, zero, chunk))
        q4.append(jnp.concatenate(parts, axis=0))

    cols4 = GROUP_A * QBLK
    kb_per_tile = ts // KBLK

    def s_tile(j):
        k = ka_ref[pl.ds(tile_off(j), ts), :]
        return [_dot_nt(k, q4[c]) for c in range(N_KV_A)]

    def pv(j, p_ref, c):
        jb = jnp.clip(j, 0, last) * kb_per_tile
        out = _dot(vat_ref[0, jb], p_ref[c, 0:KBLK, :])
        for q in range(1, kb_per_tile):
            out = out + _dot(vat_ref[0, jb + q], p_ref[c, q * KBLK:(q + 1) * KBLK, :])
        return out

    def attn_step(j, carry, s_cur, s_nxt, p_cur, p_prev):
        kt = key_ref[pl.ds(tile_off(j), ts), :]
        kpos = j * ts + krow
        sel = ((kt > tau) | ((kt == tau) & (kpos <= tau_i))) & (kpos <= qpos)
        bias = jnp.where(sel, 0.0, -jnp.inf)
        bias4 = jnp.concatenate([bias] * GROUP_A, axis=1)
        nxt = s_tile(j + 1)
        out = []
        for c in range(N_KV_A):
            m, l, acc = carry[c]
            s = s_cur[c] + bias4
            s_nxt[c] = nxt[c]
            m_new = jnp.maximum(m, jnp.max(_fold_rows(s, jnp.maximum), axis=0, keepdims=True))
            m_safe = jnp.where(m_new == -jnp.inf, 0.0, m_new)
            p = jnp.exp(s - m_safe)
            alpha = jnp.exp(m - m_safe)
            l = alpha * l + jnp.sum(_fold_rows(p, jnp.add), axis=0, keepdims=True)
            acc = (acc + pv(j - 1, p_prev, c)) * alpha
            p_cur[c] = p.astype(BF16)
            out.append((m_new, l, acc))
        return tuple(out)

    def attn_pair(jj, carry):
        carry = attn_step(2 * jj, carry, s0_ref, s1_ref, p0_ref, p1_ref)
        return attn_step(2 * jj + 1, carry, s1_ref, s0_ref, p1_ref, p0_ref)

    first = s_tile(0)
    for c in range(N_KV_A):
        s0_ref[c] = first[c]
    p1_ref[...] = jnp.zeros_like(p1_ref)
    init = tuple((jnp.full((1, cols4), -jnp.inf, F32), jnp.zeros((1, cols4), F32),
                  jnp.zeros((LANES, cols4), F32)) for _ in range(N_KV_A))
    res = lax.fori_loop(0, n_pairs, attn_pair, init)
    o = [(res[c][2] + pv(2 * n_pairs - 1, p1_ref, c)) / res[c][1]
         for c in range(N_KV_A)]
    top = lax.broadcasted_iota(I32, (LANES, QBLK), 0) < HEAD_DIM
    for g in range(GROUP_A):
        cs = slice(g * QBLK, (g + 1) * QBLK)
        oa_ref[:, g * LANES:(g + 1) * LANES] = jnp.where(top, o[0][:, cs], o[1][:, cs]).T


def _dsa_prompt(qi, wit, qa, kib, kab, vat, batch, seq):
    nq = seq // QBLK
    k_sel = max(1, min(TOPK_MAX, seq // 4))
    idx_bits = max(1, int(seq - 1).bit_length())
    qspec = lambda n: pl.BlockSpec((QBLK, n), lambda b, i: (b * nq + i, 0))
    kspec = pl.BlockSpec((seq, LANES), lambda b, i: (b, 0))
    return pl.pallas_call(
        functools.partial(_dsa_prompt_kernel, k_sel, idx_bits),
        grid=(batch, nq),
        in_specs=[qspec(Q_I_W), pl.BlockSpec((1, N_IDX_HEADS, QBLK), lambda b, i: (b, 0, i)), qspec(Q_A_W),
                  kspec, kspec,
                  pl.BlockSpec((1, seq // KBLK, N_KV_A * HEAD_DIM, KBLK), lambda b, i: (b, 0, 0, 0))],
        out_specs=qspec(Q_A_W),
        out_shape=jax.ShapeDtypeStruct((batch * seq, Q_A_W), F32),
        scratch_shapes=[pltpu.VMEM((seq, QBLK), F32)]
                       + [pltpu.VMEM((DSA_TS, N_IDX_HEADS * QBLK), F32)] * 2
                       + [pltpu.VMEM((N_KV_A, DSA_TS, GROUP_A * QBLK), F32)] * 2
                       + [pltpu.VMEM((N_KV_A, DSA_TS, GROUP_A * QBLK), BF16)] * 2,
        compiler_params=_cparams("parallel", "arbitrary"),
        name="dsa_prompt",
    )(qi, wit, qa, kib, kab, vat)


def _later_matrix(n, key_axis):
    r = lax.broadcasted_iota(I32, (n, n), 0)
    c = lax.broadcasted_iota(I32, (n, n), 1)
    return jnp.where(r > c if key_axis == 1 else c > r, 1.0, 0.0).astype(BF16)


def _sb_tile(z, strict, later_m, carry, key_axis):
    soft = jnp.log(1.0 + jnp.exp(-jnp.abs(z)))
    log_beta = jnp.minimum(z, 0.0) - soft
    log_rest = log_beta - z
    if strict is not None:
        log_rest = jnp.where(strict, log_rest, 0.0)
    hi = log_rest.astype(BF16)
    lo = (log_rest - hi.astype(F32)).astype(BF16)
    if key_axis == 1:
        later = _dot(hi, later_m) + _dot(lo, later_m)
        first_l, first_r = later[:, 0:1], log_rest[:, 0:1]
    else:
        later = _dot(later_m, hi) + _dot(later_m, lo)
        first_l, first_r = later[0:1, :], log_rest[0:1, :]
    a = jnp.exp(log_beta + later + carry)
    if strict is not None:
        a = jnp.where(strict, a, 0.0)
    return a, carry + first_l + first_r


SB_TS = 128


def _sb_prompt_kernel(qb_ref, kb_ref, vbt_ref, ob_ref, acc_ref, carry_ref):
    i = pl.program_id(1)
    ts = SB_TS
    n_pairs = N_HEADS_B // 2
    pcols = 2 * QBLK
    cols = n_pairs * pcols
    lane = lax.broadcasted_iota(I32, (QBLK, LANES), 1)
    left = lane < HEAD_DIM
    zero = jnp.zeros((QBLK, LANES), BF16)
    later_m = _later_matrix(ts, 0)
    krow = lax.broadcasted_iota(I32, (ts, cols), 0)
    qpos = i * QBLK + (lax.broadcasted_iota(I32, (ts, cols), 1) & (QBLK - 1))
    top = lax.broadcasted_iota(I32, (LANES, QBLK), 0) < HEAD_DIM
    pair = [slice(p * LANES, (p + 1) * LANES) for p in range(n_pairs)]

    q2 = []
    for p in range(n_pairs):
        chunk = qb_ref[:, pair[p]]
        q2.append(jnp.concatenate([jnp.where(left, chunk, zero), jnp.where(left, zero, chunk)], axis=0))
    acc_ref[...] = jnp.zeros_like(acc_ref)
    carry_ref[...] = jnp.zeros_like(carry_ref)

    def cond(state):
        j, go = state
        return jnp.logical_and(j >= 0, go)

    def body(state):
        j, _ = state
        off = pl.multiple_of(j * ts, ts)
        z = jnp.concatenate([_dot_nt(kb_ref[pl.ds(off, ts), pair[p]], q2[p]) for p in range(n_pairs)],
                            axis=1)
        strict = (j * ts + krow) < qpos
        a, new_carry = _sb_tile(z, strict, later_m, carry_ref[0:1, :], 0)
        ab = a.astype(BF16)
        for p in range(n_pairs):
            acc_ref[p] += _dot(vbt_ref[0, j, pair[p], :], ab[:, p * pcols:(p + 1) * pcols])
        carry_ref[...] = jnp.broadcast_to(new_carry, carry_ref.shape)
        return j - 1, jnp.max(new_carry) >= SB_STOP

    lax.while_loop(cond, body, (i, jnp.bool_(True)))
    for p in range(n_pairs):
        acc = acc_ref[p]
        ob_ref[:, pair[p]] = jnp.where(top, acc[:, 0:QBLK], acc[:, QBLK:pcols]).T


def _sb_prompt(qb, kbb, vbt, batch, seq):
    nq = seq // QBLK
    qspec = pl.BlockSpec((QBLK, Q_B_W), lambda b, i: (b * nq + i, 0))
    return pl.pallas_call(
        _sb_prompt_kernel,
        grid=(batch, nq),
        in_specs=[qspec, pl.BlockSpec((seq, Q_B_W), lambda b, i: (b, 0)),
                  pl.BlockSpec((1, seq // KBLK, Q_B_W, KBLK), lambda b, i: (b, 0, 0, 0))],
        out_specs=qspec,
        out_shape=jax.ShapeDtypeStruct((batch * seq, Q_B_W), F32),
        scratch_shapes=[pltpu.VMEM((N_HEADS_B // 2, LANES, 2 * QBLK), F32),
                        pltpu.VMEM((SUBLANES, N_HEADS_B * QBLK), F32)],
        compiler_params=_cparams("parallel", "arbitrary"),
        name="sb_prompt",
    )(qb, kbb, vbt)


TSLOTS = 8
SROWS = TSLOTS * N_HEADS_A


def _largest_divisor(n, cap):
    return max(x for x in range(1, cap + 1) if n % x == 0)


def _page_copies(pool_hbm, pt_ref, seq, buf, slot, sem, n_pages, start):
    def body(p, _):
        cp = pltpu.make_async_copy(pool_hbm.at[pt_ref[seq, p]], buf.at[slot, p], sem.at[slot])
        if start:
            cp.start()
        else:
            cp.wait()
        return 0

    lax.fori_loop(0, n_pages, body, 0)


def _page_ring(pool_hbm, pt_ref, buf, sem, n_pages):
    b = pl.program_id(0)
    slot = b & 1

    @pl.when(b == 0)
    def _():
        _page_copies(pool_hbm, pt_ref, 0, buf, 0, sem, n_pages, True)

    @pl.when(b + 1 < pl.num_programs(0))
    def _():
        _page_copies(pool_hbm, pt_ref, b + 1, buf, 1 - slot, sem, n_pages, True)

    _page_copies(pool_hbm, pt_ref, b, buf, slot, sem, n_pages, False)
    return slot


def _dsa_sample_scores_kernel(n_pages, page, ppt, pt_ref, qi_ref, w_ref, kin_ref, kidx_hbm,
                              key_ref, buf, sem):
    slot = _page_ring(kidx_hbm, pt_ref, buf, sem, n_pages)
    ts = ppt * page
    qi = qi_ref[0]
    w = w_ref[0]

    def scores(kt):
        wrel = w * jnp.maximum(_dot(qi, kt), 0.0)
        return jnp.sum(wrel.reshape(TSLOTS, N_IDX_HEADS, kt.shape[1]), axis=1)

    for t in range(n_pages // ppt):
        kt = jnp.concatenate([buf[slot, t * ppt + q] for q in range(ppt)], axis=1).astype(BF16)
        key_ref[0, :, t * ts:(t + 1) * ts] = scores(kt)
    past = n_pages * page
    qslot = lax.broadcasted_iota(I32, (TSLOTS, page), 0)
    u = lax.broadcasted_iota(I32, (TSLOTS, page), 1)
    key_ref[0, :, past:past + page] = jnp.where(u <= qslot, scores(kin_ref[0]), -jnp.inf)


def _dsa_sample_select_kernel(k_sel, idx_bits, n_tiles, ts, key_ref, tau_ref, taui_ref):
    tau, tau_i = _topk_threshold(key_ref, n_tiles, ts, 1, key_ref.shape[0], k_sel, idx_bits)
    tau_ref[...] = jnp.broadcast_to(tau, tau_ref.shape)
    taui_ref[...] = jnp.broadcast_to(tau_i, taui_ref.shape)


def _dsa_sample_attend_kernel(n_pages, page, ppt, pt_ref, qa_ref, key_ref, tau_ref, taui_ref, kvn_ref, kva_hbm,
                              o_ref, buf, sem):
    slot = _page_ring(kva_hbm, pt_ref, buf, sem, n_pages)
    ts = ppt * page
    past = n_pages * page
    kfeat = N_KV_A * HEAD_DIM
    qa = qa_ref[0]
    tau = tau_ref[:, 0:1]
    tau_i = taui_ref[:, 0:1]

    def step(carry, kt, kpos0, k, v):
        m, l, acc = carry
        n = kt.shape[1]
        kpos = kpos0 + lax.broadcasted_iota(I32, (TSLOTS, n), 1)
        qpos = past + lax.broadcasted_iota(I32, (TSLOTS, n), 0)
        sel = ((kt > tau) | ((kt == tau) & (kpos <= tau_i))) & (kpos <= qpos)
        bias = jnp.where(sel, 0.0, -jnp.inf)
        bias_rows = jnp.concatenate(
            [jnp.broadcast_to(bias[t:t + 1, :], (N_HEADS_A, n)) for t in range(TSLOTS)], axis=0)
        s = _dot(qa, k) + bias_rows
        m_new = jnp.maximum(m, jnp.max(s, axis=1, keepdims=True))
        m_safe = jnp.where(m_new == -jnp.inf, 0.0, m_new)
        pr = jnp.exp(s - m_safe)
        alpha = jnp.exp(m - m_safe)
        l = alpha * l + jnp.sum(pr, axis=1, keepdims=True)
        acc = alpha * acc + _dot_nt(pr.astype(BF16), v)
        return m_new, l, acc

    def tile(t, carry):
        off = pl.multiple_of(t * ts, ts)
        k = jnp.concatenate([buf[slot, t * ppt + q, 0:kfeat, :] for q in range(ppt)], axis=1)
        v = jnp.concatenate([buf[slot, t * ppt + q, kfeat:2 * kfeat, :] for q in range(ppt)], axis=1)
        return step(carry, key_ref[0, :, pl.ds(off, ts)], t * ts, k.astype(BF16), v.astype(BF16))

    init = (jnp.full((SROWS, 1), -jnp.inf, F32), jnp.zeros((SROWS, 1), F32), jnp.zeros((SROWS, LANES), F32))
    carry = lax.fori_loop(0, n_pages // ppt, tile, init)
    _, l, acc = step(carry, key_ref[0, :, past:past + page], past,
                     kvn_ref[0, 0:kfeat, :], kvn_ref[0, kfeat:2 * kfeat, :])
    o_ref[0] = acc / jnp.where(l > 0.0, l, 1.0)


def _dsa_sample(qi_st, w_st, qa_st, kidx_pool, kva_pool, kin_t, kvn_t, page_table, dec_seq):
    n_seq, n_pages = page_table.shape
    page = kidx_pool.shape[2]
    nk = (n_pages + 1) * page
    ppt = _largest_divisor(n_pages, 8)
    sel_ts = _largest_divisor(n_pages + 1, 8) * page
    k_sel = max(1, min(TOPK_MAX, (n_pages * page + dec_seq) // 4))
    idx_bits = max(1, int(nk - 1).bit_length())
    seq_spec = lambda shp: pl.BlockSpec((1,) + shp, lambda b, pt: (b, 0, 0))
    hbm_spec = pl.BlockSpec(memory_space=pl.ANY)
    ring = lambda feat: [pltpu.VMEM((2, n_pages, feat, page), F32), pltpu.SemaphoreType.DMA((2,))]

    keys = pl.pallas_call(
        functools.partial(_dsa_sample_scores_kernel, n_pages, page, ppt),
        grid_spec=pltpu.PrefetchScalarGridSpec(
            num_scalar_prefetch=1, grid=(n_seq,),
            in_specs=[seq_spec((SROWS, IDX_DIM)), seq_spec((SROWS, 1)), seq_spec((IDX_DIM, page)), hbm_spec],
            out_specs=seq_spec((TSLOTS, nk)),
            scratch_shapes=ring(IDX_DIM)),
        out_shape=jax.ShapeDtypeStruct((n_seq, TSLOTS, nk), F32),
        compiler_params=_cparams("arbitrary"),
        name="dsa_sample_scores",
    )(page_table, qi_st, w_st, kin_t, kidx_pool)

    rows = n_seq * TSLOTS
    tau, tau_i = pl.pallas_call(
        functools.partial(_dsa_sample_select_kernel, k_sel, idx_bits, nk // sel_ts, sel_ts),
        grid=(1,),
        in_specs=[_const_spec((rows, nk))],
        out_specs=[pl.BlockSpec((rows, LANES), lambda i: (0, 0))] * 2,
        out_shape=[jax.ShapeDtypeStruct((rows, LANES), F32), jax.ShapeDtypeStruct((rows, LANES), I32)],
        compiler_params=_cparams("arbitrary"),
        name="dsa_sample_select",
    )(keys.reshape(rows, nk))

    row_spec = pl.BlockSpec((TSLOTS, LANES), lambda b, pt: (b, 0))
    return pl.pallas_call(
        functools.partial(_dsa_sample_attend_kernel, n_pages, page, ppt),
        grid_spec=pltpu.PrefetchScalarGridSpec(
            num_scalar_prefetch=1, grid=(n_seq,),
            in_specs=[seq_spec((SROWS, LANES)), seq_spec((TSLOTS, nk)), row_spec, row_spec,
                      seq_spec((KV_A_W, page)), hbm_spec],
            out_specs=seq_spec((SROWS, LANES)),
            scratch_shapes=ring(KV_A_W)),
        out_shape=jax.ShapeDtypeStruct((n_seq, SROWS, LANES), F32),
        compiler_params=_cparams("arbitrary"),
        name="dsa_sample_attend",
    )(page_table, qa_st, keys, tau, tau_i, kvn_t, kva_pool)


def _sb_sample_kernel(n_pages, page, pt_ref, q_ref, kvn_ref, kvb_hbm, o_ref, buf, sem, acc_ref, carry_ref):
    b = pl.program_id(0)
    q = q_ref[0]
    later_m = _later_matrix(page, 1)
    last = n_pages - 1

    def page_copy(pg):
        slot = (last - pg) & 1
        return pltpu.make_async_copy(kvb_hbm.at[pt_ref[b, pg]], buf.at[slot], sem.at[slot])

    def tile(kt, vt, strict):
        a, new_carry = _sb_tile(_dot(q, kt), strict, later_m, carry_ref[:, 0:1], 1)
        acc_ref[...] += _dot_nt(a.astype(BF16), vt)
        carry_ref[...] = jnp.broadcast_to(new_carry, carry_ref.shape)
        return jnp.max(new_carry) >= SB_STOP

    page_copy(last).start()
    acc_ref[...] = jnp.zeros_like(acc_ref)
    carry_ref[...] = jnp.zeros_like(carry_ref)
    qslot = lax.broadcasted_iota(I32, (SROWS, page), 0) // N_HEADS_B
    u = lax.broadcasted_iota(I32, (SROWS, page), 1)
    go0 = tile(kvn_ref[0, 0:Q_B_W, :], kvn_ref[0, Q_B_W:2 * Q_B_W, :], u < qslot)

    def cond(state):
        pg, go = state
        return jnp.logical_and(pg >= 0, go)

    def body(state):
        pg, _ = state
        page_copy(pg).wait()

        @pl.when(pg > 0)
        def _():
            page_copy(pg - 1).start()

        slot = (last - pg) & 1
        go = tile(buf[slot, 0:Q_B_W, :].astype(BF16), buf[slot, Q_B_W:2 * Q_B_W, :].astype(BF16), None)
        return pg - 1, go

    pg_end, _ = lax.while_loop(cond, body, (jnp.int32(last), go0))

    @pl.when(pg_end >= 0)
    def _():
        page_copy(pg_end).wait()

    r = lax.broadcasted_iota(I32, (SROWS, Q_B_W), 0)
    c = lax.broadcasted_iota(I32, (SROWS, Q_B_W), 1)
    own = (r % N_HEADS_B) == (c // HEAD_DIM)
    diag = jnp.where(own, acc_ref[...], 0.0)
    o_ref[0] = jnp.sum(diag.reshape(TSLOTS, N_HEADS_B, Q_B_W), axis=1)


def _sb_sample(qb_bd, kvb_pool, kvn_t, page_table):
    n_seq, n_pages = page_table.shape
    page = kvb_pool.shape[2]
    seq_spec = lambda shp: pl.BlockSpec((1,) + shp, lambda b, pt: (b, 0, 0))
    return pl.pallas_call(
        functools.partial(_sb_sample_kernel, n_pages, page),
        grid_spec=pltpu.PrefetchScalarGridSpec(
            num_scalar_prefetch=1, grid=(n_seq,),
            in_specs=[seq_spec((SROWS, Q_B_W)), seq_spec((KV_B_W, page)), pl.BlockSpec(memory_space=pl.ANY)],
            out_specs=seq_spec((TSLOTS, Q_B_W)),
            scratch_shapes=[pltpu.VMEM((2, KV_B_W, page), F32), pltpu.SemaphoreType.DMA((2,)),
                            pltpu.VMEM((SROWS, Q_B_W), F32), pltpu.VMEM((SROWS, LANES), F32)]),
        out_shape=jax.ShapeDtypeStruct((n_seq, TSLOTS, Q_B_W), F32),
        compiler_params=_cparams("arbitrary"),
        name="sb_sample",
    )(page_table, qb_bd, kvn_t, kvb_pool)


def _out_kernel(oa_ref, ob_ref, gate_ref, x_ref, wa_ref, wb_ref, wo_ref, nf_ref,
                wg_ref, wu_ref, wd_ref, nfin_ref, y_ref):
    ba = _dot(oa_ref[...].astype(BF16), wa_ref[...])
    bb = _dot(ob_ref[...].astype(BF16), wb_ref[...])
    merged = (gate_ref[:, 0:D_MODEL].astype(F32) * ba
              + gate_ref[:, D_MODEL:2 * D_MODEL].astype(F32) * bb)
    h1 = x_ref[...] + _dot(merged.astype(BF16), wo_ref[...])
    ms = jnp.mean(h1 * h1, axis=-1, keepdims=True)
    hn = ((h1 * lax.rsqrt(ms + RMS_EPS)) * nf_ref[...]).astype(BF16)
    acc = jnp.zeros_like(h1)
    for c in range(D_FF // FFN_CHUNK):
        sl = slice(c * FFN_CHUNK, (c + 1) * FFN_CHUNK)
        gg = _dot(hn, wg_ref[:, sl])
        uu = _dot(hn, wu_ref[:, sl])
        act = (gg * (1.0 / (1.0 + jnp.exp(-gg)))) * uu
        acc = acc + _dot(act.astype(BF16), wd_ref[sl, :])
    h2 = h1 + acc
    ms2 = jnp.mean(h2 * h2, axis=-1, keepdims=True)
    y_ref[...] = (h2 * lax.rsqrt(ms2 + RMS_EPS)) * nfin_ref[...]


def _out(oa, ob, gates, x2d, wts, tm):
    rows = x2d.shape[0]
    row_spec = lambda n: pl.BlockSpec((tm, n), lambda i: (i, 0))
    return pl.pallas_call(
        _out_kernel,
        grid=(rows // tm,),
        in_specs=[row_spec(Q_A_W), row_spec(Q_B_W), row_spec(2 * D_MODEL), row_spec(D_MODEL)]
                 + [_const_spec(w.shape) for w in wts],
        out_specs=row_spec(D_MODEL),
        out_shape=jax.ShapeDtypeStruct((rows, D_MODEL), F32),
        compiler_params=_cparams("parallel"),
        name="out",
    )(oa, ob, gates, x2d, *wts)


def _rope_tables(pos):
    inv_freq = ROPE_THETA ** (-jnp.arange(HALF, dtype=F32) / HALF)
    ang = pos.astype(F32)[:, None] * inv_freq[None, :]
    cos, sin = jnp.cos(ang), jnp.sin(ang)
    cs = jnp.concatenate([cos, cos, cos, cos], axis=1)
    sn = jnp.concatenate([-sin, sin, -sin, sin], axis=1)
    return cs, sn, cos.T, sin.T


def _pack_w_in(w):
    d = w.shape[0]
    o = np.cumsum([0, Q_A_W, KV_A_W, Q_I_W, IDX_DIM, N_IDX_HEADS, Q_B_W, KV_B_W, 2 * D_MODEL])
    kw = N_KV_A * HEAD_DIM
    qa = w[:, o[0]:o[1]].reshape(d, N_HEADS_A, HEAD_DIM)[:, np.array(QA_HEAD_ORDER), :].reshape(d, Q_A_W)
    ka, va = w[:, o[1]:o[1] + kw], w[:, o[1] + kw:o[2]]
    qi, ki, wi, qb = w[:, o[2]:o[3]], w[:, o[3]:o[4]], w[:, o[4]:o[5]], w[:, o[5]:o[6]]
    kb, vb = w[:, o[6]:o[6] + Q_B_W], w[:, o[6] + Q_B_W:o[7]]
    g = w[:, o[7]:o[8]]
    w_tok = jnp.concatenate([qa, qi, ka, ki, ki, qb, kb, g], axis=1).astype(BF16)
    wi_pad = jnp.pad(wi, ((0, 0), (0, HEAD_DIM - N_IDX_HEADS)))
    w_feat = jnp.concatenate([ka, va, ki, wi_pad, kb, vb], axis=1).T.astype(BF16)
    return w_tok, w_feat


def kernel(x_prompt, x_sample, cache_kv_a, cache_k_idx, cache_kv_b, page_table, w_in, w_br_a, w_br_b,
           w_o, norm_attn, norm_ffn, w_ffn_gate, w_ffn_up, w_ffn_down, norm_final):
    depth = w_in.shape[0]
    assert depth == 1, "single-layer step"
    batch, seq, d = x_prompt.shape
    assert d == D_MODEL and seq % ROW_TILE == 0
    layer = 0

    w_tok, w_feat = _pack_w_in(w_in[layer])
    wa = w_br_a[layer].reshape(N_HEADS_A, HEAD_DIM, D_MODEL)[np.array(QA_HEAD_ORDER)].reshape(Q_A_W, D_MODEL)
    out_wts = (wa.astype(BF16), w_br_b[layer].astype(BF16), w_o[layer].astype(BF16),
               norm_ffn[layer].reshape(1, D_MODEL),
               w_ffn_gate[layer].astype(BF16), w_ffn_up[layer].astype(BF16), w_ffn_down[layer].astype(BF16),
               norm_final.reshape(1, D_MODEL))
    gain = norm_attn[layer].reshape(1, D_MODEL)

    xp = x_prompt.reshape(batch * seq, D_MODEL)
    (kva_p, kidx_p, kvb_p, qa, qi, wit, qb, gates, kab, kib, kbb, vat, vbt) = _project(
        xp, gain, w_tok, w_feat, *_rope_tables(jnp.arange(seq, dtype=I32)), batch, ROW_TILE)
    oa = _dsa_prompt(qi, wit, qa, kib, kab, vat, batch, seq)
    ob = _sb_prompt(qb, kbb, vbt, batch, seq)
    y_prompt = _out(oa, ob, gates, xp, out_wts, ROW_TILE).reshape(batch, seq, D_MODEL)

    new_kv_a_p = kva_p.reshape(depth, batch, 2, N_KV_A, HEAD_DIM, seq).transpose(0, 1, 5, 2, 3, 4)
    new_k_idx_p = kidx_p.reshape(depth, batch, IDX_DIM, seq).transpose(0, 1, 3, 2)
    new_kv_b_p = kvb_p.reshape(depth, batch, 2, N_HEADS_B, HEAD_DIM, seq).transpose(0, 1, 5, 2, 3, 4)

    n_seq, dec_seq, _ = x_sample.shape
    n_pages = page_table.shape[1]
    n_pool, page = cache_kv_a.shape[1], cache_kv_a.shape[2]
    past = n_pages * page
    rows_s = n_seq * dec_seq
    assert dec_seq <= TSLOTS and rows_s % KBLK == 0 and page == LANES
    xs = x_sample.reshape(rows_s, D_MODEL)
    cs_s, sn_s, csf_s, snf_s = _rope_tables(past + jnp.arange(dec_seq, dtype=I32))
    tabs_s = (jnp.tile(cs_s, (n_seq, 1)), jnp.tile(sn_s, (n_seq, 1)),
              jnp.tile(csf_s, (1, n_seq)), jnp.tile(snf_s, (1, n_seq)))
    (kva_s, kidx_s, kvb_s, qa_s, qi_s, wit_s, qb_s, gates_s, _, _, _, _, _) = _project(
        xs, gain, w_tok, w_feat, *tabs_s, 1, rows_s)

    def slots(a):
        return jnp.pad(a, ((0, 0), (0, TSLOTS - dec_seq)) + ((0, 0),) * (a.ndim - 2))

    def new_tile(a):
        t = a[0].reshape(a.shape[1], n_seq, dec_seq).transpose(1, 0, 2)
        return jnp.pad(t, ((0, 0), (0, 0), (0, page - dec_seq))).astype(BF16)

    qi_st = slots(qi_s.reshape(n_seq, dec_seq, N_IDX_HEADS, IDX_DIM)).reshape(n_seq, SROWS, IDX_DIM)
    w_st = slots(wit_s[0].T.reshape(n_seq, dec_seq, N_IDX_HEADS)).reshape(n_seq, SROWS, 1)
    qa5 = qa_s.reshape(n_seq, dec_seq, GROUP_A, N_KV_A, HEAD_DIM).transpose(0, 1, 3, 2, 4)
    eye2 = jnp.eye(N_KV_A, dtype=qa_s.dtype)
    qa_st = qa5[:, :, :, :, None, :] * eye2[None, None, :, None, :, None]
    qa_st = slots(qa_st.reshape(n_seq, dec_seq, N_HEADS_A, LANES)).reshape(n_seq, SROWS, LANES)
    kidx_pool = cache_k_idx[layer].transpose(0, 2, 1)
    kva_pool = cache_kv_a[layer].transpose(0, 2, 3, 4, 1).reshape(n_pool, KV_A_W, page)
    kvb_pool = cache_kv_b[layer].transpose(0, 2, 3, 4, 1).reshape(n_pool, KV_B_W, page)
    o_st = _dsa_sample(qi_st, w_st, qa_st, kidx_pool, kva_pool, new_tile(kidx_s), new_tile(kva_s),
                       page_table, dec_seq)
    o6 = o_st.reshape(n_seq, TSLOTS, N_KV_A, GROUP_A, N_KV_A, HEAD_DIM)[:, :dec_seq]
    oa_s = jnp.stack([o6[:, :, c, :, c, :] for c in range(N_KV_A)], axis=2).reshape(rows_s, Q_A_W)

    qb4 = qb_s.reshape(n_seq, dec_seq, N_HEADS_B, HEAD_DIM)
    eye8 = jnp.eye(N_HEADS_B, dtype=qb_s.dtype)
    qb_bd = qb4[:, :, :, None, :] * eye8[None, None, :, :, None]
    qb_bd = slots(qb_bd.reshape(n_seq, dec_seq, N_HEADS_B, Q_B_W)).reshape(n_seq, SROWS, Q_B_W)
    ob_st = _sb_sample(qb_bd, kvb_pool, new_tile(kvb_s), page_table)
    ob_s = ob_st[:, :dec_seq].reshape(rows_s, Q_B_W)

    out_wts_s = (w_br_a[layer].astype(BF16),) + out_wts[1:]
    y_sample = _out(oa_s, ob_s, gates_s, xs, out_wts_s, rows_s).reshape(n_seq, dec_seq, D_MODEL)

    return (y_prompt, y_sample, new_kv_a_p, new_k_idx_p, new_kv_b_p,
            kva_s[0].T.reshape(depth, n_seq, dec_seq, 2, N_KV_A, HEAD_DIM),
            kidx_s[0].T.reshape(depth, n_seq, dec_seq, IDX_DIM),
            kvb_s[0].T.reshape(depth, n_seq, dec_seq, 2, N_HEADS_B, HEAD_DIM))
```

```python
import functools

import jax
import jax.numpy as jnp
import numpy as np
from jax import lax
from jax.experimental import pallas as pl
from jax.experimental.pallas import tpu as pltpu

F32 = jnp.float32
BF16 = jnp.bfloat16
I32 = jnp.int32

D_MODEL = 1024
HEAD_DIM = 64
N_HEADS_A = 8
N_KV_A = 2
GROUP_A = N_HEADS_A // N_KV_A
N_IDX_HEADS = 8
IDX_DIM = 64
TOPK_MAX = 256
N_HEADS_B = 8
Q_A_W = N_HEADS_A * HEAD_DIM
KV_A_W = 2 * N_KV_A * HEAD_DIM
Q_I_W = N_IDX_HEADS * IDX_DIM
Q_B_W = N_HEADS_B * HEAD_DIM
KV_B_W = 2 * N_HEADS_B * HEAD_DIM
D_FF = -(-8 * D_MODEL // (3 * 256)) * 256
ROPE_THETA = 10000.0
RMS_EPS = 1e-6
QK_SCALE = HEAD_DIM ** -0.5
IDX_SCALE = IDX_DIM ** -0.5
HALF = HEAD_DIM // 2
LOG2_E = 1.4426950408889634

LANES = 128
SUBLANES = 8
QBLK = 128
KBLK = 128
SUM_ROWS = 16
FFN_CHUNK = 256
ROW_TILE = 512

TK_QA = 0
TK_QI = 512
TK_KA = 1024
TK_KI = 1152
TK_QB = 1280
TK_KB = 1792
TK_G = 2304
TK_W = 4352
FM_KA = 0
FM_VA = 128
FM_KI = 256
FM_WI = 320
FM_KB = 384
FM_VB = 896
FM_W = 1408
QA_HEAD_ORDER = (0, 4, 1, 5, 2, 6, 3, 7)

NEGINF_KEY = -2139095041
POSINF_KEY = 0x7F800000
MIN_NORMAL_KEY = 0x00800000
IDX_ALL = 1 << 30
SB_STOP = -105.0

VMEM_LIMIT = 60 * 1024 * 1024

NT_DIMS = (((1,), (1,)), ((), ()))


def _cparams(*sem):
    return pltpu.CompilerParams(dimension_semantics=sem, vmem_limit_bytes=VMEM_LIMIT)


def _const_spec(shape):
    nd = len(shape)
    return pl.BlockSpec(shape, lambda *_: (0,) * nd, pipeline_mode=pl.Buffered(1))


def _dot_nt(a, b):
    return lax.dot_general(a, b, NT_DIMS, preferred_element_type=F32)


def _dot(a, b):
    return jnp.dot(a, b, preferred_element_type=F32)


def _project_kernel(x_ref, g_ref, wt_ref, wf_ref, cs_ref, sn_ref, csf_ref, snf_ref,
                    kva_ref, kidx_ref, kvb_ref,
                    qa_ref, qi_ref, wit_ref, qb_ref, gate_ref,
                    kab_ref, kib_ref, kbb_ref, vat_ref, vbt_ref):
    x = x_ref[...]
    ms = jnp.mean(x * x, axis=-1, keepdims=True)
    xn = (x * lax.rsqrt(ms + RMS_EPS)) * g_ref[...]
    xb = xn.astype(BF16)
    tm = x.shape[0]
    cs = cs_ref[...]
    sn = sn_ref[...]
    lane = lax.broadcasted_iota(I32, (tm, LANES), 1)
    first_half = (lane & HALF) == 0

    def mm(c0, n):
        return _dot(xb, wt_ref[:, c0:c0 + n])

    def rope(y):
        partner = jnp.where(first_half, pltpu.roll(y, LANES - HALF, 1), pltpu.roll(y, HALF, 1))
        return y * cs + partner * sn

    ya = mm(TK_QA, Q_A_W)
    for c in range(Q_A_W // LANES):
        sl = slice(c * LANES, (c + 1) * LANES)
        qa_ref[:, sl] = (rope(ya[:, sl]) * (QK_SCALE * LOG2_E)).astype(BF16)
    yi = mm(TK_QI, Q_I_W)
    for c in range(Q_I_W // LANES):
        sl = slice(c * LANES, (c + 1) * LANES)
        qi_ref[:, sl] = rope(yi[:, sl]).astype(BF16)
    kab_ref[...] = rope(mm(TK_KA, LANES)).astype(BF16)
    kib_ref[...] = rope(mm(TK_KI, LANES)).astype(BF16)
    qb_ref[...] = (mm(TK_QB, Q_B_W) * QK_SCALE).astype(BF16)
    kbb_ref[...] = mm(TK_KB, Q_B_W).astype(BF16)
    for c in range(2 * D_MODEL // 512):
        sl = slice(c * 512, (c + 1) * 512)
        y = mm(TK_G + c * 512, 512)
        gate_ref[:, sl] = (1.0 / (1.0 + jnp.exp(-y))).astype(BF16)

    csf = csf_ref[...]
    snf = snf_ref[...]

    def fm(r0, n):
        return _dot_nt(wf_ref[r0:r0 + n, :], xb)

    def rope_fm(y):
        lo, hi = y[0:HALF], y[HALF:HEAD_DIM]
        return jnp.concatenate([lo * csf - hi * snf, hi * csf + lo * snf], axis=0)

    ka = fm(FM_KA, N_KV_A * HEAD_DIM)
    for c in range(N_KV_A):
        kva_ref[0, c * HEAD_DIM:(c + 1) * HEAD_DIM, :] = rope_fm(ka[c * HEAD_DIM:(c + 1) * HEAD_DIM])
    va = fm(FM_VA, N_KV_A * HEAD_DIM)
    kva_ref[0, N_KV_A * HEAD_DIM:2 * N_KV_A * HEAD_DIM, :] = va
    kidx_ref[0] = rope_fm(fm(FM_KI, IDX_DIM))
    wi = fm(FM_WI, HEAD_DIM)
    wit_ref[0] = (wi[0:N_IDX_HEADS] * (N_IDX_HEADS ** -0.5)) * IDX_SCALE
    kvb_ref[0, 0:Q_B_W, :] = fm(FM_KB, Q_B_W)
    vb = fm(FM_VB, Q_B_W)
    kvb_ref[0, Q_B_W:2 * Q_B_W, :] = vb
    for q in range(tm // KBLK):
        sl = slice(q * KBLK, (q + 1) * KBLK)
        vat_ref[0, q] = va[:, sl].astype(BF16)
        vbt_ref[0, q] = vb[:, sl].astype(BF16)


def _project(x2d, gain, w_tok, w_feat, cs, sn, csf, snf, groups, tm):
    rows = x2d.shape[0]
    t = rows // groups
    nb = t // tm
    nkb = tm // KBLK
    row_spec = lambda n: pl.BlockSpec((tm, n), lambda i: (i, 0))
    tab_spec = pl.BlockSpec((tm, LANES), lambda i: (i % nb, 0))
    tabf_spec = pl.BlockSpec((HALF, tm), lambda i: (0, i % nb))
    fm_spec = lambda n: pl.BlockSpec((1, n, tm), lambda i: (i // nb, 0, i % nb))
    blk_spec = lambda n: pl.BlockSpec((1, nkb, n, KBLK), lambda i: (i // nb, i % nb, 0, 0))
    out_specs = [fm_spec(KV_A_W), fm_spec(IDX_DIM), fm_spec(KV_B_W),
                 row_spec(Q_A_W), row_spec(Q_I_W), fm_spec(N_IDX_HEADS), row_spec(Q_B_W), row_spec(2 * D_MODEL),
                 row_spec(LANES), row_spec(LANES), row_spec(Q_B_W),
                 blk_spec(N_KV_A * HEAD_DIM), blk_spec(Q_B_W)]
    fm_shape = lambda n, dt: jax.ShapeDtypeStruct((groups, n, t), dt)
    tk_shape = lambda n, dt: jax.ShapeDtypeStruct((rows, n), dt)
    blk_shape = lambda n: jax.ShapeDtypeStruct((groups, t // KBLK, n, KBLK), BF16)
    out_shape = [fm_shape(KV_A_W, F32), fm_shape(IDX_DIM, F32), fm_shape(KV_B_W, F32),
                 tk_shape(Q_A_W, BF16), tk_shape(Q_I_W, BF16), fm_shape(N_IDX_HEADS, F32),
                 tk_shape(Q_B_W, BF16), tk_shape(2 * D_MODEL, BF16),
                 tk_shape(LANES, BF16), tk_shape(LANES, BF16), tk_shape(Q_B_W, BF16),
                 blk_shape(N_KV_A * HEAD_DIM), blk_shape(Q_B_W)]
    return pl.pallas_call(
        _project_kernel,
        grid=(rows // tm,),
        in_specs=[row_spec(D_MODEL), _const_spec((1, D_MODEL)), _const_spec((D_MODEL, TK_W)),
                  _const_spec((FM_W, D_MODEL)), tab_spec, tab_spec, tabf_spec, tabf_spec],
        out_specs=out_specs,
        out_shape=out_shape,
        compiler_params=_cparams("parallel"),
        name="project",
    )(x2d, gain, w_tok, w_feat, cs, sn, csf, snf)


def _key_to_float(k):
    return lax.bitcast_convert_type(k ^ ((k >> 31) & jnp.int32(0x7FFFFFFF)), F32)


def _fold_rows(x, op):
    n = x.shape[0]
    assert n % SUBLANES == 0 and (n // SUBLANES) & (n // SUBLANES - 1) == 0
    while n > SUBLANES:
        n //= 2
        x = op(x[:n], x[n:])
    return x


def _count(key_ref, n_tiles, ts, key_axis, width, pred):
    if key_axis == 1:
        def body(j, acc):
            off = pl.multiple_of(j * ts, ts)
            kt = key_ref[:, pl.ds(off, ts)]
            kpos = j * ts + lax.broadcasted_iota(I32, (width, ts), 1)
            m = jnp.where(pred(kt, kpos), 1.0, 0.0)
            s = m[:, 0:LANES]
            for c in range(1, ts // LANES):
                s = s + m[:, c * LANES:(c + 1) * LANES]
            return acc + s

        acc = lax.fori_loop(0, n_tiles, body, jnp.zeros((width, LANES), F32))
        return jnp.sum(acc, axis=1, keepdims=True)

    def body(j, acc):
        off = pl.multiple_of(j * ts, ts)
        kt = key_ref[pl.ds(off, ts), :]
        kpos = j * ts + lax.broadcasted_iota(I32, (ts, width), 0)
        return acc + _fold_rows(jnp.where(pred(kt, kpos), 1.0, 0.0), jnp.add)

    acc = lax.fori_loop(0, n_tiles, body, jnp.zeros((SUBLANES, width), F32))
    return jnp.sum(acc, axis=0, keepdims=True)


def _topk_threshold(key_ref, n_tiles, ts, key_axis, width, k_sel, idx_bits):
    k_f = float(k_sel)
    vec = (width, 1) if key_axis == 1 else (1, width)
    count = functools.partial(_count, key_ref, n_tiles, ts, key_axis, width)

    def as_float(k):
        zero_band = (k >= -MIN_NORMAL_KEY) & (k < MIN_NORMAL_KEY)
        return _key_to_float(jnp.where(zero_band, 0, k))

    def bis(b, r):
        cand = r + lax.shift_left(jnp.int32(1), jnp.int32(31) - b)
        valid = (cand >= NEGINF_KEY) & (cand <= POSINF_KEY)
        cand_f = as_float(jnp.where(valid, cand, 0))
        return jnp.where(valid & (count(lambda st, kp: st >= cand_f) >= k_f), cand, r)

    tau = as_float(lax.fori_loop(0, 32, bis, jnp.full(vec, NEGINF_KEY, I32)))
    need = k_f - count(lambda kt, kp: kt > tau)
    tie = (count(lambda kt, kp: kt == tau) > need) & (tau > -jnp.inf)

    def resolve():
        def ib(b, x):
            c = x + lax.shift_left(jnp.int32(1), jnp.int32(idx_bits - 1) - b)
            cnt = count(lambda kt, kp: (kt == tau) & (kp < c))
            return jnp.where(cnt < need, c, x)

        x = lax.fori_loop(0, idx_bits, ib, jnp.zeros(vec, I32))
        return jnp.where(tie, x, IDX_ALL)

    any_tie = jnp.max(jnp.where(tie, 1.0, 0.0)) > 0.5
    tau_i = lax.cond(any_tie, resolve, lambda: jnp.full(vec, IDX_ALL, I32))
    return tau, tau_i


DSA_TS = 256


def _dsa_prompt_kernel(k_sel, idx_bits, qi_ref, wit_ref, qa_ref, ki_ref, ka_ref, vat_ref,
                       oa_ref, key_ref, rel0_ref, rel1_ref, s0_ref, s1_ref, p0_ref, p1_ref):
    i = pl.program_id(1)
    ts = DSA_TS
    n_tiles = (i * QBLK + QBLK + ts - 1) // ts
    lane = lax.broadcasted_iota(I32, (QBLK, LANES), 1)
    left = lane < HEAD_DIM
    krow = lax.broadcasted_iota(I32, (ts, QBLK), 0)
    qpos = i * QBLK + lax.broadcasted_iota(I32, (ts, QBLK), 1)

    zero = jnp.zeros((QBLK, LANES), BF16)
    qi_parts = []
    for p in range(Q_I_W // LANES):
        chunk = qi_ref[:, p * LANES:(p + 1) * LANES]
        qi_parts.append(jnp.where(left, chunk, zero))
        qi_parts.append(jnp.where(left, zero, chunk))
    qi_all = jnp.concatenate(qi_parts, axis=0)
    wt = wit_ref[0]

    last = n_tiles - 1
    n_pairs = (n_tiles + 1) // 2

    def tile_off(j):
        return pl.multiple_of(jnp.minimum(j, last) * ts, ts)

    def rel_tile(j):
        return _dot_nt(ki_ref[pl.ds(tile_off(j), ts), :], qi_all)

    def score_step(j, cur_ref, nxt_ref):
        rel = cur_ref[...]
        nxt_ref[...] = rel_tile(j + 1)
        sc = jnp.zeros((ts, QBLK), F32)
        for h in range(N_IDX_HEADS):
            sc = sc + wt[h:h + 1, :] * jnp.maximum(rel[:, h * QBLK:(h + 1) * QBLK], 0.0)
        sc = jnp.where(j * ts + krow <= qpos, sc, -jnp.inf)
        key_ref[pl.ds(pl.multiple_of(j * ts, ts), ts), :] = sc

    def score_pair(jj, _):
        score_step(2 * jj, rel0_ref, rel1_ref)
        score_step(2 * jj + 1, rel1_ref, rel0_ref)
        return 0

    rel0_ref[...] = rel_tile(0)
    lax.fori_loop(0, n_pairs, score_pair, 0)
    tau, tau_i = _topk_threshold(key_ref, n_pairs, 2 * ts, 0, QBLK, k_sel, idx_bits)

    q4 = []
    for c in range(N_KV_A):
        parts = []
        for g in range(GROUP_A):
            chunk = qa_ref[:, g * LANES:(g + 1) * LANES]
            parts.append(jnp.where(left, chunk, zero) if c == 0 else jnp.where(left, zero, chunk))
        q4.append(jnp.concatenate(parts, axis=0))

    cols4 = GROUP_A * QBLK
    kb_per_tile = ts // KBLK

    def s_tile(j):
        k = ka_ref[pl.ds(tile_off(j), ts), :]
        return [_dot_nt(k, q4[c]) for c in range(N_KV_A)]

    ones_rows = jnp.ones((SUM_ROWS, ts), BF16)

    def values(j):
        jb = jnp.clip(j, 0, last) * kb_per_tile
        v = jnp.concatenate([vat_ref[0, jb + q] for q in range(kb_per_tile)], axis=1)
        return jnp.concatenate([v, ones_rows], axis=0)

    def pv(vals, p_ref, c):
        return _dot(vals, p_ref[c])

    def attn_step(j, carry, s_cur, s_nxt, p_cur, p_prev):
        kt = key_ref[pl.ds(tile_off(j), ts), :]
        kpos = j * ts + krow
        sel = ((kt > tau) | ((kt == tau) & (kpos <= tau_i))) & (kpos <= qpos)
        bias = jnp.where(sel, 0.0, -jnp.inf)
        bias4 = jnp.concatenate([bias] * GROUP_A, axis=1)
        nxt = s_tile(j + 1)
        vals = values(j - 1)
        out = []
        for c in range(N_KV_A):
            m, acc = carry[c]
            s = s_cur[c] + bias4
            s_nxt[c] = nxt[c]
            m_new = jnp.maximum(m, jnp.max(_fold_rows(s, jnp.maximum), axis=0, keepdims=True))
            m_safe = jnp.where(m_new == -jnp.inf, 0.0, m_new)
            acc = (acc + pv(vals, p_prev, c)) * jnp.exp2(m - m_safe)
            p_cur[c] = jnp.exp2(s - m_safe).astype(BF16)
            out.append((m_new, acc))
        return tuple(out)

    def attn_pair(jj, carry):
        carry = attn_step(2 * jj, carry, s0_ref, s1_ref, p0_ref, p1_ref)
        return attn_step(2 * jj + 1, carry, s1_ref, s0_ref, p1_ref, p0_ref)

    first = s_tile(0)
    for c in range(N_KV_A):
        s0_ref[c] = first[c]
    p1_ref[...] = jnp.zeros_like(p1_ref)
    init = tuple((jnp.full((1, cols4), -jnp.inf, F32), jnp.zeros((LANES + SUM_ROWS, cols4), F32))
                 for _ in range(N_KV_A))
    res = lax.fori_loop(0, n_pairs, attn_pair, init)
    vals = values(2 * n_pairs - 1)
    o = []
    for c in range(N_KV_A):
        acc = res[c][1] + pv(vals, p1_ref, c)
        o.append(acc[0:LANES] / acc[LANES:LANES + 1])
    top = lax.broadcasted_iota(I32, (LANES, QBLK), 0) < HEAD_DIM
    for g in range(GROUP_A):
        cs = slice(g * QBLK, (g + 1) * QBLK)
        oa_ref[:, g * LANES:(g + 1) * LANES] = jnp.where(top, o[0][:, cs], o[1][:, cs]).T


def _dsa_prompt(qi, wit, qa, kib, kab, vat, batch, seq):
    nq = seq // QBLK
    assert seq % (2 * DSA_TS) == 0, "key tiles are visited in pairs"
    k_sel = max(1, min(TOPK_MAX, seq // 4))
    idx_bits = max(1, int(seq - 1).bit_length())
    qspec = lambda n: pl.BlockSpec((QBLK, n), lambda b, i: (b * nq + i, 0))
    kspec = pl.BlockSpec((seq, LANES), lambda b, i: (b, 0))
    return pl.pallas_call(
        functools.partial(_dsa_prompt_kernel, k_sel, idx_bits),
        grid=(batch, nq),
        in_specs=[qspec(Q_I_W), pl.BlockSpec((1, N_IDX_HEADS, QBLK), lambda b, i: (b, 0, i)), qspec(Q_A_W),
                  kspec, kspec,
                  pl.BlockSpec((1, seq // KBLK, N_KV_A * HEAD_DIM, KBLK), lambda b, i: (b, 0, 0, 0))],
        out_specs=qspec(Q_A_W),
        out_shape=jax.ShapeDtypeStruct((batch * seq, Q_A_W), F32),
        scratch_shapes=[pltpu.VMEM((seq, QBLK), F32)]
                       + [pltpu.VMEM((DSA_TS, N_IDX_HEADS * QBLK), F32)] * 2
                       + [pltpu.VMEM((N_KV_A, DSA_TS, GROUP_A * QBLK), F32)] * 2
                       + [pltpu.VMEM((N_KV_A, DSA_TS, GROUP_A * QBLK), BF16)] * 2,
        compiler_params=_cparams("parallel", "arbitrary"),
        name="dsa_prompt",
    )(qi, wit, qa, kib, kab, vat)


def _later_matrix(n, key_axis):
    r = lax.broadcasted_iota(I32, (n, n), 0)
    c = lax.broadcasted_iota(I32, (n, n), 1)
    return jnp.where(r > c if key_axis == 1 else c > r, 1.0, 0.0).astype(BF16)


def _sb_tile(z, strict, later_m, carry, key_axis):
    soft = jnp.log(1.0 + jnp.exp(-jnp.abs(z)))
    log_beta = jnp.minimum(z, 0.0) - soft
    log_rest = log_beta - z
    if strict is not None:
        log_rest = jnp.where(strict, log_rest, 0.0)
    hi = log_rest.astype(BF16)
    lo = (log_rest - hi.astype(F32)).astype(BF16)
    if key_axis == 1:
        later = _dot(hi, later_m) + _dot(lo, later_m)
        first_l, first_r = later[:, 0:1], log_rest[:, 0:1]
    else:
        later = _dot(later_m, hi) + _dot(later_m, lo)
        first_l, first_r = later[0:1, :], log_rest[0:1, :]
    a = jnp.exp(log_beta + later + carry)
    if strict is not None:
        a = jnp.where(strict, a, 0.0)
    return a, carry + first_l + first_r


SB_TS = 128


def _sb_prompt_kernel(qb_ref, kb_ref, vbt_ref, ob_ref, acc_ref, carry_ref):
    i = pl.program_id(1)
    ts = SB_TS
    n_pairs = N_HEADS_B // 2
    pcols = 2 * QBLK
    cols = n_pairs * pcols
    lane = lax.broadcasted_iota(I32, (QBLK, LANES), 1)
    left = lane < HEAD_DIM
    zero = jnp.zeros((QBLK, LANES), BF16)
    later_m = _later_matrix(ts, 0)
    krow = lax.broadcasted_iota(I32, (ts, cols), 0)
    qpos = i * QBLK + (lax.broadcasted_iota(I32, (ts, cols), 1) & (QBLK - 1))
    top = lax.broadcasted_iota(I32, (LANES, QBLK), 0) < HEAD_DIM
    pair = [slice(p * LANES, (p + 1) * LANES) for p in range(n_pairs)]

    q2 = []
    for p in range(n_pairs):
        chunk = qb_ref[:, pair[p]]
        q2.append(jnp.concatenate([jnp.where(left, chunk, zero), jnp.where(left, zero, chunk)], axis=0))
    acc_ref[...] = jnp.zeros_like(acc_ref)
    carry_ref[...] = jnp.zeros_like(carry_ref)

    def cond(state):
        j, go = state
        return jnp.logical_and(j >= 0, go)

    def body(state):
        j, _ = state
        off = pl.multiple_of(j * ts, ts)
        z = jnp.concatenate([_dot_nt(kb_ref[pl.ds(off, ts), pair[p]], q2[p]) for p in range(n_pairs)],
                            axis=1)
        strict = (j * ts + krow) < qpos
        a, new_carry = _sb_tile(z, strict, later_m, carry_ref[0:1, :], 0)
        ab = a.astype(BF16)
        for p in range(n_pairs):
            acc_ref[p] += _dot(vbt_ref[0, j, pair[p], :], ab[:, p * pcols:(p + 1) * pcols])
        carry_ref[...] = jnp.broadcast_to(new_carry, carry_ref.shape)
        return j - 1, jnp.max(new_carry) >= SB_STOP

    lax.while_loop(cond, body, (i, jnp.bool_(True)))
    for p in range(n_pairs):
        acc = acc_ref[p]
        ob_ref[:, pair[p]] = jnp.where(top, acc[:, 0:QBLK], acc[:, QBLK:pcols]).T


def _sb_prompt(qb, kbb, vbt, batch, seq):
    nq = seq // QBLK
    qspec = pl.BlockSpec((QBLK, Q_B_W), lambda b, i: (b * nq + i, 0))
    return pl.pallas_call(
        _sb_prompt_kernel,
        grid=(batch, nq),
        in_specs=[qspec, pl.BlockSpec((seq, Q_B_W), lambda b, i: (b, 0)),
                  pl.BlockSpec((1, seq // KBLK, Q_B_W, KBLK), lambda b, i: (b, 0, 0, 0))],
        out_specs=qspec,
        out_shape=jax.ShapeDtypeStruct((batch * seq, Q_B_W), F32),
        scratch_shapes=[pltpu.VMEM((N_HEADS_B // 2, LANES, 2 * QBLK), F32),
                        pltpu.VMEM((SUBLANES, N_HEADS_B * QBLK), F32)],
        compiler_params=_cparams("parallel", "arbitrary"),
        name="sb_prompt",
    )(qb, kbb, vbt)


TSLOTS = 8
SROWS = TSLOTS * N_HEADS_A


def _largest_divisor(n, cap):
    return max(x for x in range(1, cap + 1) if n % x == 0)


def _page_copies(pool_hbm, pt_ref, seq, buf, slot, sem, n_pages, start):
    def body(p, _):
        cp = pltpu.make_async_copy(pool_hbm.at[pt_ref[seq, p]], buf.at[slot, p], sem.at[slot])
        if start:
            cp.start()
        else:
            cp.wait()
        return 0

    lax.fori_loop(0, n_pages, body, 0)


def _page_ring(pool_hbm, pt_ref, buf, sem, n_pages):
    b = pl.program_id(0)
    slot = b & 1

    @pl.when(b == 0)
    def _():
        _page_copies(pool_hbm, pt_ref, 0, buf, 0, sem, n_pages, True)

    @pl.when(b + 1 < pl.num_programs(0))
    def _():
        _page_copies(pool_hbm, pt_ref, b + 1, buf, 1 - slot, sem, n_pages, True)

    _page_copies(pool_hbm, pt_ref, b, buf, slot, sem, n_pages, False)
    return slot


def _dsa_sample_scores_kernel(n_pages, page, ppt, pt_ref, qi_ref, w_ref, kin_ref, kidx_hbm,
                              key_ref, buf, sem):
    slot = _page_ring(kidx_hbm, pt_ref, buf, sem, n_pages)
    ts = ppt * page
    qi = qi_ref[0]
    w = w_ref[0]

    def scores(kt):
        wrel = w * jnp.maximum(_dot(qi, kt), 0.0)
        return jnp.sum(wrel.reshape(TSLOTS, N_IDX_HEADS, kt.shape[1]), axis=1)

    for t in range(n_pages // ppt):
        kt = jnp.concatenate([buf[slot, t * ppt + q] for q in range(ppt)], axis=1).astype(BF16)
        key_ref[0, :, t * ts:(t + 1) * ts] = scores(kt)
    past = n_pages * page
    qslot = lax.broadcasted_iota(I32, (TSLOTS, page), 0)
    u = lax.broadcasted_iota(I32, (TSLOTS, page), 1)
    key_ref[0, :, past:past + page] = jnp.where(u <= qslot, scores(kin_ref[0]), -jnp.inf)


def _dsa_sample_select_kernel(k_sel, idx_bits, n_tiles, ts, key_ref, tau_ref, taui_ref):
    tau, tau_i = _topk_threshold(key_ref, n_tiles, ts, 1, key_ref.shape[0], k_sel, idx_bits)
    tau_ref[...] = jnp.broadcast_to(tau, tau_ref.shape)
    taui_ref[...] = jnp.broadcast_to(tau_i, taui_ref.shape)


def _dsa_sample_attend_kernel(n_pages, page, ppt, pt_ref, qa_ref, key_ref, tau_ref, taui_ref, kvn_ref, kva_hbm,
                              o_ref, buf, sem):
    slot = _page_ring(kva_hbm, pt_ref, buf, sem, n_pages)
    ts = ppt * page
    past = n_pages * page
    kfeat = N_KV_A * HEAD_DIM
    qa = qa_ref[0]
    tau = tau_ref[:, 0:1]
    tau_i = taui_ref[:, 0:1]

    def step(carry, kt, kpos0, k, v):
        m, l, acc = carry
        n = kt.shape[1]
        kpos = kpos0 + lax.broadcasted_iota(I32, (TSLOTS, n), 1)
        qpos = past + lax.broadcasted_iota(I32, (TSLOTS, n), 0)
        sel = ((kt > tau) | ((kt == tau) & (kpos <= tau_i))) & (kpos <= qpos)
        bias = jnp.where(sel, 0.0, -jnp.inf)
        bias_rows = jnp.concatenate(
            [jnp.broadcast_to(bias[t:t + 1, :], (N_HEADS_A, n)) for t in range(TSLOTS)], axis=0)
        s = _dot(qa, k) + bias_rows
        m_new = jnp.maximum(m, jnp.max(s, axis=1, keepdims=True))
        m_safe = jnp.where(m_new == -jnp.inf, 0.0, m_new)
        pr = jnp.exp2(s - m_safe)
        alpha = jnp.exp2(m - m_safe)
        l = alpha * l + jnp.sum(pr, axis=1, keepdims=True)
        acc = alpha * acc + _dot_nt(pr.astype(BF16), v)
        return m_new, l, acc

    def tile(t, carry):
        off = pl.multiple_of(t * ts, ts)
        k = jnp.concatenate([buf[slot, t * ppt + q, 0:kfeat, :] for q in range(ppt)], axis=1)
        v = jnp.concatenate([buf[slot, t * ppt + q, kfeat:2 * kfeat, :] for q in range(ppt)], axis=1)
        return step(carry, key_ref[0, :, pl.ds(off, ts)], t * ts, k.astype(BF16), v.astype(BF16))

    init = (jnp.full((SROWS, 1), -jnp.inf, F32), jnp.zeros((SROWS, 1), F32), jnp.zeros((SROWS, LANES), F32))
    carry = lax.fori_loop(0, n_pages // ppt, tile, init)
    _, l, acc = step(carry, key_ref[0, :, past:past + page], past,
                     kvn_ref[0, 0:kfeat, :], kvn_ref[0, kfeat:2 * kfeat, :])
    o_ref[0] = acc / jnp.where(l > 0.0, l, 1.0)


def _dsa_sample(qi_st, w_st, qa_st, kidx_pool, kva_pool, kin_t, kvn_t, page_table, dec_seq):
    n_seq, n_pages = page_table.shape
    page = kidx_pool.shape[2]
    nk = (n_pages + 1) * page
    ppt = _largest_divisor(n_pages, 8)
    sel_ts = _largest_divisor(n_pages + 1, 8) * page
    k_sel = max(1, min(TOPK_MAX, (n_pages * page + dec_seq) // 4))
    idx_bits = max(1, int(nk - 1).bit_length())
    seq_spec = lambda shp: pl.BlockSpec((1,) + shp, lambda b, pt: (b, 0, 0))
    hbm_spec = pl.BlockSpec(memory_space=pl.ANY)
    ring = lambda feat: [pltpu.VMEM((2, n_pages, feat, page), F32), pltpu.SemaphoreType.DMA((2,))]

    keys = pl.pallas_call(
        functools.partial(_dsa_sample_scores_kernel, n_pages, page, ppt),
        grid_spec=pltpu.PrefetchScalarGridSpec(
            num_scalar_prefetch=1, grid=(n_seq,),
            in_specs=[seq_spec((SROWS, IDX_DIM)), seq_spec((SROWS, 1)), seq_spec((IDX_DIM, page)), hbm_spec],
            out_specs=seq_spec((TSLOTS, nk)),
            scratch_shapes=ring(IDX_DIM)),
        out_shape=jax.ShapeDtypeStruct((n_seq, TSLOTS, nk), F32),
        compiler_params=_cparams("arbitrary"),
        name="dsa_sample_scores",
    )(page_table, qi_st, w_st, kin_t, kidx_pool)

    rows = n_seq * TSLOTS
    tau, tau_i = pl.pallas_call(
        functools.partial(_dsa_sample_select_kernel, k_sel, idx_bits, nk // sel_ts, sel_ts),
        grid=(1,),
        in_specs=[_const_spec((rows, nk))],
        out_specs=[pl.BlockSpec((rows, LANES), lambda i: (0, 0))] * 2,
        out_shape=[jax.ShapeDtypeStruct((rows, LANES), F32), jax.ShapeDtypeStruct((rows, LANES), I32)],
        compiler_params=_cparams("arbitrary"),
        name="dsa_sample_select",
    )(keys.reshape(rows, nk))

    row_spec = pl.BlockSpec((TSLOTS, LANES), lambda b, pt: (b, 0))
    return pl.pallas_call(
        functools.partial(_dsa_sample_attend_kernel, n_pages, page, ppt),
        grid_spec=pltpu.PrefetchScalarGridSpec(
            num_scalar_prefetch=1, grid=(n_seq,),
            in_specs=[seq_spec((SROWS, LANES)), seq_spec((TSLOTS, nk)), row_spec, row_spec,
                      seq_spec((KV_A_W, page)), hbm_spec],
            out_specs=seq_spec((SROWS, LANES)),
            scratch_shapes=ring(KV_A_W)),
        out_shape=jax.ShapeDtypeStruct((n_seq, SROWS, LANES), F32),
        compiler_params=_cparams("arbitrary"),
        name="dsa_sample_attend",
    )(page_table, qa_st, keys, tau, tau_i, kvn_t, kva_pool)


def _sb_sample_kernel(n_pages, page, pt_ref, q_ref, kvn_ref, kvb_hbm, o_ref, buf, sem, acc_ref, carry_ref):
    b = pl.program_id(0)
    q = q_ref[0]
    later_m = _later_matrix(page, 1)
    last = n_pages - 1

    def page_copy(pg):
        slot = (last - pg) & 1
        return pltpu.make_async_copy(kvb_hbm.at[pt_ref[b, pg]], buf.at[slot], sem.at[slot])

    def tile(kt, vt, strict):
        a, new_carry = _sb_tile(_dot(q, kt), strict, later_m, carry_ref[:, 0:1], 1)
        acc_ref[...] += _dot_nt(a.astype(BF16), vt)
        carry_ref[...] = jnp.broadcast_to(new_carry, carry_ref.shape)
        return jnp.max(new_carry) >= SB_STOP

    page_copy(last).start()
    acc_ref[...] = jnp.zeros_like(acc_ref)
    carry_ref[...] = jnp.zeros_like(carry_ref)
    qslot = lax.broadcasted_iota(I32, (SROWS, page), 0) // N_HEADS_B
    u = lax.broadcasted_iota(I32, (SROWS, page), 1)
    go0 = tile(kvn_ref[0, 0:Q_B_W, :], kvn_ref[0, Q_B_W:2 * Q_B_W, :], u < qslot)

    def cond(state):
        pg, go = state
        return jnp.logical_and(pg >= 0, go)

    def body(state):
        pg, _ = state
        page_copy(pg).wait()

        @pl.when(pg > 0)
        def _():
            page_copy(pg - 1).start()

        slot = (last - pg) & 1
        go = tile(buf[slot, 0:Q_B_W, :].astype(BF16), buf[slot, Q_B_W:2 * Q_B_W, :].astype(BF16), None)
        return pg - 1, go

    pg_end, _ = lax.while_loop(cond, body, (jnp.int32(last), go0))

    @pl.when(pg_end >= 0)
    def _():
        page_copy(pg_end).wait()

    r = lax.broadcasted_iota(I32, (SROWS, Q_B_W), 0)
    c = lax.broadcasted_iota(I32, (SROWS, Q_B_W), 1)
    own = (r % N_HEADS_B) == (c // HEAD_DIM)
    diag = jnp.where(own, acc_ref[...], 0.0)
    o_ref[0] = jnp.sum(diag.reshape(TSLOTS, N_HEADS_B, Q_B_W), axis=1)


def _sb_sample(qb_bd, kvb_pool, kvn_t, page_table):
    n_seq, n_pages = page_table.shape
    page = kvb_pool.shape[2]
    seq_spec = lambda shp: pl.BlockSpec((1,) + shp, lambda b, pt: (b, 0, 0))
    return pl.pallas_call(
        functools.partial(_sb_sample_kernel, n_pages, page),
        grid_spec=pltpu.PrefetchScalarGridSpec(
            num_scalar_prefetch=1, grid=(n_seq,),
            in_specs=[seq_spec((SROWS, Q_B_W)), seq_spec((KV_B_W, page)), pl.BlockSpec(memory_space=pl.ANY)],
            out_specs=seq_spec((TSLOTS, Q_B_W)),
            scratch_shapes=[pltpu.VMEM((2, KV_B_W, page), F32), pltpu.SemaphoreType.DMA((2,)),
                            pltpu.VMEM((SROWS, Q_B_W), F32), pltpu.VMEM((SROWS, LANES), F32)]),
        out_shape=jax.ShapeDtypeStruct((n_seq, TSLOTS, Q_B_W), F32),
        compiler_params=_cparams("arbitrary"),
        name="sb_sample",
    )(page_table, qb_bd, kvn_t, kvb_pool)


def _out_kernel(oa_ref, ob_ref, gate_ref, x_ref, wa_ref, wb_ref, wo_ref, nf_ref,
                wg_ref, wu_ref, wd_ref, nfin_ref, y_ref):
    ba = _dot(oa_ref[...].astype(BF16), wa_ref[...])
    bb = _dot(ob_ref[...].astype(BF16), wb_ref[...])
    merged = (gate_ref[:, 0:D_MODEL].astype(F32) * ba
              + gate_ref[:, D_MODEL:2 * D_MODEL].astype(F32) * bb)
    h1 = x_ref[...] + _dot(merged.astype(BF16), wo_ref[...])
    ms = jnp.mean(h1 * h1, axis=-1, keepdims=True)
    hn = ((h1 * lax.rsqrt(ms + RMS_EPS)) * nf_ref[...]).astype(BF16)
    acc = jnp.zeros_like(h1)
    for c in range(D_FF // FFN_CHUNK):
        sl = slice(c * FFN_CHUNK, (c + 1) * FFN_CHUNK)
        gg = _dot(hn, wg_ref[:, sl])
        uu = _dot(hn, wu_ref[:, sl])
        act = (gg * (1.0 / (1.0 + jnp.exp(-gg)))) * uu
        acc = acc + _dot(act.astype(BF16), wd_ref[sl, :])
    h2 = h1 + acc
    ms2 = jnp.mean(h2 * h2, axis=-1, keepdims=True)
    y_ref[...] = (h2 * lax.rsqrt(ms2 + RMS_EPS)) * nfin_ref[...]


def _out(oa, ob, gates, x2d, wts, tm):
    rows = x2d.shape[0]
    row_spec = lambda n: pl.BlockSpec((tm, n), lambda i: (i, 0))
    return pl.pallas_call(
        _out_kernel,
        grid=(rows // tm,),
        in_specs=[row_spec(Q_A_W), row_spec(Q_B_W), row_spec(2 * D_MODEL), row_spec(D_MODEL)]
                 + [_const_spec(w.shape) for w in wts],
        out_specs=row_spec(D_MODEL),
        out_shape=jax.ShapeDtypeStruct((rows, D_MODEL), F32),
        compiler_params=_cparams("parallel"),
        name="out",
    )(oa, ob, gates, x2d, *wts)


def _rope_tables(pos):
    inv_freq = ROPE_THETA ** (-jnp.arange(HALF, dtype=F32) / HALF)
    ang = pos.astype(F32)[:, None] * inv_freq[None, :]
    cos, sin = jnp.cos(ang), jnp.sin(ang)
    cs = jnp.concatenate([cos, cos, cos, cos], axis=1)
    sn = jnp.concatenate([-sin, sin, -sin, sin], axis=1)
    return cs, sn, cos.T, sin.T


def _pack_w_in(w):
    d = w.shape[0]
    o = np.cumsum([0, Q_A_W, KV_A_W, Q_I_W, IDX_DIM, N_IDX_HEADS, Q_B_W, KV_B_W, 2 * D_MODEL])
    kw = N_KV_A * HEAD_DIM
    qa = w[:, o[0]:o[1]].reshape(d, N_HEADS_A, HEAD_DIM)[:, np.array(QA_HEAD_ORDER), :].reshape(d, Q_A_W)
    ka, va = w[:, o[1]:o[1] + kw], w[:, o[1] + kw:o[2]]
    qi, ki, wi, qb = w[:, o[2]:o[3]], w[:, o[3]:o[4]], w[:, o[4]:o[5]], w[:, o[5]:o[6]]
    kb, vb = w[:, o[6]:o[6] + Q_B_W], w[:, o[6] + Q_B_W:o[7]]
    g = w[:, o[7]:o[8]]
    w_tok = jnp.concatenate([qa, qi, ka, ki, ki, qb, kb, g], axis=1).astype(BF16)
    wi_pad = jnp.pad(wi, ((0, 0), (0, HEAD_DIM - N_IDX_HEADS)))
    w_feat = jnp.concatenate([ka, va, ki, wi_pad, kb, vb], axis=1).T.astype(BF16)
    return w_tok, w_feat


def kernel(x_prompt, x_sample, cache_kv_a, cache_k_idx, cache_kv_b, page_table, w_in, w_br_a, w_br_b,
           w_o, norm_attn, norm_ffn, w_ffn_gate, w_ffn_up, w_ffn_down, norm_final):
    depth = w_in.shape[0]
    assert depth == 1, "single-layer step"
    batch, seq, d = x_prompt.shape
    assert d == D_MODEL and seq % ROW_TILE == 0
    layer = 0

    w_tok, w_feat = _pack_w_in(w_in[layer])
    wa = w_br_a[layer].reshape(N_HEADS_A, HEAD_DIM, D_MODEL)[np.array(QA_HEAD_ORDER)].reshape(Q_A_W, D_MODEL)
    out_wts = (wa.astype(BF16), w_br_b[layer].astype(BF16), w_o[layer].astype(BF16),
               norm_ffn[layer].reshape(1, D_MODEL),
               w_ffn_gate[layer].astype(BF16), w_ffn_up[layer].astype(BF16), w_ffn_down[layer].astype(BF16),
               norm_final.reshape(1, D_MODEL))
    gain = norm_attn[layer].reshape(1, D_MODEL)

    xp = x_prompt.reshape(batch * seq, D_MODEL)
    (kva_p, kidx_p, kvb_p, qa, qi, wit, qb, gates, kab, kib, kbb, vat, vbt) = _project(
        xp, gain, w_tok, w_feat, *_rope_tables(jnp.arange(seq, dtype=I32)), batch, ROW_TILE)
    oa = _dsa_prompt(qi, wit, qa, kib, kab, vat, batch, seq)
    ob = _sb_prompt(qb, kbb, vbt, batch, seq)
    y_prompt = _out(oa, ob, gates, xp, out_wts, ROW_TILE).reshape(batch, seq, D_MODEL)

    new_kv_a_p = kva_p.reshape(depth, batch, 2, N_KV_A, HEAD_DIM, seq).transpose(0, 1, 5, 2, 3, 4)
    new_k_idx_p = kidx_p.reshape(depth, batch, IDX_DIM, seq).transpose(0, 1, 3, 2)
    new_kv_b_p = kvb_p.reshape(depth, batch, 2, N_HEADS_B, HEAD_DIM, seq).transpose(0, 1, 5, 2, 3, 4)

    n_seq, dec_seq, _ = x_sample.shape
    n_pages = page_table.shape[1]
    n_pool, page = cache_kv_a.shape[1], cache_kv_a.shape[2]
    past = n_pages * page
    rows_s = n_seq * dec_seq
    assert dec_seq <= TSLOTS and rows_s % KBLK == 0 and page == LANES
    xs = x_sample.reshape(rows_s, D_MODEL)
    cs_s, sn_s, csf_s, snf_s = _rope_tables(past + jnp.arange(dec_seq, dtype=I32))
    tabs_s = (jnp.tile(cs_s, (n_seq, 1)), jnp.tile(sn_s, (n_seq, 1)),
              jnp.tile(csf_s, (1, n_seq)), jnp.tile(snf_s, (1, n_seq)))
    (kva_s, kidx_s, kvb_s, qa_s, qi_s, wit_s, qb_s, gates_s, _, _, _, _, _) = _project(
        xs, gain, w_tok, w_feat, *tabs_s, 1, rows_s)

    def slots(a):
        return jnp.pad(a, ((0, 0), (0, TSLOTS - dec_seq)) + ((0, 0),) * (a.ndim - 2))

    def new_tile(a):
        t = a[0].reshape(a.shape[1], n_seq, dec_seq).transpose(1, 0, 2)
        return jnp.pad(t, ((0, 0), (0, 0), (0, page - dec_seq))).astype(BF16)

    qi_st = slots(qi_s.reshape(n_seq, dec_seq, N_IDX_HEADS, IDX_DIM)).reshape(n_seq, SROWS, IDX_DIM)
    w_st = slots(wit_s[0].T.reshape(n_seq, dec_seq, N_IDX_HEADS)).reshape(n_seq, SROWS, 1)
    qa5 = qa_s.reshape(n_seq, dec_seq, GROUP_A, N_KV_A, HEAD_DIM).transpose(0, 1, 3, 2, 4)
    eye2 = jnp.eye(N_KV_A, dtype=qa_s.dtype)
    qa_st = qa5[:, :, :, :, None, :] * eye2[None, None, :, None, :, None]
    qa_st = slots(qa_st.reshape(n_seq, dec_seq, N_HEADS_A, LANES)).reshape(n_seq, SROWS, LANES)
    kidx_pool = cache_k_idx[layer].transpose(0, 2, 1)
    kva_pool = cache_kv_a[layer].transpose(0, 2, 3, 4, 1).reshape(n_pool, KV_A_W, page)
    kvb_pool = cache_kv_b[layer].transpose(0, 2, 3, 4, 1).reshape(n_pool, KV_B_W, page)
    o_st = _dsa_sample(qi_st, w_st, qa_st, kidx_pool, kva_pool, new_tile(kidx_s), new_tile(kva_s),
                       page_table, dec_seq)
    o6 = o_st.reshape(n_seq, TSLOTS, N_KV_A, GROUP_A, N_KV_A, HEAD_DIM)[:, :dec_seq]
    oa_s = jnp.stack([o6[:, :, c, :, c, :] for c in range(N_KV_A)], axis=2).reshape(rows_s, Q_A_W)

    qb4 = qb_s.reshape(n_seq, dec_seq, N_HEADS_B, HEAD_DIM)
    eye8 = jnp.eye(N_HEADS_B, dtype=qb_s.dtype)
    qb_bd = qb4[:, :, :, None, :] * eye8[None, None, :, :, None]
    qb_bd = slots(qb_bd.reshape(n_seq, dec_seq, N_HEADS_B, Q_B_W)).reshape(n_seq, SROWS, Q_B_W)
    ob_st = _sb_sample(qb_bd, kvb_pool, new_tile(kvb_s), page_table)
    ob_s = ob_st[:, :dec_seq].reshape(rows_s, Q_B_W)

    out_wts_s = (w_br_a[layer].astype(BF16),) + out_wts[1:]
    y_sample = _out(oa_s, ob_s, gates_s, xs, out_wts_s, rows_s).reshape(n_seq, dec_seq, D_MODEL)

    return (y_prompt, y_sample, new_kv_a_p, new_k_idx_p, new_kv_b_p,
            kva_s[0].T.reshape(depth, n_seq, dec_seq, 2, N_KV_A, HEAD_DIM),
            kidx_s[0].T.reshape(depth, n_seq, dec_seq, IDX_DIM),
            kvb_s[0].T.reshape(depth, n_seq, dec_seq, 2, N_HEADS_B, HEAD_DIM))
```

```python
import functools

import jax
import jax.numpy as jnp
import numpy as np
from jax import lax
from jax.experimental import pallas as pl
from jax.experimental.pallas import tpu as pltpu

F32 = jnp.float32
BF16 = jnp.bfloat16
I32 = jnp.int32

D_MODEL = 1024
HEAD_DIM = 64
N_HEADS_A = 8
N_KV_A = 2
GROUP_A = N_HEADS_A // N_KV_A
N_IDX_HEADS = 8
IDX_DIM = 64
TOPK_MAX = 256
N_HEADS_B = 8
Q_A_W = N_HEADS_A * HEAD_DIM
KV_A_W = 2 * N_KV_A * HEAD_DIM
Q_I_W = N_IDX_HEADS * IDX_DIM
Q_B_W = N_HEADS_B * HEAD_DIM
KV_B_W = 2 * N_HEADS_B * HEAD_DIM
D_FF = -(-8 * D_MODEL // (3 * 256)) * 256
ROPE_THETA = 10000.0
RMS_EPS = 1e-6
QK_SCALE = HEAD_DIM ** -0.5
IDX_SCALE = IDX_DIM ** -0.5
HALF = HEAD_DIM // 2
LOG2_E = 1.4426950408889634

LANES = 128
SUBLANES = 8
QBLK = 128
KBLK = 128
SUM_ROWS = 16
FFN_CHUNK = 256
ROW_TILE = 512

TK_QA = 0
TK_QI = 512
TK_KA = 1024
TK_KI = 1152
TK_QB = 1280
TK_KB = 1792
TK_G = 2304
TK_W = 4352
FM_KA = 0
FM_VA = 128
FM_KI = 256
FM_WI = 320
FM_KB = 384
FM_VB = 896
FM_W = 1408
QA_HEAD_ORDER = (0, 4, 1, 5, 2, 6, 3, 7)

NEGINF_KEY = -2139095041
POSINF_KEY = 0x7F800000
MIN_NORMAL_KEY = 0x00800000
NEGINF_KEY16 = -32641
POSINF_KEY16 = 0x7F80
MIN_NORMAL_KEY16 = 0x0080
IDX_ALL = 1 << 30
SB_STOP = -105.0

VMEM_LIMIT = 60 * 1024 * 1024

NT_DIMS = (((1,), (1,)), ((), ()))


def _cparams(*sem):
    return pltpu.CompilerParams(dimension_semantics=sem, vmem_limit_bytes=VMEM_LIMIT)


def _const_spec(shape):
    nd = len(shape)
    return pl.BlockSpec(shape, lambda *_: (0,) * nd, pipeline_mode=pl.Buffered(1))


def _dot_nt(a, b):
    return lax.dot_general(a, b, NT_DIMS, preferred_element_type=F32)


def _dot(a, b):
    return jnp.dot(a, b, preferred_element_type=F32)


def _project_kernel(x_ref, g_ref, wt_ref, wf_ref, cs_ref, sn_ref, csf_ref, snf_ref,
                    kva_ref, kidx_ref, kvb_ref,
                    qa_ref, qi_ref, wit_ref, qb_ref, gate_ref,
                    kab_ref, kib_ref, kbb_ref, vat_ref, vbt_ref):
    x = x_ref[...]
    ms = jnp.mean(x * x, axis=-1, keepdims=True)
    xn = (x * lax.rsqrt(ms + RMS_EPS)) * g_ref[...]
    xb = xn.astype(BF16)
    tm = x.shape[0]
    cs = cs_ref[...]
    sn = sn_ref[...]
    lane = lax.broadcasted_iota(I32, (tm, LANES), 1)
    first_half = (lane & HALF) == 0

    def mm(c0, n):
        return _dot(xb, wt_ref[:, c0:c0 + n])

    def rope(y):
        partner = jnp.where(first_half, pltpu.roll(y, LANES - HALF, 1), pltpu.roll(y, HALF, 1))
        return y * cs + partner * sn

    ya = mm(TK_QA, Q_A_W)
    for c in range(Q_A_W // LANES):
        sl = slice(c * LANES, (c + 1) * LANES)
        qa_ref[:, sl] = (rope(ya[:, sl]) * (QK_SCALE * LOG2_E)).astype(BF16)
    yi = mm(TK_QI, Q_I_W)
    for c in range(Q_I_W // LANES):
        sl = slice(c * LANES, (c + 1) * LANES)
        qi_ref[:, sl] = rope(yi[:, sl]).astype(BF16)
    kab_ref[...] = rope(mm(TK_KA, LANES)).astype(BF16)
    kib_ref[...] = rope(mm(TK_KI, LANES)).astype(BF16)
    qb_ref[...] = (mm(TK_QB, Q_B_W) * QK_SCALE).astype(BF16)
    kbb_ref[...] = mm(TK_KB, Q_B_W).astype(BF16)
    for c in range(2 * D_MODEL // 512):
        sl = slice(c * 512, (c + 1) * 512)
        y = mm(TK_G + c * 512, 512)
        gate_ref[:, sl] = (1.0 / (1.0 + jnp.exp(-y))).astype(BF16)

    csf = csf_ref[...]
    snf = snf_ref[...]

    def fm(r0, n):
        return _dot_nt(wf_ref[r0:r0 + n, :], xb)

    def rope_fm(y):
        lo, hi = y[0:HALF], y[HALF:HEAD_DIM]
        return jnp.concatenate([lo * csf - hi * snf, hi * csf + lo * snf], axis=0)

    ka = fm(FM_KA, N_KV_A * HEAD_DIM)
    for c in range(N_KV_A):
        kva_ref[0, c * HEAD_DIM:(c + 1) * HEAD_DIM, :] = rope_fm(ka[c * HEAD_DIM:(c + 1) * HEAD_DIM])
    va = fm(FM_VA, N_KV_A * HEAD_DIM)
    kva_ref[0, N_KV_A * HEAD_DIM:2 * N_KV_A * HEAD_DIM, :] = va
    kidx_ref[0] = rope_fm(fm(FM_KI, IDX_DIM))
    wi = fm(FM_WI, HEAD_DIM)
    wit_ref[0] = (wi[0:N_IDX_HEADS] * (N_IDX_HEADS ** -0.5)) * IDX_SCALE
    kvb_ref[0, 0:Q_B_W, :] = fm(FM_KB, Q_B_W)
    vb = fm(FM_VB, Q_B_W)
    kvb_ref[0, Q_B_W:2 * Q_B_W, :] = vb
    for q in range(tm // KBLK):
        sl = slice(q * KBLK, (q + 1) * KBLK)
        vat_ref[0, q] = va[:, sl].astype(BF16)
        vbt_ref[0, q] = vb[:, sl].astype(BF16)


def _project(x2d, gain, w_tok, w_feat, cs, sn, csf, snf, groups, tm):
    rows = x2d.shape[0]
    t = rows // groups
    nb = t // tm
    nkb = tm // KBLK
    row_spec = lambda n: pl.BlockSpec((tm, n), lambda i: (i, 0))
    tab_spec = pl.BlockSpec((tm, LANES), lambda i: (i % nb, 0))
    tabf_spec = pl.BlockSpec((HALF, tm), lambda i: (0, i % nb))
    fm_spec = lambda n: pl.BlockSpec((1, n, tm), lambda i: (i // nb, 0, i % nb))
    blk_spec = lambda n: pl.BlockSpec((1, nkb, n, KBLK), lambda i: (i // nb, i % nb, 0, 0))
    out_specs = [fm_spec(KV_A_W), fm_spec(IDX_DIM), fm_spec(KV_B_W),
                 row_spec(Q_A_W), row_spec(Q_I_W), fm_spec(N_IDX_HEADS), row_spec(Q_B_W), row_spec(2 * D_MODEL),
                 row_spec(LANES), row_spec(LANES), row_spec(Q_B_W),
                 blk_spec(N_KV_A * HEAD_DIM), blk_spec(Q_B_W)]
    fm_shape = lambda n, dt: jax.ShapeDtypeStruct((groups, n, t), dt)
    tk_shape = lambda n, dt: jax.ShapeDtypeStruct((rows, n), dt)
    blk_shape = lambda n: jax.ShapeDtypeStruct((groups, t // KBLK, n, KBLK), BF16)
    out_shape = [fm_shape(KV_A_W, F32), fm_shape(IDX_DIM, F32), fm_shape(KV_B_W, F32),
                 tk_shape(Q_A_W, BF16), tk_shape(Q_I_W, BF16), fm_shape(N_IDX_HEADS, F32),
                 tk_shape(Q_B_W, BF16), tk_shape(2 * D_MODEL, BF16),
                 tk_shape(LANES, BF16), tk_shape(LANES, BF16), tk_shape(Q_B_W, BF16),
                 blk_shape(N_KV_A * HEAD_DIM), blk_shape(Q_B_W)]
    return pl.pallas_call(
        _project_kernel,
        grid=(rows // tm,),
        in_specs=[row_spec(D_MODEL), _const_spec((1, D_MODEL)), _const_spec((D_MODEL, TK_W)),
                  _const_spec((FM_W, D_MODEL)), tab_spec, tab_spec, tabf_spec, tabf_spec],
        out_specs=out_specs,
        out_shape=out_shape,
        compiler_params=_cparams("parallel"),
        name="project",
    )(x2d, gain, w_tok, w_feat, cs, sn, csf, snf)


def _key_to_float(k):
    return lax.bitcast_convert_type(k ^ ((k >> 31) & jnp.int32(0x7FFFFFFF)), F32)


def _fold_rows(x, op, rows=SUBLANES):
    n = x.shape[0]
    assert n % rows == 0 and (n // rows) & (n // rows - 1) == 0
    while n > rows:
        n //= 2
        x = op(x[:n], x[n:])
    return x


def _count(key_ref, n_tiles, ts, key_axis, width, pred):
    if key_axis == 1:
        def body(j, acc):
            off = pl.multiple_of(j * ts, ts)
            kt = key_ref[:, pl.ds(off, ts)]
            kpos = j * ts + lax.broadcasted_iota(I32, (width, ts), 1)
            m = jnp.where(pred(kt, kpos), 1.0, 0.0)
            s = m[:, 0:LANES]
            for c in range(1, ts // LANES):
                s = s + m[:, c * LANES:(c + 1) * LANES]
            return acc + s

        acc = lax.fori_loop(0, n_tiles, body, jnp.zeros((width, LANES), F32))
        return jnp.sum(acc, axis=1, keepdims=True)

    def body(j, acc):
        off = pl.multiple_of(j * ts, ts)
        kt = key_ref[pl.ds(off, ts), :]
        kpos = j * ts + lax.broadcasted_iota(I32, (ts, width), 0)
        return acc + _fold_rows(jnp.where(pred(kt, kpos), 1.0, 0.0), jnp.add)

    acc = lax.fori_loop(0, n_tiles, body, jnp.zeros((SUBLANES, width), F32))
    return jnp.sum(acc, axis=0, keepdims=True)


def _topk_threshold(key_ref, n_tiles, ts, key_axis, width, k_sel, idx_bits, coarse_ref=None):
    k_f = float(k_sel)
    vec = (width, 1) if key_axis == 1 else (1, width)
    count = functools.partial(_count, key_ref, n_tiles, ts, key_axis, width)

    def as_float(k):
        zero_band = (k >= -MIN_NORMAL_KEY) & (k < MIN_NORMAL_KEY)
        return _key_to_float(jnp.where(zero_band, 0, k))

    def bis(top_bit, b, r):
        cand = r + lax.shift_left(jnp.int32(1), jnp.int32(top_bit) - b)
        valid = (cand >= NEGINF_KEY) & (cand <= POSINF_KEY)
        cand_f = as_float(jnp.where(valid, cand, 0))
        return jnp.where(valid & (count(lambda st, kp: st >= cand_f) >= k_f), cand, r)

    if coarse_ref is None:
        r = lax.fori_loop(0, 32, functools.partial(bis, 31), jnp.full(vec, NEGINF_KEY, I32))
    else:
        assert key_axis == 0

        def count16(cand_b):
            def body(j, acc):
                off = pl.multiple_of(j * ts, ts)
                m = jnp.where(coarse_ref[pl.ds(off, ts), :] >= cand_b,
                              jnp.ones((), BF16), jnp.zeros((), BF16))
                return acc + _fold_rows(m, jnp.add, 2 * SUBLANES)

            acc = lax.fori_loop(0, n_tiles, body, jnp.zeros((2 * SUBLANES, width), BF16))
            return jnp.sum(acc.astype(F32), axis=0, keepdims=True)

        def bits16(k):
            return lax.shift_left(k ^ ((k >> 15) & jnp.int32(0x7FFF)), jnp.int32(16))

        def bis16(b, r):
            cand = r + lax.shift_left(jnp.int32(1), jnp.int32(15) - b)
            valid = (cand >= NEGINF_KEY16) & (cand <= POSINF_KEY16)
            zero_band = (cand >= -MIN_NORMAL_KEY16) & (cand < MIN_NORMAL_KEY16)
            cand_b = lax.bitcast_convert_type(bits16(jnp.where(valid & ~zero_band, cand, 0)), F32).astype(BF16)
            return jnp.where(valid & (count16(cand_b) >= k_f), cand, r)

        r16 = lax.fori_loop(0, 16, bis16, jnp.full(vec, NEGINF_KEY16, I32))
        b32 = bits16(r16)
        k1 = b32 ^ ((b32 >> 31) & jnp.int32(0x7FFFFFFF))
        r = jnp.maximum(k1 - 0x8000, NEGINF_KEY)
        r = lax.fori_loop(0, 17, functools.partial(bis, 16), r)

    tau = as_float(r)
    need = k_f - count(lambda kt, kp: kt > tau)
    tie = (count(lambda kt, kp: kt == tau) > need) & (tau > -jnp.inf)

    def resolve():
        def ib(b, x):
            c = x + lax.shift_left(jnp.int32(1), jnp.int32(idx_bits - 1) - b)
            cnt = count(lambda kt, kp: (kt == tau) & (kp < c))
            return jnp.where(cnt < need, c, x)

        x = lax.fori_loop(0, idx_bits, ib, jnp.zeros(vec, I32))
        return jnp.where(tie, x, IDX_ALL)

    any_tie = jnp.max(jnp.where(tie, 1.0, 0.0)) > 0.5
    tau_i = lax.cond(any_tie, resolve, lambda: jnp.full(vec, IDX_ALL, I32))
    return tau, tau_i


DSA_TS = 256


def _dsa_prompt_kernel(k_sel, idx_bits, qi_ref, wit_ref, qa_ref, ki_ref, ka_ref, vat_ref,
                       oa_ref, key_ref, keyb_ref, rel0_ref, rel1_ref, s0_ref, s1_ref, p0_ref, p1_ref):
    i = pl.program_id(1)
    ts = DSA_TS
    n_tiles = (i * QBLK + QBLK + ts - 1) // ts
    lane = lax.broadcasted_iota(I32, (QBLK, LANES), 1)
    left = lane < HEAD_DIM
    krow = lax.broadcasted_iota(I32, (ts, QBLK), 0)
    qpos = i * QBLK + lax.broadcasted_iota(I32, (ts, QBLK), 1)

    zero = jnp.zeros((QBLK, LANES), BF16)
    qi_parts = []
    for p in range(Q_I_W // LANES):
        chunk = qi_ref[:, p * LANES:(p + 1) * LANES]
        qi_parts.append(jnp.where(left, chunk, zero))
        qi_parts.append(jnp.where(left, zero, chunk))
    qi_all = jnp.concatenate(qi_parts, axis=0)
    wt = wit_ref[0]

    last = n_tiles - 1
    n_pairs = (n_tiles + 1) // 2

    def tile_off(j):
        return pl.multiple_of(jnp.minimum(j, last) * ts, ts)

    def rel_tile(j):
        return _dot_nt(ki_ref[pl.ds(tile_off(j), ts), :], qi_all)

    def score_step(j, cur_ref, nxt_ref):
        rel = cur_ref[...]
        nxt_ref[...] = rel_tile(j + 1)
        sc = jnp.zeros((ts, QBLK), F32)
        for h in range(N_IDX_HEADS):
            sc = sc + wt[h:h + 1, :] * jnp.maximum(rel[:, h * QBLK:(h + 1) * QBLK], 0.0)
        sc = jnp.where(j * ts + krow <= qpos, sc, -jnp.inf)
        off = pl.multiple_of(j * ts, ts)
        key_ref[pl.ds(off, ts), :] = sc
        keyb_ref[pl.ds(off, ts), :] = sc.astype(BF16)

    def score_pair(jj, _):
        score_step(2 * jj, rel0_ref, rel1_ref)
        score_step(2 * jj + 1, rel1_ref, rel0_ref)
        return 0

    rel0_ref[...] = rel_tile(0)
    lax.fori_loop(0, n_pairs, score_pair, 0)
    tau, tau_i = _topk_threshold(key_ref, n_pairs, 2 * ts, 0, QBLK, k_sel, idx_bits, coarse_ref=keyb_ref)

    q4 = []
    for c in range(N_KV_A):
        parts = []
        for g in range(GROUP_A):
            chunk = qa_ref[:, g * LANES:(g + 1) * LANES]
            parts.append(jnp.where(left, chunk, zero) if c == 0 else jnp.where(left, zero, chunk))
        q4.append(jnp.concatenate(parts, axis=0))

    cols4 = GROUP_A * QBLK
    kb_per_tile = ts // KBLK

    def s_tile(j):
        k = ka_ref[pl.ds(tile_off(j), ts), :]
        return [_dot_nt(k, q4[c]) for c in range(N_KV_A)]

    ones_rows = jnp.ones((SUM_ROWS, ts), BF16)

    def values(j):
        jb = jnp.clip(j, 0, last) * kb_per_tile
        v = jnp.concatenate([vat_ref[0, jb + q] for q in range(kb_per_tile)], axis=1)
        return jnp.concatenate([v, ones_rows], axis=0)

    def pv(vals, p_ref, c):
        return _dot(vals, p_ref[c])

    def attn_step(j, carry, s_cur, s_nxt, p_cur, p_prev):
        kt = key_ref[pl.ds(tile_off(j), ts), :]
        kpos = j * ts + krow
        sel = ((kt > tau) | ((kt == tau) & (kpos <= tau_i))) & (kpos <= qpos)
        bias = jnp.where(sel, 0.0, -jnp.inf)
        bias4 = jnp.concatenate([bias] * GROUP_A, axis=1)
        nxt = s_tile(j + 1)
        vals = values(j - 1)
        out = []
        for c in range(N_KV_A):
            m, acc = carry[c]
            s = s_cur[c] + bias4
            s_nxt[c] = nxt[c]
            m_new = jnp.maximum(m, jnp.max(_fold_rows(s, jnp.maximum), axis=0, keepdims=True))
            m_safe = jnp.where(m_new == -jnp.inf, 0.0, m_new)
            acc = (acc + pv(vals, p_prev, c)) * jnp.exp2(m - m_safe)
            p_cur[c] = jnp.exp2(s - m_safe).astype(BF16)
            out.append((m_new, acc))
        return tuple(out)

    def attn_pair(jj, carry):
        carry = attn_step(2 * jj, carry, s0_ref, s1_ref, p0_ref, p1_ref)
        return attn_step(2 * jj + 1, carry, s1_ref, s0_ref, p1_ref, p0_ref)

    first = s_tile(0)
    for c in range(N_KV_A):
        s0_ref[c] = first[c]
    p1_ref[...] = jnp.zeros_like(p1_ref)
    init = tuple((jnp.full((1, cols4), -jnp.inf, F32), jnp.zeros((LANES + SUM_ROWS, cols4), F32))
                 for _ in range(N_KV_A))
    res = lax.fori_loop(0, n_pairs, attn_pair, init)
    vals = values(2 * n_pairs - 1)
    o = []
    for c in range(N_KV_A):
        acc = res[c][1] + pv(vals, p1_ref, c)
        o.append(acc[0:LANES] / acc[LANES:LANES + 1])
    top = lax.broadcasted_iota(I32, (LANES, QBLK), 0) < HEAD_DIM
    for g in range(GROUP_A):
        cs = slice(g * QBLK, (g + 1) * QBLK)
        oa_ref[:, g * LANES:(g + 1) * LANES] = jnp.where(top, o[0][:, cs], o[1][:, cs]).T


def _dsa_prompt(qi, wit, qa, kib, kab, vat, batch, seq):
    nq = seq // QBLK
    assert seq % (2 * DSA_TS) == 0, "key tiles are visited in pairs"
    k_sel = max(1, min(TOPK_MAX, seq // 4))
    idx_bits = max(1, int(seq - 1).bit_length())
    qspec = lambda n: pl.BlockSpec((QBLK, n), lambda b, i: (b * nq + i, 0))
    kspec = pl.BlockSpec((seq, LANES), lambda b, i: (b, 0))
    return pl.pallas_call(
        functools.partial(_dsa_prompt_kernel, k_sel, idx_bits),
        grid=(batch, nq),
        in_specs=[qspec(Q_I_W), pl.BlockSpec((1, N_IDX_HEADS, QBLK), lambda b, i: (b, 0, i)), qspec(Q_A_W),
                  kspec, kspec,
                  pl.BlockSpec((1, seq // KBLK, N_KV_A * HEAD_DIM, KBLK), lambda b, i: (b, 0, 0, 0))],
        out_specs=qspec(Q_A_W),
        out_shape=jax.ShapeDtypeStruct((batch * seq, Q_A_W), F32),
        scratch_shapes=[pltpu.VMEM((seq, QBLK), F32), pltpu.VMEM((seq, QBLK), BF16)]
                       + [pltpu.VMEM((DSA_TS, N_IDX_HEADS * QBLK), F32)] * 2
                       + [pltpu.VMEM((N_KV_A, DSA_TS, GROUP_A * QBLK), F32)] * 2
                       + [pltpu.VMEM((N_KV_A, DSA_TS, GROUP_A * QBLK), BF16)] * 2,
        compiler_params=_cparams("parallel", "arbitrary"),
        name="dsa_prompt",
    )(qi, wit, qa, kib, kab, vat)


def _later_matrix(n, key_axis):
    r = lax.broadcasted_iota(I32, (n, n), 0)
    c = lax.broadcasted_iota(I32, (n, n), 1)
    return jnp.where(r > c if key_axis == 1 else c > r, 1.0, 0.0).astype(BF16)


def _sb_tile(z, strict, later_m, carry, key_axis):
    soft = jnp.log(1.0 + jnp.exp(-jnp.abs(z)))
    log_beta = jnp.minimum(z, 0.0) - soft
    log_rest = log_beta - z
    if strict is not None:
        log_rest = jnp.where(strict, log_rest, 0.0)
    hi = log_rest.astype(BF16)
    lo = (log_rest - hi.astype(F32)).astype(BF16)
    if key_axis == 1:
        later = _dot(hi, later_m) + _dot(lo, later_m)
        first_l, first_r = later[:, 0:1], log_rest[:, 0:1]
    else:
        later = _dot(later_m, hi) + _dot(later_m, lo)
        first_l, first_r = later[0:1, :], log_rest[0:1, :]
    a = jnp.exp(log_beta + later + carry)
    if strict is not None:
        a = jnp.where(strict, a, 0.0)
    return a, carry + first_l + first_r


SB_TS = 128


def _sb_prompt_kernel(qb_ref, kb_ref, vbt_ref, ob_ref, acc_ref, carry_ref, z_ref):
    i = pl.program_id(1)
    ts = SB_TS
    n_pairs = N_HEADS_B // 2
    pcols = 2 * QBLK
    cols = n_pairs * pcols
    lane = lax.broadcasted_iota(I32, (QBLK, LANES), 1)
    left = lane < HEAD_DIM
    zero = jnp.zeros((QBLK, LANES), BF16)
    later_m = _later_matrix(ts, 0)
    krow = lax.broadcasted_iota(I32, (ts, cols), 0)
    qpos = i * QBLK + (lax.broadcasted_iota(I32, (ts, cols), 1) & (QBLK - 1))
    top = lax.broadcasted_iota(I32, (LANES, QBLK), 0) < HEAD_DIM
    pair = [slice(p * LANES, (p + 1) * LANES) for p in range(n_pairs)]

    q2 = []
    for p in range(n_pairs):
        chunk = qb_ref[:, pair[p]]
        q2.append(jnp.concatenate([jnp.where(left, chunk, zero), jnp.where(left, zero, chunk)], axis=0))
    acc_ref[...] = jnp.zeros_like(acc_ref)
    carry_ref[...] = jnp.zeros_like(carry_ref)

    def cond(state):
        j, go = state
        return jnp.logical_and(j >= 0, go)

    def logits(j):
        off = pl.multiple_of(jnp.maximum(j, 0) * ts, ts)
        return jnp.concatenate([_dot_nt(kb_ref[pl.ds(off, ts), pair[p]], q2[p]) for p in range(n_pairs)],
                               axis=1)

    def body(state):
        j, _ = state
        z = z_ref[...]
        z_next = logits(j - 1)
        strict = (j * ts + krow) < qpos
        a, new_carry = _sb_tile(z, strict, later_m, carry_ref[0:1, :], 0)
        ab = a.astype(BF16)
        for p in range(n_pairs):
            acc_ref[p] += _dot(vbt_ref[0, j, pair[p], :], ab[:, p * pcols:(p + 1) * pcols])
        carry_ref[...] = jnp.broadcast_to(new_carry, carry_ref.shape)
        z_ref[...] = z_next
        return j - 1, jnp.max(new_carry) >= SB_STOP

    z_ref[...] = logits(i)
    lax.while_loop(cond, body, (i, jnp.bool_(True)))
    for p in range(n_pairs):
        acc = acc_ref[p]
        ob_ref[:, pair[p]] = jnp.where(top, acc[:, 0:QBLK], acc[:, QBLK:pcols]).T


def _sb_prompt(qb, kbb, vbt, batch, seq):
    nq = seq // QBLK
    qspec = pl.BlockSpec((QBLK, Q_B_W), lambda b, i: (b * nq + i, 0))
    return pl.pallas_call(
        _sb_prompt_kernel,
        grid=(batch, nq),
        in_specs=[qspec, pl.BlockSpec((seq, Q_B_W), lambda b, i: (b, 0)),
                  pl.BlockSpec((1, seq // KBLK, Q_B_W, KBLK), lambda b, i: (b, 0, 0, 0))],
        out_specs=qspec,
        out_shape=jax.ShapeDtypeStruct((batch * seq, Q_B_W), F32),
        scratch_shapes=[pltpu.VMEM((N_HEADS_B // 2, LANES, 2 * QBLK), F32),
                        pltpu.VMEM((SUBLANES, N_HEADS_B * QBLK), F32),
                        pltpu.VMEM((SB_TS, N_HEADS_B * QBLK), F32)],
        compiler_params=_cparams("parallel", "arbitrary"),
        name="sb_prompt",
    )(qb, kbb, vbt)


TSLOTS = 8
SROWS = TSLOTS * N_HEADS_A


def _largest_divisor(n, cap):
    return max(x for x in range(1, cap + 1) if n % x == 0)


def _page_copies(pool_hbm, pt_ref, seq, buf, slot, sem, n_pages, start):
    def body(p, _):
        cp = pltpu.make_async_copy(pool_hbm.at[pt_ref[seq, p]], buf.at[slot, p], sem.at[slot])
        if start:
            cp.start()
        else:
            cp.wait()
        return 0

    lax.fori_loop(0, n_pages, body, 0)


def _page_ring(pool_hbm, pt_ref, buf, sem, n_pages):
    b = pl.program_id(0)
    slot = b & 1

    @pl.when(b == 0)
    def _():
        _page_copies(pool_hbm, pt_ref, 0, buf, 0, sem, n_pages, True)

    @pl.when(b + 1 < pl.num_programs(0))
    def _():
        _page_copies(pool_hbm, pt_ref, b + 1, buf, 1 - slot, sem, n_pages, True)

    _page_copies(pool_hbm, pt_ref, b, buf, slot, sem, n_pages, False)
    return slot


def _dsa_sample_scores_kernel(n_pages, page, ppt, pt_ref, qi_ref, w_ref, kin_ref, kidx_hbm,
                              key_ref, buf, sem):
    slot = _page_ring(kidx_hbm, pt_ref, buf, sem, n_pages)
    ts = ppt * page
    qi = qi_ref[0]
    w = w_ref[0]

    def scores(kt):
        wrel = w * jnp.maximum(_dot(qi, kt), 0.0)
        return jnp.sum(wrel.reshape(TSLOTS, N_IDX_HEADS, kt.shape[1]), axis=1)

    for t in range(n_pages // ppt):
        kt = jnp.concatenate([buf[slot, t * ppt + q] for q in range(ppt)], axis=1).astype(BF16)
        key_ref[0, :, t * ts:(t + 1) * ts] = scores(kt)
    past = n_pages * page
    qslot = lax.broadcasted_iota(I32, (TSLOTS, page), 0)
    u = lax.broadcasted_iota(I32, (TSLOTS, page), 1)
    key_ref[0, :, past:past + page] = jnp.where(u <= qslot, scores(kin_ref[0]), -jnp.inf)


def _dsa_sample_select_kernel(k_sel, idx_bits, n_tiles, ts, key_ref, tau_ref, taui_ref):
    tau, tau_i = _topk_threshold(key_ref, n_tiles, ts, 1, key_ref.shape[0], k_sel, idx_bits)
    tau_ref[...] = jnp.broadcast_to(tau, tau_ref.shape)
    taui_ref[...] = jnp.broadcast_to(tau_i, taui_ref.shape)


def _dsa_sample_attend_kernel(n_pages, page, ppt, pt_ref, qa_ref, key_ref, tau_ref, taui_ref, kvn_ref, kva_hbm,
                              o_ref, buf, sem):
    slot = _page_ring(kva_hbm, pt_ref, buf, sem, n_pages)
    ts = ppt * page
    past = n_pages * page
    kfeat = N_KV_A * HEAD_DIM
    qa = qa_ref[0]
    tau = tau_ref[:, 0:1]
    tau_i = taui_ref[:, 0:1]

    def step(carry, kt, kpos0, k, v):
        m, l, acc = carry
        n = kt.shape[1]
        kpos = kpos0 + lax.broadcasted_iota(I32, (TSLOTS, n), 1)
        qpos = past + lax.broadcasted_iota(I32, (TSLOTS, n), 0)
        sel = ((kt > tau) | ((kt == tau) & (kpos <= tau_i))) & (kpos <= qpos)
        bias = jnp.where(sel, 0.0, -jnp.inf)
        bias_rows = jnp.concatenate(
            [jnp.broadcast_to(bias[t:t + 1, :], (N_HEADS_A, n)) for t in range(TSLOTS)], axis=0)
        s = _dot(qa, k) + bias_rows
        m_new = jnp.maximum(m, jnp.max(s, axis=1, keepdims=True))
        m_safe = jnp.where(m_new == -jnp.inf, 0.0, m_new)
        pr = jnp.exp2(s - m_safe)
        alpha = jnp.exp2(m - m_safe)
        l = alpha * l + jnp.sum(pr, axis=1, keepdims=True)
        acc = alpha * acc + _dot_nt(pr.astype(BF16), v)
        return m_new, l, acc

    def tile(t, carry):
        off = pl.multiple_of(t * ts, ts)
        k = jnp.concatenate([buf[slot, t * ppt + q, 0:kfeat, :] for q in range(ppt)], axis=1)
        v = jnp.concatenate([buf[slot, t * ppt + q, kfeat:2 * kfeat, :] for q in range(ppt)], axis=1)
        return step(carry, key_ref[0, :, pl.ds(off, ts)], t * ts, k.astype(BF16), v.astype(BF16))

    init = (jnp.full((SROWS, 1), -jnp.inf, F32), jnp.zeros((SROWS, 1), F32), jnp.zeros((SROWS, LANES), F32))
    carry = lax.fori_loop(0, n_pages // ppt, tile, init)
    _, l, acc = step(carry, key_ref[0, :, past:past + page], past,
                     kvn_ref[0, 0:kfeat, :], kvn_ref[0, kfeat:2 * kfeat, :])
    o_ref[0] = acc / jnp.where(l > 0.0, l, 1.0)


def _dsa_sample(qi_st, w_st, qa_st, kidx_pool, kva_pool, kin_t, kvn_t, page_table, dec_seq):
    n_seq, n_pages = page_table.shape
    page = kidx_pool.shape[2]
    nk = (n_pages + 1) * page
    ppt = _largest_divisor(n_pages, 8)
    sel_ts = _largest_divisor(n_pages + 1, 8) * page
    k_sel = max(1, min(TOPK_MAX, (n_pages * page + dec_seq) // 4))
    idx_bits = max(1, int(nk - 1).bit_length())
    seq_spec = lambda shp: pl.BlockSpec((1,) + shp, lambda b, pt: (b, 0, 0))
    hbm_spec = pl.BlockSpec(memory_space=pl.ANY)
    ring = lambda feat: [pltpu.VMEM((2, n_pages, feat, page), F32), pltpu.SemaphoreType.DMA((2,))]

    keys = pl.pallas_call(
        functools.partial(_dsa_sample_scores_kernel, n_pages, page, ppt),
        grid_spec=pltpu.PrefetchScalarGridSpec(
            num_scalar_prefetch=1, grid=(n_seq,),
            in_specs=[seq_spec((SROWS, IDX_DIM)), seq_spec((SROWS, 1)), seq_spec((IDX_DIM, page)), hbm_spec],
            out_specs=seq_spec((TSLOTS, nk)),
            scratch_shapes=ring(IDX_DIM)),
        out_shape=jax.ShapeDtypeStruct((n_seq, TSLOTS, nk), F32),
        compiler_params=_cparams("arbitrary"),
        name="dsa_sample_scores",
    )(page_table, qi_st, w_st, kin_t, kidx_pool)

    rows = n_seq * dec_seq
    tau, tau_i = pl.pallas_call(
        functools.partial(_dsa_sample_select_kernel, k_sel, idx_bits, nk // sel_ts, sel_ts),
        grid=(1,),
        in_specs=[_const_spec((rows, nk))],
        out_specs=[pl.BlockSpec((rows, LANES), lambda i: (0, 0))] * 2,
        out_shape=[jax.ShapeDtypeStruct((rows, LANES), F32), jax.ShapeDtypeStruct((rows, LANES), I32)],
        compiler_params=_cparams("arbitrary"),
        name="dsa_sample_select",
    )(keys[:, :dec_seq].reshape(rows, nk))
    pad_slots = lambda a: jnp.pad(a.reshape(n_seq, dec_seq, LANES),
                                  ((0, 0), (0, TSLOTS - dec_seq), (0, 0))).reshape(n_seq * TSLOTS, LANES)
    tau, tau_i = pad_slots(tau), pad_slots(tau_i)

    row_spec = pl.BlockSpec((TSLOTS, LANES), lambda b, pt: (b, 0))
    return pl.pallas_call(
        functools.partial(_dsa_sample_attend_kernel, n_pages, page, ppt),
        grid_spec=pltpu.PrefetchScalarGridSpec(
            num_scalar_prefetch=1, grid=(n_seq,),
            in_specs=[seq_spec((SROWS, LANES)), seq_spec((TSLOTS, nk)), row_spec, row_spec,
                      seq_spec((KV_A_W, page)), hbm_spec],
            out_specs=seq_spec((SROWS, LANES)),
            scratch_shapes=ring(KV_A_W)),
        out_shape=jax.ShapeDtypeStruct((n_seq, SROWS, LANES), F32),
        compiler_params=_cparams("arbitrary"),
        name="dsa_sample_attend",
    )(page_table, qa_st, keys, tau, tau_i, kvn_t, kva_pool)


def _sb_sample_kernel(n_pages, page, pt_ref, q_ref, kvn_ref, kvb_hbm, o_ref, buf, sem, acc_ref, carry_ref):
    b = pl.program_id(0)
    q = q_ref[0]
    later_m = _later_matrix(page, 1)
    last = n_pages - 1

    def page_copy(pg):
        slot = (last - pg) & 1
        return pltpu.make_async_copy(kvb_hbm.at[pt_ref[b, pg]], buf.at[slot], sem.at[slot])

    def tile(kt, vt, strict):
        a, new_carry = _sb_tile(_dot(q, kt), strict, later_m, carry_ref[:, 0:1], 1)
        acc_ref[...] += _dot_nt(a.astype(BF16), vt)
        carry_ref[...] = jnp.broadcast_to(new_carry, carry_ref.shape)
        return jnp.max(new_carry) >= SB_STOP

    page_copy(last).start()
    acc_ref[...] = jnp.zeros_like(acc_ref)
    carry_ref[...] = jnp.zeros_like(carry_ref)
    qslot = lax.broadcasted_iota(I32, (SROWS, page), 0) // N_HEADS_B
    u = lax.broadcasted_iota(I32, (SROWS, page), 1)
    go0 = tile(kvn_ref[0, 0:Q_B_W, :], kvn_ref[0, Q_B_W:2 * Q_B_W, :], u < qslot)

    def cond(state):
        pg, go = state
        return jnp.logical_and(pg >= 0, go)

    def body(state):
        pg, _ = state
        page_copy(pg).wait()

        @pl.when(pg > 0)
        def _():
            page_copy(pg - 1).start()

        slot = (last - pg) & 1
        go = tile(buf[slot, 0:Q_B_W, :].astype(BF16), buf[slot, Q_B_W:2 * Q_B_W, :].astype(BF16), None)
        return pg - 1, go

    pg_end, _ = lax.while_loop(cond, body, (jnp.int32(last), go0))

    @pl.when(pg_end >= 0)
    def _():
        page_copy(pg_end).wait()

    r = lax.broadcasted_iota(I32, (SROWS, Q_B_W), 0)
    c = lax.broadcasted_iota(I32, (SROWS, Q_B_W), 1)
    own = (r % N_HEADS_B) == (c // HEAD_DIM)
    diag = jnp.where(own, acc_ref[...], 0.0)
    o_ref[0] = jnp.sum(diag.reshape(TSLOTS, N_HEADS_B, Q_B_W), axis=1)


def _sb_sample(qb_bd, kvb_pool, kvn_t, page_table):
    n_seq, n_pages = page_table.shape
    page = kvb_pool.shape[2]
    seq_spec = lambda shp: pl.BlockSpec((1,) + shp, lambda b, pt: (b, 0, 0))
    return pl.pallas_call(
        functools.partial(_sb_sample_kernel, n_pages, page),
        grid_spec=pltpu.PrefetchScalarGridSpec(
            num_scalar_prefetch=1, grid=(n_seq,),
            in_specs=[seq_spec((SROWS, Q_B_W)), seq_spec((KV_B_W, page)), pl.BlockSpec(memory_space=pl.ANY)],
            out_specs=seq_spec((TSLOTS, Q_B_W)),
            scratch_shapes=[pltpu.VMEM((2, KV_B_W, page), F32), pltpu.SemaphoreType.DMA((2,)),
                            pltpu.VMEM((SROWS, Q_B_W), F32), pltpu.VMEM((SROWS, LANES), F32)]),
        out_shape=jax.ShapeDtypeStruct((n_seq, TSLOTS, Q_B_W), F32),
        compiler_params=_cparams("arbitrary"),
        name="sb_sample",
    )(page_table, qb_bd, kvn_t, kvb_pool)


def _out_kernel(oa_ref, ob_ref, gate_ref, x_ref, wa_ref, wb_ref, wo_ref, nf_ref,
                wg_ref, wu_ref, wd_ref, nfin_ref, y_ref):
    ba = _dot(oa_ref[...].astype(BF16), wa_ref[...])
    bb = _dot(ob_ref[...].astype(BF16), wb_ref[...])
    merged = (gate_ref[:, 0:D_MODEL].astype(F32) * ba
              + gate_ref[:, D_MODEL:2 * D_MODEL].astype(F32) * bb)
    h1 = x_ref[...] + _dot(merged.astype(BF16), wo_ref[...])
    ms = jnp.mean(h1 * h1, axis=-1, keepdims=True)
    hn = ((h1 * lax.rsqrt(ms + RMS_EPS)) * nf_ref[...]).astype(BF16)
    acc = jnp.zeros_like(h1)
    for c in range(D_FF // FFN_CHUNK):
        sl = slice(c * FFN_CHUNK, (c + 1) * FFN_CHUNK)
        gg = _dot(hn, wg_ref[:, sl])
        uu = _dot(hn, wu_ref[:, sl])
        act = (gg * (1.0 / (1.0 + jnp.exp(-gg)))) * uu
        acc = acc + _dot(act.astype(BF16), wd_ref[sl, :])
    h2 = h1 + acc
    ms2 = jnp.mean(h2 * h2, axis=-1, keepdims=True)
    y_ref[...] = (h2 * lax.rsqrt(ms2 + RMS_EPS)) * nfin_ref[...]


def _out(oa, ob, gates, x2d, wts, tm):
    rows = x2d.shape[0]
    row_spec = lambda n: pl.BlockSpec((tm, n), lambda i: (i, 0))
    return pl.pallas_call(
        _out_kernel,
        grid=(rows // tm,),
        in_specs=[row_spec(Q_A_W), row_spec(Q_B_W), row_spec(2 * D_MODEL), row_spec(D_MODEL)]
                 + [_const_spec(w.shape) for w in wts],
        out_specs=row_spec(D_MODEL),
        out_shape=jax.ShapeDtypeStruct((rows, D_MODEL), F32),
        compiler_params=_cparams("parallel"),
        name="out",
    )(oa, ob, gates, x2d, *wts)


def _rope_tables(pos):
    inv_freq = ROPE_THETA ** (-jnp.arange(HALF, dtype=F32) / HALF)
    ang = pos.astype(F32)[:, None] * inv_freq[None, :]
    cos, sin = jnp.cos(ang), jnp.sin(ang)
    cs = jnp.concatenate([cos, cos, cos, cos], axis=1)
    sn = jnp.concatenate([-sin, sin, -sin, sin], axis=1)
    return cs, sn, cos.T, sin.T


def _pack_w_in(w):
    d = w.shape[0]
    o = np.cumsum([0, Q_A_W, KV_A_W, Q_I_W, IDX_DIM, N_IDX_HEADS, Q_B_W, KV_B_W, 2 * D_MODEL])
    kw = N_KV_A * HEAD_DIM
    qa = w[:, o[0]:o[1]].reshape(d, N_HEADS_A, HEAD_DIM)[:, np.array(QA_HEAD_ORDER), :].reshape(d, Q_A_W)
    ka, va = w[:, o[1]:o[1] + kw], w[:, o[1] + kw:o[2]]
    qi, ki, wi, qb = w[:, o[2]:o[3]], w[:, o[3]:o[4]], w[:, o[4]:o[5]], w[:, o[5]:o[6]]
    kb, vb = w[:, o[6]:o[6] + Q_B_W], w[:, o[6] + Q_B_W:o[7]]
    g = w[:, o[7]:o[8]]
    w_tok = jnp.concatenate([qa, qi, ka, ki, ki, qb, kb, g], axis=1).astype(BF16)
    wi_pad = jnp.pad(wi, ((0, 0), (0, HEAD_DIM - N_IDX_HEADS)))
    w_feat = jnp.concatenate([ka, va, ki, wi_pad, kb, vb], axis=1).T.astype(BF16)
    return w_tok, w_feat


def kernel(x_prompt, x_sample, cache_kv_a, cache_k_idx, cache_kv_b, page_table, w_in, w_br_a, w_br_b,
           w_o, norm_attn, norm_ffn, w_ffn_gate, w_ffn_up, w_ffn_down, norm_final):
    depth = w_in.shape[0]
    assert depth == 1, "single-layer step"
    batch, seq, d = x_prompt.shape
    assert d == D_MODEL and seq % ROW_TILE == 0
    layer = 0

    w_tok, w_feat = _pack_w_in(w_in[layer])
    wa = w_br_a[layer].reshape(N_HEADS_A, HEAD_DIM, D_MODEL)[np.array(QA_HEAD_ORDER)].reshape(Q_A_W, D_MODEL)
    out_wts = (wa.astype(BF16), w_br_b[layer].astype(BF16), w_o[layer].astype(BF16),
               norm_ffn[layer].reshape(1, D_MODEL),
               w_ffn_gate[layer].astype(BF16), w_ffn_up[layer].astype(BF16), w_ffn_down[layer].astype(BF16),
               norm_final.reshape(1, D_MODEL))
    gain = norm_attn[layer].reshape(1, D_MODEL)

    xp = x_prompt.reshape(batch * seq, D_MODEL)
    (kva_p, kidx_p, kvb_p, qa, qi, wit, qb, gates, kab, kib, kbb, vat, vbt) = _project(
        xp, gain, w_tok, w_feat, *_rope_tables(jnp.arange(seq, dtype=I32)), batch, ROW_TILE)
    oa = _dsa_prompt(qi, wit, qa, kib, kab, vat, batch, seq)
    ob = _sb_prompt(qb, kbb, vbt, batch, seq)
    y_prompt = _out(oa, ob, gates, xp, out_wts, ROW_TILE).reshape(batch, seq, D_MODEL)

    new_kv_a_p = kva_p.reshape(depth, batch, 2, N_KV_A, HEAD_DIM, seq).transpose(0, 1, 5, 2, 3, 4)
    new_k_idx_p = kidx_p.reshape(depth, batch, IDX_DIM, seq).transpose(0, 1, 3, 2)
    new_kv_b_p = kvb_p.reshape(depth, batch, 2, N_HEADS_B, HEAD_DIM, seq).transpose(0, 1, 5, 2, 3, 4)

    n_seq, dec_seq, _ = x_sample.shape
    n_pages = page_table.shape[1]
    n_pool, page = cache_kv_a.shape[1], cache_kv_a.shape[2]
    past = n_pages * page
    rows_s = n_seq * dec_seq
    assert dec_seq <= TSLOTS and rows_s % KBLK == 0 and page == LANES
    xs = x_sample.reshape(rows_s, D_MODEL)
    cs_s, sn_s, csf_s, snf_s = _rope_tables(past + jnp.arange(dec_seq, dtype=I32))
    tabs_s = (jnp.tile(cs_s, (n_seq, 1)), jnp.tile(sn_s, (n_seq, 1)),
              jnp.tile(csf_s, (1, n_seq)), jnp.tile(snf_s, (1, n_seq)))
    (kva_s, kidx_s, kvb_s, qa_s, qi_s, wit_s, qb_s, gates_s, _, _, _, _, _) = _project(
        xs, gain, w_tok, w_feat, *tabs_s, 1, rows_s)

    def slots(a):
        return jnp.pad(a, ((0, 0), (0, TSLOTS - dec_seq)) + ((0, 0),) * (a.ndim - 2))

    def new_tile(a):
        t = a[0].reshape(a.shape[1], n_seq, dec_seq).transpose(1, 0, 2)
        return jnp.pad(t, ((0, 0), (0, 0), (0, page - dec_seq))).astype(BF16)

    qi_st = slots(qi_s.reshape(n_seq, dec_seq, N_IDX_HEADS, IDX_DIM)).reshape(n_seq, SROWS, IDX_DIM)
    w_st = slots(wit_s[0].T.reshape(n_seq, dec_seq, N_IDX_HEADS)).reshape(n_seq, SROWS, 1)
    qa5 = qa_s.reshape(n_seq, dec_seq, GROUP_A, N_KV_A, HEAD_DIM).transpose(0, 1, 3, 2, 4)
    eye2 = jnp.eye(N_KV_A, dtype=qa_s.dtype)
    qa_st = qa5[:, :, :, :, None, :] * eye2[None, None, :, None, :, None]
    qa_st = slots(qa_st.reshape(n_seq, dec_seq, N_HEADS_A, LANES)).reshape(n_seq, SROWS, LANES)
    kidx_pool = cache_k_idx[layer].transpose(0, 2, 1)
    kva_pool = cache_kv_a[layer].transpose(0, 2, 3, 4, 1).reshape(n_pool, KV_A_W, page)
    kvb_pool = cache_kv_b[layer].transpose(0, 2, 3, 4, 1).reshape(n_pool, KV_B_W, page)
    o_st = _dsa_sample(qi_st, w_st, qa_st, kidx_pool, kva_pool, new_tile(kidx_s), new_tile(kva_s),
                       page_table, dec_seq)
    o6 = o_st.reshape(n_seq, TSLOTS, N_KV_A, GROUP_A, N_KV_A, HEAD_DIM)[:, :dec_seq]
    oa_s = jnp.stack([o6[:, :, c, :, c, :] for c in range(N_KV_A)], axis=2).reshape(rows_s, Q_A_W)

    qb4 = qb_s.reshape(n_seq, dec_seq, N_HEADS_B, HEAD_DIM)
    eye8 = jnp.eye(N_HEADS_B, dtype=qb_s.dtype)
    qb_bd = qb4[:, :, :, None, :] * eye8[None, None, :, :, None]
    qb_bd = slots(qb_bd.reshape(n_seq, dec_seq, N_HEADS_B, Q_B_W)).reshape(n_seq, SROWS, Q_B_W)
    ob_st = _sb_sample(qb_bd, kvb_pool, new_tile(kvb_s), page_table)
    ob_s = ob_st[:, :dec_seq].reshape(rows_s, Q_B_W)

    out_wts_s = (w_br_a[layer].astype(BF16),) + out_wts[1:]
    y_sample = _out(oa_s, ob_s, gates_s, xs, out_wts_s, rows_s).reshape(n_seq, dec_seq, D_MODEL)

    return (y_prompt, y_sample, new_kv_a_p, new_k_idx_p, new_kv_b_p,
            kva_s[0].T.reshape(depth, n_seq, dec_seq, 2, N_KV_A, HEAD_DIM),
            kidx_s[0].T.reshape(depth, n_seq, dec_seq, IDX_DIM),
            kvb_s[0].T.reshape(depth, n_seq, dec_seq, 2, N_HEADS_B, HEAD_DIM))
```

```python
import functools

import jax
import jax.numpy as jnp
import numpy as np
from jax import lax
from jax.experimental import pallas as pl
from jax.experimental.pallas import tpu as pltpu

F32 = jnp.float32
BF16 = jnp.bfloat16
I32 = jnp.int32

D_MODEL = 1024
HEAD_DIM = 64
N_HEADS_A = 8
N_KV_A = 2
GROUP_A = N_HEADS_A // N_KV_A
N_IDX_HEADS = 8
IDX_DIM = 64
TOPK_MAX = 256
N_HEADS_B = 8
Q_A_W = N_HEADS_A * HEAD_DIM
KV_A_W = 2 * N_KV_A * HEAD_DIM
Q_I_W = N_IDX_HEADS * IDX_DIM
Q_B_W = N_HEADS_B * HEAD_DIM
KV_B_W = 2 * N_HEADS_B * HEAD_DIM
D_FF = -(-8 * D_MODEL // (3 * 256)) * 256
ROPE_THETA = 10000.0
RMS_EPS = 1e-6
QK_SCALE = HEAD_DIM ** -0.5
IDX_SCALE = IDX_DIM ** -0.5
HALF = HEAD_DIM // 2
LOG2_E = 1.4426950408889634

LANES = 128
SUBLANES = 8
QBLK = 128
KBLK = 128
SUM_ROWS = 16
FFN_CHUNK = 256
ROW_TILE = 512

TK_QA = 0
TK_QI = 512
TK_QB = 1024
TK_G = 1536
TK_W = 3584
FM_KA = 0
FM_VA = 128
FM_KI = 256
FM_WI = 320
FM_KB = 384
FM_VB = 896
FM_W = 1408
QA_HEAD_ORDER = (0, 4, 1, 5, 2, 6, 3, 7)

NEGINF_KEY = -2139095041
POSINF_KEY = 0x7F800000
MIN_NORMAL_KEY = 0x00800000
IDX_ALL = 1 << 30
SB_STOP = -105.0

VMEM_LIMIT = 60 * 1024 * 1024

NT_DIMS = (((1,), (1,)), ((), ()))


def _cparams(*sem):
    return pltpu.CompilerParams(dimension_semantics=sem, vmem_limit_bytes=VMEM_LIMIT)


def _const_spec(shape):
    nd = len(shape)
    return pl.BlockSpec(shape, lambda *_: (0,) * nd, pipeline_mode=pl.Buffered(1))


def _dot_nt(a, b):
    return lax.dot_general(a, b, NT_DIMS, preferred_element_type=F32)


def _dot(a, b):
    return jnp.dot(a, b, preferred_element_type=F32)


def _project_kernel(x_ref, g_ref, wt_ref, wf_ref, cs_ref, sn_ref, csf_ref, snf_ref,
                    kva_ref, kidx_ref, kvb_ref,
                    qa_ref, qi_ref, wit_ref, qb_ref, gate_ref,
                    kab_ref, kib_ref, kbb_ref, vat_ref, vbt_ref):
    x = x_ref[...]
    ms = jnp.mean(x * x, axis=-1, keepdims=True)
    xn = (x * lax.rsqrt(ms + RMS_EPS)) * g_ref[...]
    xb = xn.astype(BF16)
    tm = x.shape[0]
    cs = cs_ref[...]
    sn = sn_ref[...]
    lane = lax.broadcasted_iota(I32, (tm, LANES), 1)
    first_half = (lane & HALF) == 0

    def mm(c0, n):
        return _dot(xb, wt_ref[:, c0:c0 + n])

    def rope(y):
        partner = jnp.where(first_half, pltpu.roll(y, LANES - HALF, 1), pltpu.roll(y, HALF, 1))
        return y * cs + partner * sn

    ya = mm(TK_QA, Q_A_W)
    for c in range(Q_A_W // LANES):
        sl = slice(c * LANES, (c + 1) * LANES)
        qa_ref[:, sl] = (rope(ya[:, sl]) * (QK_SCALE * LOG2_E)).astype(BF16)
    yi = mm(TK_QI, Q_I_W)
    for c in range(Q_I_W // LANES):
        sl = slice(c * LANES, (c + 1) * LANES)
        qi_ref[:, sl] = rope(yi[:, sl]).astype(BF16)
    qb_ref[...] = (mm(TK_QB, Q_B_W) * QK_SCALE).astype(BF16)
    for c in range(2 * D_MODEL // 512):
        sl = slice(c * 512, (c + 1) * 512)
        y = mm(TK_G + c * 512, 512)
        gate_ref[:, sl] = (1.0 / (1.0 + jnp.exp(-y))).astype(BF16)

    csf = csf_ref[...]
    snf = snf_ref[...]

    def fm(r0, n):
        return _dot_nt(wf_ref[r0:r0 + n, :], xb)

    def rope_fm(y):
        lo, hi = y[0:HALF], y[HALF:HEAD_DIM]
        return jnp.concatenate([lo * csf - hi * snf, hi * csf + lo * snf], axis=0)

    ka = fm(FM_KA, N_KV_A * HEAD_DIM)
    ka = jnp.concatenate([rope_fm(ka[c * HEAD_DIM:(c + 1) * HEAD_DIM]) for c in range(N_KV_A)], axis=0)
    kva_ref[0, 0:N_KV_A * HEAD_DIM, :] = ka
    kab_ref[...] = ka.T.astype(BF16)
    va = fm(FM_VA, N_KV_A * HEAD_DIM)
    kva_ref[0, N_KV_A * HEAD_DIM:2 * N_KV_A * HEAD_DIM, :] = va
    ki = rope_fm(fm(FM_KI, IDX_DIM))
    kidx_ref[0] = ki
    kib_ref[...] = jnp.concatenate([ki, ki], axis=0).T.astype(BF16)
    wi = fm(FM_WI, HEAD_DIM)
    wit_ref[0] = (wi[0:N_IDX_HEADS] * (N_IDX_HEADS ** -0.5)) * IDX_SCALE
    kb = fm(FM_KB, Q_B_W)
    kvb_ref[0, 0:Q_B_W, :] = kb
    kbb_ref[...] = kb.T.astype(BF16)
    vb = fm(FM_VB, Q_B_W)
    kvb_ref[0, Q_B_W:2 * Q_B_W, :] = vb
    for q in range(tm // KBLK):
        sl = slice(q * KBLK, (q + 1) * KBLK)
        vat_ref[0, q] = va[:, sl].astype(BF16)
        vbt_ref[0, q] = vb[:, sl].astype(BF16)


def _project(x2d, gain, w_tok, w_feat, cs, sn, csf, snf, groups, tm):
    rows = x2d.shape[0]
    t = rows // groups
    nb = t // tm
    nkb = tm // KBLK
    row_spec = lambda n: pl.BlockSpec((tm, n), lambda i: (i, 0))
    tab_spec = pl.BlockSpec((tm, LANES), lambda i: (i % nb, 0))
    tabf_spec = pl.BlockSpec((HALF, tm), lambda i: (0, i % nb))
    fm_spec = lambda n: pl.BlockSpec((1, n, tm), lambda i: (i // nb, 0, i % nb))
    blk_spec = lambda n: pl.BlockSpec((1, nkb, n, KBLK), lambda i: (i // nb, i % nb, 0, 0))
    out_specs = [fm_spec(KV_A_W), fm_spec(IDX_DIM), fm_spec(KV_B_W),
                 row_spec(Q_A_W), row_spec(Q_I_W), fm_spec(N_IDX_HEADS), row_spec(Q_B_W), row_spec(2 * D_MODEL),
                 row_spec(LANES), row_spec(LANES), row_spec(Q_B_W),
                 blk_spec(N_KV_A * HEAD_DIM), blk_spec(Q_B_W)]
    fm_shape = lambda n, dt: jax.ShapeDtypeStruct((groups, n, t), dt)
    tk_shape = lambda n, dt: jax.ShapeDtypeStruct((rows, n), dt)
    blk_shape = lambda n: jax.ShapeDtypeStruct((groups, t // KBLK, n, KBLK), BF16)
    out_shape = [fm_shape(KV_A_W, F32), fm_shape(IDX_DIM, F32), fm_shape(KV_B_W, F32),
                 tk_shape(Q_A_W, BF16), tk_shape(Q_I_W, BF16), fm_shape(N_IDX_HEADS, F32),
                 tk_shape(Q_B_W, BF16), tk_shape(2 * D_MODEL, BF16),
                 tk_shape(LANES, BF16), tk_shape(LANES, BF16), tk_shape(Q_B_W, BF16),
                 blk_shape(N_KV_A * HEAD_DIM), blk_shape(Q_B_W)]
    return pl.pallas_call(
        _project_kernel,
        grid=(rows // tm,),
        in_specs=[row_spec(D_MODEL), _const_spec((1, D_MODEL)), _const_spec((D_MODEL, TK_W)),
                  _const_spec((FM_W, D_MODEL)), tab_spec, tab_spec, tabf_spec, tabf_spec],
        out_specs=out_specs,
        out_shape=out_shape,
        compiler_params=_cparams("parallel"),
        name="project",
    )(x2d, gain, w_tok, w_feat, cs, sn, csf, snf)


def _key_to_float(k):
    return lax.bitcast_convert_type(k ^ ((k >> 31) & jnp.int32(0x7FFFFFFF)), F32)


def _fold_rows(x, op, rows=SUBLANES):
    n = x.shape[0]
    assert n % rows == 0 and (n // rows) & (n // rows - 1) == 0
    while n > rows:
        n //= 2
        x = op(x[:n], x[n:])
    return x


def _count(key_ref, n_tiles, ts, key_axis, width, pred):
    if key_axis == 1:
        def body(j, acc):
            off = pl.multiple_of(j * ts, ts)
            kt = key_ref[:, pl.ds(off, ts)]
            kpos = j * ts + lax.broadcasted_iota(I32, (width, ts), 1)
            m = jnp.where(pred(kt, kpos), 1.0, 0.0)
            s = m[:, 0:LANES]
            for c in range(1, ts // LANES):
                s = s + m[:, c * LANES:(c + 1) * LANES]
            return acc + s

        acc = lax.fori_loop(0, n_tiles, body, jnp.zeros((width, LANES), F32))
        return jnp.sum(acc, axis=1, keepdims=True)

    def body(j, acc):
        off = pl.multiple_of(j * ts, ts)
        kt = key_ref[pl.ds(off, ts), :]
        kpos = j * ts + lax.broadcasted_iota(I32, (ts, width), 0)
        return acc + _fold_rows(jnp.where(pred(kt, kpos), 1.0, 0.0), jnp.add)

    acc = lax.fori_loop(0, n_tiles, body, jnp.zeros((SUBLANES, width), F32))
    return jnp.sum(acc, axis=0, keepdims=True)


def _topk_threshold(key_ref, n_tiles, ts, key_axis, width, k_sel, idx_bits):
    k_f = float(k_sel)
    vec = (width, 1) if key_axis == 1 else (1, width)
    count = functools.partial(_count, key_ref, n_tiles, ts, key_axis, width)

    def as_float(k):
        zero_band = (k >= -MIN_NORMAL_KEY) & (k < MIN_NORMAL_KEY)
        return _key_to_float(jnp.where(zero_band, 0, k))

    def bis(b, r):
        cand = r + lax.shift_left(jnp.int32(1), jnp.int32(31) - b)
        valid = (cand >= NEGINF_KEY) & (cand <= POSINF_KEY)
        cand_f = as_float(jnp.where(valid, cand, 0))
        return jnp.where(valid & (count(lambda st, kp: st >= cand_f) >= k_f), cand, r)

    tau = as_float(lax.fori_loop(0, 32, bis, jnp.full(vec, NEGINF_KEY, I32)))
    need = k_f - count(lambda kt, kp: kt > tau)
    tie = (count(lambda kt, kp: kt == tau) > need) & (tau > -jnp.inf)

    def resolve():
        def ib(b, x):
            c = x + lax.shift_left(jnp.int32(1), jnp.int32(idx_bits - 1) - b)
            cnt = count(lambda kt, kp: (kt == tau) & (kp < c))
            return jnp.where(cnt < need, c, x)

        x = lax.fori_loop(0, idx_bits, ib, jnp.zeros(vec, I32))
        return jnp.where(tie, x, IDX_ALL)

    any_tie = jnp.max(jnp.where(tie, 1.0, 0.0)) > 0.5
    tau_i = lax.cond(any_tie, resolve, lambda: jnp.full(vec, IDX_ALL, I32))
    return tau, tau_i


DSA_TS = 256


def _dsa_prompt_kernel(k_sel, idx_bits, qi_ref, wit_ref, qa_ref, ki_ref, ka_ref, vat_ref,
                       oa_ref, key_ref, rel0_ref, rel1_ref, s0_ref, s1_ref, p0_ref, p1_ref):
    i = pl.program_id(1)
    ts = DSA_TS
    n_tiles = (i * QBLK + QBLK + ts - 1) // ts
    lane = lax.broadcasted_iota(I32, (QBLK, LANES), 1)
    left = lane < HEAD_DIM
    krow = lax.broadcasted_iota(I32, (ts, QBLK), 0)
    qpos = i * QBLK + lax.broadcasted_iota(I32, (ts, QBLK), 1)

    zero = jnp.zeros((QBLK, LANES), BF16)
    qi_parts = []
    for p in range(Q_I_W // LANES):
        chunk = qi_ref[:, p * LANES:(p + 1) * LANES]
        qi_parts.append(jnp.where(left, chunk, zero))
        qi_parts.append(jnp.where(left, zero, chunk))
    qi_all = jnp.concatenate(qi_parts, axis=0)
    wt = wit_ref[0]

    last = n_tiles - 1
    n_pairs = (n_tiles + 1) // 2

    def tile_off(j):
        return pl.multiple_of(jnp.minimum(j, last) * ts, ts)

    def rel_tile(j):
        return _dot_nt(ki_ref[pl.ds(tile_off(j), ts), :], qi_all)

    def score_step(j, cur_ref, nxt_ref):
        rel = cur_ref[...]
        nxt_ref[...] = rel_tile(j + 1)
        sc = jnp.zeros((ts, QBLK), F32)
        for h in range(N_IDX_HEADS):
            sc = sc + wt[h:h + 1, :] * jnp.maximum(rel[:, h * QBLK:(h + 1) * QBLK], 0.0)
        sc = jnp.where(j * ts + krow <= qpos, sc, -jnp.inf)
        key_ref[pl.ds(pl.multiple_of(j * ts, ts), ts), :] = sc

    def score_pair(jj, _):
        score_step(2 * jj, rel0_ref, rel1_ref)
        score_step(2 * jj + 1, rel1_ref, rel0_ref)
        return 0

    rel0_ref[...] = rel_tile(0)
    lax.fori_loop(0, n_pairs, score_pair, 0)
    tau, tau_i = _topk_threshold(key_ref, n_pairs, 2 * ts, 0, QBLK, k_sel, idx_bits)

    q4 = []
    for c in range(N_KV_A):
        parts = []
        for g in range(GROUP_A):
            chunk = qa_ref[:, g * LANES:(g + 1) * LANES]
            parts.append(jnp.where(left, chunk, zero) if c == 0 else jnp.where(left, zero, chunk))
        q4.append(jnp.concatenate(parts, axis=0))

    cols4 = GROUP_A * QBLK
    kb_per_tile = ts // KBLK

    def s_tile(j):
        k = ka_ref[pl.ds(tile_off(j), ts), :]
        return [_dot_nt(k, q4[c]) for c in range(N_KV_A)]

    ones_rows = jnp.ones((SUM_ROWS, ts), BF16)

    def values(j):
        jb = jnp.clip(j, 0, last) * kb_per_tile
        v = jnp.concatenate([vat_ref[0, jb + q] for q in range(kb_per_tile)], axis=1)
        return jnp.concatenate([v, ones_rows], axis=0)

    def pv(vals, p_ref, c):
        return _dot(vals, p_ref[c])

    def attn_step(j, carry, s_cur, s_nxt, p_cur, p_prev):
        kt = key_ref[pl.ds(tile_off(j), ts), :]
        kpos = j * ts + krow
        sel = ((kt > tau) | ((kt == tau) & (kpos <= tau_i))) & (kpos <= qpos)
        bias = jnp.where(sel, 0.0, -jnp.inf)
        bias4 = jnp.concatenate([bias] * GROUP_A, axis=1)
        nxt = s_tile(j + 1)
        vals = values(j - 1)
        out = []
        for c in range(N_KV_A):
            m, acc = carry[c]
            s = s_cur[c] + bias4
            s_nxt[c] = nxt[c]
            m_new = jnp.maximum(m, jnp.max(_fold_rows(s, jnp.maximum), axis=0, keepdims=True))
            m_safe = jnp.where(m_new == -jnp.inf, 0.0, m_new)
            acc = (acc + pv(vals, p_prev, c)) * jnp.exp2(m - m_safe)
            p_cur[c] = jnp.exp2(s - m_safe).astype(BF16)
            out.append((m_new, acc))
        return tuple(out)

    def attn_pair(jj, carry):
        carry = attn_step(2 * jj, carry, s0_ref, s1_ref, p0_ref, p1_ref)
        return attn_step(2 * jj + 1, carry, s1_ref, s0_ref, p1_ref, p0_ref)

    first = s_tile(0)
    for c in range(N_KV_A):
        s0_ref[c] = first[c]
    p1_ref[...] = jnp.zeros_like(p1_ref)
    init = tuple((jnp.full((1, cols4), -jnp.inf, F32), jnp.zeros((LANES + SUM_ROWS, cols4), F32))
                 for _ in range(N_KV_A))
    res = lax.fori_loop(0, n_pairs, attn_pair, init)
    vals = values(2 * n_pairs - 1)
    o = []
    for c in range(N_KV_A):
        acc = res[c][1] + pv(vals, p1_ref, c)
        o.append(acc[0:LANES] / acc[LANES:LANES + 1])
    top = lax.broadcasted_iota(I32, (LANES, QBLK), 0) < HEAD_DIM
    for g in range(GROUP_A):
        cs = slice(g * QBLK, (g + 1) * QBLK)
        oa_ref[:, g * LANES:(g + 1) * LANES] = jnp.where(top, o[0][:, cs], o[1][:, cs]).T


def _dsa_prompt(qi, wit, qa, kib, kab, vat, batch, seq):
    nq = seq // QBLK
    assert seq % (2 * DSA_TS) == 0, "key tiles are visited in pairs"
    k_sel = max(1, min(TOPK_MAX, seq // 4))
    idx_bits = max(1, int(seq - 1).bit_length())
    qspec = lambda n: pl.BlockSpec((QBLK, n), lambda b, i: (b * nq + i, 0))
    kspec = pl.BlockSpec((seq, LANES), lambda b, i: (b, 0))
    return pl.pallas_call(
        functools.partial(_dsa_prompt_kernel, k_sel, idx_bits),
        grid=(batch, nq),
        in_specs=[qspec(Q_I_W), pl.BlockSpec((1, N_IDX_HEADS, QBLK), lambda b, i: (b, 0, i)), qspec(Q_A_W),
                  kspec, kspec,
                  pl.BlockSpec((1, seq // KBLK, N_KV_A * HEAD_DIM, KBLK), lambda b, i: (b, 0, 0, 0))],
        out_specs=qspec(Q_A_W),
        out_shape=jax.ShapeDtypeStruct((batch * seq, Q_A_W), F32),
        scratch_shapes=[pltpu.VMEM((seq, QBLK), F32)]
                       + [pltpu.VMEM((DSA_TS, N_IDX_HEADS * QBLK), F32)] * 2
                       + [pltpu.VMEM((N_KV_A, DSA_TS, GROUP_A * QBLK), F32)] * 2
                       + [pltpu.VMEM((N_KV_A, DSA_TS, GROUP_A * QBLK), BF16)] * 2,
        compiler_params=_cparams("parallel", "arbitrary"),
        name="dsa_prompt",
    )(qi, wit, qa, kib, kab, vat)


def _later_matrix(n, key_axis):
    r = lax.broadcasted_iota(I32, (n, n), 0)
    c = lax.broadcasted_iota(I32, (n, n), 1)
    return jnp.where(r > c if key_axis == 1 else c > r, 1.0, 0.0).astype(BF16)


def _sb_tile(z, strict, later_m, carry, key_axis):
    soft = jnp.log(1.0 + jnp.exp(-jnp.abs(z)))
    log_beta = jnp.minimum(z, 0.0) - soft
    log_rest = log_beta - z
    if strict is not None:
        log_rest = jnp.where(strict, log_rest, 0.0)
    hi = log_rest.astype(BF16)
    lo = (log_rest - hi.astype(F32)).astype(BF16)
    if key_axis == 1:
        later = _dot(hi, later_m) + _dot(lo, later_m)
        first_l, first_r = later[:, 0:1], log_rest[:, 0:1]
    else:
        later = _dot(later_m, hi) + _dot(later_m, lo)
        first_l, first_r = later[0:1, :], log_rest[0:1, :]
    a = jnp.exp(log_beta + later + carry)
    if strict is not None:
        a = jnp.where(strict, a, 0.0)
    return a, carry + first_l + first_r


SB_TS = 128


def _sb_prompt_kernel(qb_ref, kb_ref, vbt_ref, ob_ref, acc_ref, carry_ref, z_ref):
    i = pl.program_id(1)
    ts = SB_TS
    n_pairs = N_HEADS_B // 2
    pcols = 2 * QBLK
    cols = n_pairs * pcols
    lane = lax.broadcasted_iota(I32, (QBLK, LANES), 1)
    left = lane < HEAD_DIM
    zero = jnp.zeros((QBLK, LANES), BF16)
    later_m = _later_matrix(ts, 0)
    krow = lax.broadcasted_iota(I32, (ts, cols), 0)
    qpos = i * QBLK + (lax.broadcasted_iota(I32, (ts, cols), 1) & (QBLK - 1))
    top = lax.broadcasted_iota(I32, (LANES, QBLK), 0) < HEAD_DIM
    pair = [slice(p * LANES, (p + 1) * LANES) for p in range(n_pairs)]

    q2 = []
    for p in range(n_pairs):
        chunk = qb_ref[:, pair[p]]
        q2.append(jnp.concatenate([jnp.where(left, chunk, zero), jnp.where(left, zero, chunk)], axis=0))
    acc_ref[...] = jnp.zeros_like(acc_ref)
    carry_ref[...] = jnp.zeros_like(carry_ref)

    def cond(state):
        j, go = state
        return jnp.logical_and(j >= 0, go)

    def logits(j):
        off = pl.multiple_of(jnp.maximum(j, 0) * ts, ts)
        return jnp.concatenate([_dot_nt(kb_ref[pl.ds(off, ts), pair[p]], q2[p]) for p in range(n_pairs)],
                               axis=1)

    def body(state):
        j, _ = state
        z = z_ref[...]
        z_next = logits(j - 1)
        strict = (j * ts + krow) < qpos
        a, new_carry = _sb_tile(z, strict, later_m, carry_ref[0:1, :], 0)
        ab = a.astype(BF16)
        for p in range(n_pairs):
            acc_ref[p] += _dot(vbt_ref[0, j, pair[p], :], ab[:, p * pcols:(p + 1) * pcols])
        carry_ref[...] = jnp.broadcast_to(new_carry, carry_ref.shape)
        z_ref[...] = z_next
        return j - 1, jnp.max(new_carry) >= SB_STOP

    z_ref[...] = logits(i)
    lax.while_loop(cond, body, (i, jnp.bool_(True)))
    for p in range(n_pairs):
        acc = acc_ref[p]
        ob_ref[:, pair[p]] = jnp.where(top, acc[:, 0:QBLK], acc[:, QBLK:pcols]).T


def _sb_prompt(qb, kbb, vbt, batch, seq):
    nq = seq // QBLK
    qspec = pl.BlockSpec((QBLK, Q_B_W), lambda b, i: (b * nq + i, 0))
    return pl.pallas_call(
        _sb_prompt_kernel,
        grid=(batch, nq),
        in_specs=[qspec, pl.BlockSpec((seq, Q_B_W), lambda b, i: (b, 0)),
                  pl.BlockSpec((1, seq // KBLK, Q_B_W, KBLK), lambda b, i: (b, 0, 0, 0))],
        out_specs=qspec,
        out_shape=jax.ShapeDtypeStruct((batch * seq, Q_B_W), F32),
        scratch_shapes=[pltpu.VMEM((N_HEADS_B // 2, LANES, 2 * QBLK), F32),
                        pltpu.VMEM((SUBLANES, N_HEADS_B * QBLK), F32),
                        pltpu.VMEM((SB_TS, N_HEADS_B * QBLK), F32)],
        compiler_params=_cparams("parallel", "arbitrary"),
        name="sb_prompt",
    )(qb, kbb, vbt)


TSLOTS = 8
SROWS = TSLOTS * N_HEADS_A


def _largest_divisor(n, cap):
    return max(x for x in range(1, cap + 1) if n % x == 0)


def _page_copies(pool_hbm, pt_ref, seq, buf, slot, sem, n_pages, start):
    def body(p, _):
        cp = pltpu.make_async_copy(pool_hbm.at[pt_ref[seq, p]], buf.at[slot, p], sem.at[slot])
        if start:
            cp.start()
        else:
            cp.wait()
        return 0

    lax.fori_loop(0, n_pages, body, 0)


def _page_ring(pool_hbm, pt_ref, buf, sem, n_pages):
    b = pl.program_id(0)
    slot = b & 1

    @pl.when(b == 0)
    def _():
        _page_copies(pool_hbm, pt_ref, 0, buf, 0, sem, n_pages, True)

    @pl.when(b + 1 < pl.num_programs(0))
    def _():
        _page_copies(pool_hbm, pt_ref, b + 1, buf, 1 - slot, sem, n_pages, True)

    _page_copies(pool_hbm, pt_ref, b, buf, slot, sem, n_pages, False)
    return slot


def _dsa_sample_scores_kernel(n_pages, page, ppt, pt_ref, qi_ref, w_ref, kin_ref, kidx_hbm,
                              key_ref, buf, sem):
    slot = _page_ring(kidx_hbm, pt_ref, buf, sem, n_pages)
    ts = ppt * page
    qi = qi_ref[0]
    w = w_ref[0]

    def scores(kt):
        wrel = w * jnp.maximum(_dot(qi, kt), 0.0)
        return jnp.sum(wrel.reshape(TSLOTS, N_IDX_HEADS, kt.shape[1]), axis=1)

    for t in range(n_pages // ppt):
        kt = jnp.concatenate([buf[slot, t * ppt + q] for q in range(ppt)], axis=1).astype(BF16)
        key_ref[0, :, t * ts:(t + 1) * ts] = scores(kt)
    past = n_pages * page
    qslot = lax.broadcasted_iota(I32, (TSLOTS, page), 0)
    u = lax.broadcasted_iota(I32, (TSLOTS, page), 1)
    key_ref[0, :, past:past + page] = jnp.where(u <= qslot, scores(kin_ref[0]), -jnp.inf)


def _dsa_sample_select_kernel(k_sel, idx_bits, n_tiles, ts, key_ref, tau_ref, taui_ref):
    tau, tau_i = _topk_threshold(key_ref, n_tiles, ts, 1, key_ref.shape[0], k_sel, idx_bits)
    tau_ref[...] = jnp.broadcast_to(tau, tau_ref.shape)
    taui_ref[...] = jnp.broadcast_to(tau_i, taui_ref.shape)


def _dsa_sample_attend_kernel(n_pages, page, ppt, pt_ref, qa_ref, key_ref, tau_ref, taui_ref, kvn_ref, kva_hbm,
                              o_ref, buf, sem, s_ref):
    slot = _page_ring(kva_hbm, pt_ref, buf, sem, n_pages)
    ts = ppt * page
    past = n_pages * page
    kfeat = N_KV_A * HEAD_DIM
    qa = qa_ref[0]
    tau = tau_ref[:, 0:1]
    tau_i = taui_ref[:, 0:1]

    n_t = n_pages // ppt

    def masked_scores(kt, kpos0, k):
        n = kt.shape[1]
        kpos = kpos0 + lax.broadcasted_iota(I32, (TSLOTS, n), 1)
        qpos = past + lax.broadcasted_iota(I32, (TSLOTS, n), 0)
        sel = ((kt > tau) | ((kt == tau) & (kpos <= tau_i))) & (kpos <= qpos)
        bias = jnp.where(sel, 0.0, -jnp.inf)
        bias_rows = jnp.concatenate(
            [jnp.broadcast_to(bias[t:t + 1, :], (N_HEADS_A, n)) for t in range(TSLOTS)], axis=0)
        return _dot(qa, k) + bias_rows

    def pages(t, f0):
        return jnp.concatenate([buf[slot, t * ppt + q, f0:f0 + kfeat, :] for q in range(ppt)],
                               axis=1).astype(BF16)

    mx = jnp.full((SROWS, ts), -jnp.inf, F32)
    for t in range(n_t):
        s = masked_scores(key_ref[0, :, t * ts:(t + 1) * ts], t * ts, pages(t, 0))
        s_ref[:, t * ts:(t + 1) * ts] = s
        mx = jnp.maximum(mx, s)
    s_new = masked_scores(key_ref[0, :, past:past + page], past, kvn_ref[0, 0:kfeat, :])
    m = jnp.maximum(jnp.max(mx, axis=1, keepdims=True), jnp.max(s_new, axis=1, keepdims=True))
    m = jnp.where(m == -jnp.inf, 0.0, m)

    p_new = jnp.exp2(s_new - m)
    acc = _dot_nt(p_new.astype(BF16), kvn_ref[0, kfeat:2 * kfeat, :])
    psum = jnp.zeros((SROWS, ts), F32)
    for t in range(n_t):
        p = jnp.exp2(s_ref[:, t * ts:(t + 1) * ts] - m)
        psum = psum + p
        acc = acc + _dot_nt(p.astype(BF16), pages(t, kfeat))
    l = jnp.sum(psum, axis=1, keepdims=True) + jnp.sum(p_new, axis=1, keepdims=True)
    o_ref[0] = acc / jnp.where(l > 0.0, l, 1.0)


def _dsa_sample(qi_st, w_st, qa_st, kidx_pool, kva_pool, kin_t, kvn_t, page_table, dec_seq):
    n_seq, n_pages = page_table.shape
    page = kidx_pool.shape[2]
    nk = (n_pages + 1) * page
    ppt = _largest_divisor(n_pages, 8)
    sel_ts = _largest_divisor(n_pages + 1, 8) * page
    k_sel = max(1, min(TOPK_MAX, (n_pages * page + dec_seq) // 4))
    idx_bits = max(1, int(nk - 1).bit_length())
    seq_spec = lambda shp: pl.BlockSpec((1,) + shp, lambda b, pt: (b, 0, 0))
    hbm_spec = pl.BlockSpec(memory_space=pl.ANY)
    ring = lambda feat: [pltpu.VMEM((2, n_pages, feat, page), F32), pltpu.SemaphoreType.DMA((2,))]

    keys = pl.pallas_call(
        functools.partial(_dsa_sample_scores_kernel, n_pages, page, ppt),
        grid_spec=pltpu.PrefetchScalarGridSpec(
            num_scalar_prefetch=1, grid=(n_seq,),
            in_specs=[seq_spec((SROWS, IDX_DIM)), seq_spec((SROWS, 1)), seq_spec((IDX_DIM, page)), hbm_spec],
            out_specs=seq_spec((TSLOTS, nk)),
            scratch_shapes=ring(IDX_DIM)),
        out_shape=jax.ShapeDtypeStruct((n_seq, TSLOTS, nk), F32),
        compiler_params=_cparams("arbitrary"),
        name="dsa_sample_scores",
    )(page_table, qi_st, w_st, kin_t, kidx_pool)

    rows = n_seq * dec_seq
    tau, tau_i = pl.pallas_call(
        functools.partial(_dsa_sample_select_kernel, k_sel, idx_bits, nk // sel_ts, sel_ts),
        grid=(1,),
        in_specs=[_const_spec((rows, nk))],
        out_specs=[pl.BlockSpec((rows, LANES), lambda i: (0, 0))] * 2,
        out_shape=[jax.ShapeDtypeStruct((rows, LANES), F32), jax.ShapeDtypeStruct((rows, LANES), I32)],
        compiler_params=_cparams("arbitrary"),
        name="dsa_sample_select",
    )(keys[:, :dec_seq].reshape(rows, nk))
    pad_slots = lambda a: jnp.pad(a.reshape(n_seq, dec_seq, LANES),
                                  ((0, 0), (0, TSLOTS - dec_seq), (0, 0))).reshape(n_seq * TSLOTS, LANES)
    tau, tau_i = pad_slots(tau), pad_slots(tau_i)

    row_spec = pl.BlockSpec((TSLOTS, LANES), lambda b, pt: (b, 0))
    return pl.pallas_call(
        functools.partial(_dsa_sample_attend_kernel, n_pages, page, ppt),
        grid_spec=pltpu.PrefetchScalarGridSpec(
            num_scalar_prefetch=1, grid=(n_seq,),
            in_specs=[seq_spec((SROWS, LANES)), seq_spec((TSLOTS, nk)), row_spec, row_spec,
                      seq_spec((KV_A_W, page)), hbm_spec],
            out_specs=seq_spec((SROWS, LANES)),
            scratch_shapes=ring(KV_A_W) + [pltpu.VMEM((SROWS, n_pages * page), F32)]),
        out_shape=jax.ShapeDtypeStruct((n_seq, SROWS, LANES), F32),
        compiler_params=_cparams("arbitrary"),
        name="dsa_sample_attend",
    )(page_table, qa_st, keys, tau, tau_i, kvn_t, kva_pool)


def _sb_sample_kernel(n_pages, page, pt_ref, q_ref, kvn_ref, kvb_hbm, o_ref, buf, sem, acc_ref, carry_ref):
    b = pl.program_id(0)
    q = q_ref[0]
    later_m = _later_matrix(page, 1)
    last = n_pages - 1

    def page_copy(pg):
        slot = (last - pg) & 1
        return pltpu.make_async_copy(kvb_hbm.at[pt_ref[b, pg]], buf.at[slot], sem.at[slot])

    def tile(kt, vt, strict):
        a, new_carry = _sb_tile(_dot(q, kt), strict, later_m, carry_ref[:, 0:1], 1)
        acc_ref[...] += _dot_nt(a.astype(BF16), vt)
        carry_ref[...] = jnp.broadcast_to(new_carry, carry_ref.shape)
        return jnp.max(new_carry) >= SB_STOP

    page_copy(last).start()
    acc_ref[...] = jnp.zeros_like(acc_ref)
    carry_ref[...] = jnp.zeros_like(carry_ref)
    qslot = lax.broadcasted_iota(I32, (SROWS, page), 0) // N_HEADS_B
    u = lax.broadcasted_iota(I32, (SROWS, page), 1)
    go0 = tile(kvn_ref[0, 0:Q_B_W, :], kvn_ref[0, Q_B_W:2 * Q_B_W, :], u < qslot)

    def cond(state):
        pg, go = state
        return jnp.logical_and(pg >= 0, go)

    def body(state):
        pg, _ = state
        page_copy(pg).wait()

        @pl.when(pg > 0)
        def _():
            page_copy(pg - 1).start()

        slot = (last - pg) & 1
        go = tile(buf[slot, 0:Q_B_W, :].astype(BF16), buf[slot, Q_B_W:2 * Q_B_W, :].astype(BF16), None)
        return pg - 1, go

    pg_end, _ = lax.while_loop(cond, body, (jnp.int32(last), go0))

    @pl.when(pg_end >= 0)
    def _():
        page_copy(pg_end).wait()

    r = lax.broadcasted_iota(I32, (SROWS, Q_B_W), 0)
    c = lax.broadcasted_iota(I32, (SROWS, Q_B_W), 1)
    own = (r % N_HEADS_B) == (c // HEAD_DIM)
    diag = jnp.where(own, acc_ref[...], 0.0)
    o_ref[0] = jnp.sum(diag.reshape(TSLOTS, N_HEADS_B, Q_B_W), axis=1)


def _sb_sample(qb_bd, kvb_pool, kvn_t, page_table):
    n_seq, n_pages = page_table.shape
    page = kvb_pool.shape[2]
    seq_spec = lambda shp: pl.BlockSpec((1,) + shp, lambda b, pt: (b, 0, 0))
    return pl.pallas_call(
        functools.partial(_sb_sample_kernel, n_pages, page),
        grid_spec=pltpu.PrefetchScalarGridSpec(
            num_scalar_prefetch=1, grid=(n_seq,),
            in_specs=[seq_spec((SROWS, Q_B_W)), seq_spec((KV_B_W, page)), pl.BlockSpec(memory_space=pl.ANY)],
            out_specs=seq_spec((TSLOTS, Q_B_W)),
            scratch_shapes=[pltpu.VMEM((2, KV_B_W, page), F32), pltpu.SemaphoreType.DMA((2,)),
                            pltpu.VMEM((SROWS, Q_B_W), F32), pltpu.VMEM((SROWS, LANES), F32)]),
        out_shape=jax.ShapeDtypeStruct((n_seq, TSLOTS, Q_B_W), F32),
        compiler_params=_cparams("arbitrary"),
        name="sb_sample",
    )(page_table, qb_bd, kvn_t, kvb_pool)


def _out_kernel(oa_ref, ob_ref, gate_ref, x_ref, wa_ref, wb_ref, wo_ref, nf_ref,
                wg_ref, wu_ref, wd_ref, nfin_ref, y_ref):
    ba = _dot(oa_ref[...].astype(BF16), wa_ref[...])
    bb = _dot(ob_ref[...].astype(BF16), wb_ref[...])
    merged = (gate_ref[:, 0:D_MODEL].astype(F32) * ba
              + gate_ref[:, D_MODEL:2 * D_MODEL].astype(F32) * bb)
    h1 = x_ref[...] + _dot(merged.astype(BF16), wo_ref[...])
    ms = jnp.mean(h1 * h1, axis=-1, keepdims=True)
    hn = ((h1 * lax.rsqrt(ms + RMS_EPS)) * nf_ref[...]).astype(BF16)
    acc = jnp.zeros_like(h1)
    for c in range(D_FF // FFN_CHUNK):
        sl = slice(c * FFN_CHUNK, (c + 1) * FFN_CHUNK)
        gg = _dot(hn, wg_ref[:, sl])
        uu = _dot(hn, wu_ref[:, sl])
        act = (gg * (1.0 / (1.0 + jnp.exp(-gg)))) * uu
        acc = acc + _dot(act.astype(BF16), wd_ref[sl, :])
    h2 = h1 + acc
    ms2 = jnp.mean(h2 * h2, axis=-1, keepdims=True)
    y_ref[...] = (h2 * lax.rsqrt(ms2 + RMS_EPS)) * nfin_ref[...]


def _out(oa, ob, gates, x2d, wts, tm):
    rows = x2d.shape[0]
    row_spec = lambda n: pl.BlockSpec((tm, n), lambda i: (i, 0))
    return pl.pallas_call(
        _out_kernel,
        grid=(rows // tm,),
        in_specs=[row_spec(Q_A_W), row_spec(Q_B_W), row_spec(2 * D_MODEL), row_spec(D_MODEL)]
                 + [_const_spec(w.shape) for w in wts],
        out_specs=row_spec(D_MODEL),
        out_shape=jax.ShapeDtypeStruct((rows, D_MODEL), F32),
        compiler_params=_cparams("parallel"),
        name="out",
    )(oa, ob, gates, x2d, *wts)


def _rope_tables(pos):
    inv_freq = ROPE_THETA ** (-jnp.arange(HALF, dtype=F32) / HALF)
    ang = pos.astype(F32)[:, None] * inv_freq[None, :]
    cos, sin = jnp.cos(ang), jnp.sin(ang)
    cs = jnp.concatenate([cos, cos, cos, cos], axis=1)
    sn = jnp.concatenate([-sin, sin, -sin, sin], axis=1)
    return cs, sn, cos.T, sin.T


def _pack_w_in(w):
    d = w.shape[0]
    o = np.cumsum([0, Q_A_W, KV_A_W, Q_I_W, IDX_DIM, N_IDX_HEADS, Q_B_W, KV_B_W, 2 * D_MODEL])
    kw = N_KV_A * HEAD_DIM
    qa = w[:, o[0]:o[1]].reshape(d, N_HEADS_A, HEAD_DIM)[:, np.array(QA_HEAD_ORDER), :].reshape(d, Q_A_W)
    ka, va = w[:, o[1]:o[1] + kw], w[:, o[1] + kw:o[2]]
    qi, ki, wi, qb = w[:, o[2]:o[3]], w[:, o[3]:o[4]], w[:, o[4]:o[5]], w[:, o[5]:o[6]]
    kb, vb = w[:, o[6]:o[6] + Q_B_W], w[:, o[6] + Q_B_W:o[7]]
    g = w[:, o[7]:o[8]]
    w_tok = jnp.concatenate([qa, qi, qb, g], axis=1).astype(BF16)
    wi_pad = jnp.pad(wi, ((0, 0), (0, HEAD_DIM - N_IDX_HEADS)))
    w_feat = jnp.concatenate([ka, va, ki, wi_pad, kb, vb], axis=1).T.astype(BF16)
    return w_tok, w_feat


def kernel(x_prompt, x_sample, cache_kv_a, cache_k_idx, cache_kv_b, page_table, w_in, w_br_a, w_br_b,
           w_o, norm_attn, norm_ffn, w_ffn_gate, w_ffn_up, w_ffn_down, norm_final):
    depth = w_in.shape[0]
    assert depth == 1, "single-layer step"
    batch, seq, d = x_prompt.shape
    assert d == D_MODEL and seq % ROW_TILE == 0
    layer = 0

    w_tok, w_feat = _pack_w_in(w_in[layer])
    wa = w_br_a[layer].reshape(N_HEADS_A, HEAD_DIM, D_MODEL)[np.array(QA_HEAD_ORDER)].reshape(Q_A_W, D_MODEL)
    out_wts = (wa.astype(BF16), w_br_b[layer].astype(BF16), w_o[layer].astype(BF16),
               norm_ffn[layer].reshape(1, D_MODEL),
               w_ffn_gate[layer].astype(BF16), w_ffn_up[layer].astype(BF16), w_ffn_down[layer].astype(BF16),
               norm_final.reshape(1, D_MODEL))
    gain = norm_attn[layer].reshape(1, D_MODEL)

    xp = x_prompt.reshape(batch * seq, D_MODEL)
    (kva_p, kidx_p, kvb_p, qa, qi, wit, qb, gates, kab, kib, kbb, vat, vbt) = _project(
        xp, gain, w_tok, w_feat, *_rope_tables(jnp.arange(seq, dtype=I32)), batch, ROW_TILE)
    oa = _dsa_prompt(qi, wit, qa, kib, kab, vat, batch, seq)
    ob = _sb_prompt(qb, kbb, vbt, batch, seq)
    y_prompt = _out(oa, ob, gates, xp, out_wts, ROW_TILE).reshape(batch, seq, D_MODEL)

    new_kv_a_p = kva_p.reshape(depth, batch, 2, N_KV_A, HEAD_DIM, seq).transpose(0, 1, 5, 2, 3, 4)
    new_k_idx_p = kidx_p.reshape(depth, batch, IDX_DIM, seq).transpose(0, 1, 3, 2)
    new_kv_b_p = kvb_p.reshape(depth, batch, 2, N_HEADS_B, HEAD_DIM, seq).transpose(0, 1, 5, 2, 3, 4)

    n_seq, dec_seq, _ = x_sample.shape
    n_pages = page_table.shape[1]
    n_pool, page = cache_kv_a.shape[1], cache_kv_a.shape[2]
    past = n_pages * page
    rows_s = n_seq * dec_seq
    assert dec_seq <= TSLOTS and rows_s % KBLK == 0 and page == LANES
    xs = x_sample.reshape(rows_s, D_MODEL)
    cs_s, sn_s, csf_s, snf_s = _rope_tables(past + jnp.arange(dec_seq, dtype=I32))
    tabs_s = (jnp.tile(cs_s, (n_seq, 1)), jnp.tile(sn_s, (n_seq, 1)),
              jnp.tile(csf_s, (1, n_seq)), jnp.tile(snf_s, (1, n_seq)))
    (kva_s, kidx_s, kvb_s, qa_s, qi_s, wit_s, qb_s, gates_s, _, _, _, _, _) = _project(
        xs, gain, w_tok, w_feat, *tabs_s, 1, rows_s)

    def slots(a):
        return jnp.pad(a, ((0, 0), (0, TSLOTS - dec_seq)) + ((0, 0),) * (a.ndim - 2))

    def new_tile(a):
        t = a[0].reshape(a.shape[1], n_seq, dec_seq).transpose(1, 0, 2)
        return jnp.pad(t, ((0, 0), (0, 0), (0, page - dec_seq))).astype(BF16)

    qi_st = slots(qi_s.reshape(n_seq, dec_seq, N_IDX_HEADS, IDX_DIM)).reshape(n_seq, SROWS, IDX_DIM)
    w_st = slots(wit_s[0].T.reshape(n_seq, dec_seq, N_IDX_HEADS)).reshape(n_seq, SROWS, 1)
    qa5 = qa_s.reshape(n_seq, dec_seq, GROUP_A, N_KV_A, HEAD_DIM).transpose(0, 1, 3, 2, 4)
    eye2 = jnp.eye(N_KV_A, dtype=qa_s.dtype)
    qa_st = qa5[:, :, :, :, None, :] * eye2[None, None, :, None, :, None]
    qa_st = slots(qa_st.reshape(n_seq, dec_seq, N_HEADS_A, LANES)).reshape(n_seq, SROWS, LANES)
    kidx_pool = cache_k_idx[layer].transpose(0, 2, 1)
    kva_pool = cache_kv_a[layer].transpose(0, 2, 3, 4, 1).reshape(n_pool, KV_A_W, page)
    kvb_pool = cache_kv_b[layer].transpose(0, 2, 3, 4, 1).reshape(n_pool, KV_B_W, page)
    o_st = _dsa_sample(qi_st, w_st, qa_st, kidx_pool, kva_pool, new_tile(kidx_s), new_tile(kva_s),
                       page_table, dec_seq)
    o6 = o_st.reshape(n_seq, TSLOTS, N_KV_A, GROUP_A, N_KV_A, HEAD_DIM)[:, :dec_seq]
    oa_s = jnp.stack([o6[:, :, c, :, c, :] for c in range(N_KV_A)], axis=2).reshape(rows_s, Q_A_W)

    qb4 = qb_s.reshape(n_seq, dec_seq, N_HEADS_B, HEAD_DIM)
    eye8 = jnp.eye(N_HEADS_B, dtype=qb_s.dtype)
    qb_bd = qb4[:, :, :, None, :] * eye8[None, None, :, :, None]
    qb_bd = slots(qb_bd.reshape(n_seq, dec_seq, N_HEADS_B, Q_B_W)).reshape(n_seq, SROWS, Q_B_W)
    ob_st = _sb_sample(qb_bd, kvb_pool, new_tile(kvb_s), page_table)
    ob_s = ob_st[:, :dec_seq].reshape(rows_s, Q_B_W)

    out_wts_s = (w_br_a[layer].astype(BF16),) + out_wts[1:]
    y_sample = _out(oa_s, ob_s, gates_s, xs, out_wts_s, rows_s).reshape(n_seq, dec_seq, D_MODEL)

    return (y_prompt, y_sample, new_kv_a_p, new_k_idx_p, new_kv_b_p,
            kva_s[0].T.reshape(depth, n_seq, dec_seq, 2, N_KV_A, HEAD_DIM),
            kidx_s[0].T.reshape(depth, n_seq, dec_seq, IDX_DIM),
            kvb_s[0].T.reshape(depth, n_seq, dec_seq, 2, N_HEADS_B, HEAD_DIM))
```

```python
import functools

import jax
import jax.numpy as jnp
import numpy as np
from jax import lax
from jax.experimental import pallas as pl
from jax.experimental.pallas import tpu as pltpu

F32 = jnp.float32
BF16 = jnp.bfloat16
I32 = jnp.int32

D_MODEL = 1024
HEAD_DIM = 64
N_HEADS_A = 8
N_KV_A = 2
GROUP_A = N_HEADS_A // N_KV_A
N_IDX_HEADS = 8
IDX_DIM = 64
TOPK_MAX = 256
N_HEADS_B = 8
Q_A_W = N_HEADS_A * HEAD_DIM
KV_A_W = 2 * N_KV_A * HEAD_DIM
Q_I_W = N_IDX_HEADS * IDX_DIM
Q_B_W = N_HEADS_B * HEAD_DIM
KV_B_W = 2 * N_HEADS_B * HEAD_DIM
D_FF = -(-8 * D_MODEL // (3 * 256)) * 256
ROPE_THETA = 10000.0
RMS_EPS = 1e-6
QK_SCALE = HEAD_DIM ** -0.5
IDX_SCALE = IDX_DIM ** -0.5
HALF = HEAD_DIM // 2
LOG2_E = 1.4426950408889634

LANES = 128
SUBLANES = 8
QBLK = 128
KBLK = 128
SUM_ROWS = 16
FFN_CHUNK = 256
ROW_TILE = 512

TK_QA = 0
TK_QI = 512
TK_QB = 1024
TK_G = 1536
TK_W = 3584
FM_KA = 0
FM_VA = 128
FM_KI = 256
FM_WI = 320
FM_KB = 384
FM_VB = 896
FM_W = 1408
QA_HEAD_ORDER = (0, 4, 1, 5, 2, 6, 3, 7)

NEGINF_KEY = -2139095041
POSINF_KEY = 0x7F800000
MIN_NORMAL_KEY = 0x00800000
IDX_ALL = 1 << 30
SB_STOP = -105.0

VMEM_LIMIT = 60 * 1024 * 1024

NT_DIMS = (((1,), (1,)), ((), ()))


def _cparams(*sem):
    return pltpu.CompilerParams(dimension_semantics=sem, vmem_limit_bytes=VMEM_LIMIT)


def _const_spec(shape):
    nd = len(shape)
    return pl.BlockSpec(shape, lambda *_: (0,) * nd, pipeline_mode=pl.Buffered(1))


def _dot_nt(a, b):
    return lax.dot_general(a, b, NT_DIMS, preferred_element_type=F32)


def _dot(a, b):
    return jnp.dot(a, b, preferred_element_type=F32)


def _project_kernel(x_ref, g_ref, wt_ref, wf_ref, cs_ref, sn_ref, csf_ref, snf_ref,
                    kva_ref, kidx_ref, kvb_ref,
                    qa_ref, qi_ref, wit_ref, qb_ref, gate_ref,
                    kab_ref, kib_ref, kbb_ref, vat_ref, vbt_ref):
    x = x_ref[...]
    ms = jnp.mean(x * x, axis=-1, keepdims=True)
    xn = (x * lax.rsqrt(ms + RMS_EPS)) * g_ref[...]
    xb = xn.astype(BF16)
    tm = x.shape[0]
    cs = cs_ref[...]
    sn = sn_ref[...]
    lane = lax.broadcasted_iota(I32, (tm, LANES), 1)
    first_half = (lane & HALF) == 0

    def mm(c0, n):
        return _dot(xb, wt_ref[:, c0:c0 + n])

    def rope(y):
        partner = jnp.where(first_half, pltpu.roll(y, LANES - HALF, 1), pltpu.roll(y, HALF, 1))
        return y * cs + partner * sn

    ya = mm(TK_QA, Q_A_W)
    for c in range(Q_A_W // LANES):
        sl = slice(c * LANES, (c + 1) * LANES)
        qa_ref[:, sl] = (rope(ya[:, sl]) * (QK_SCALE * LOG2_E)).astype(BF16)
    yi = mm(TK_QI, Q_I_W)
    for c in range(Q_I_W // LANES):
        sl = slice(c * LANES, (c + 1) * LANES)
        qi_ref[:, sl] = rope(yi[:, sl]).astype(BF16)
    qb_ref[...] = (mm(TK_QB, Q_B_W) * QK_SCALE).astype(BF16)
    for c in range(2 * D_MODEL // 512):
        sl = slice(c * 512, (c + 1) * 512)
        y = mm(TK_G + c * 512, 512)
        gate_ref[:, sl] = (1.0 / (1.0 + jnp.exp(-y))).astype(BF16)

    csf = csf_ref[...]
    snf = snf_ref[...]

    def fm(r0, n):
        return _dot_nt(wf_ref[r0:r0 + n, :], xb)

    def rope_fm(y):
        lo, hi = y[0:HALF], y[HALF:HEAD_DIM]
        return jnp.concatenate([lo * csf - hi * snf, hi * csf + lo * snf], axis=0)

    ka = fm(FM_KA, N_KV_A * HEAD_DIM)
    ka = jnp.concatenate([rope_fm(ka[c * HEAD_DIM:(c + 1) * HEAD_DIM]) for c in range(N_KV_A)], axis=0)
    kva_ref[0, 0:N_KV_A * HEAD_DIM, :] = ka
    kab_ref[...] = ka.T.astype(BF16)
    va = fm(FM_VA, N_KV_A * HEAD_DIM)
    kva_ref[0, N_KV_A * HEAD_DIM:2 * N_KV_A * HEAD_DIM, :] = va
    ki = rope_fm(fm(FM_KI, IDX_DIM))
    kidx_ref[0] = ki
    kib_ref[...] = jnp.concatenate([ki, ki], axis=0).T.astype(BF16)
    wi = fm(FM_WI, HEAD_DIM)
    wit_ref[0] = (wi[0:N_IDX_HEADS] * (N_IDX_HEADS ** -0.5)) * IDX_SCALE
    kb = fm(FM_KB, Q_B_W)
    kvb_ref[0, 0:Q_B_W, :] = kb
    kbb_ref[...] = kb.T.astype(BF16)
    vb = fm(FM_VB, Q_B_W)
    kvb_ref[0, Q_B_W:2 * Q_B_W, :] = vb
    for q in range(tm // KBLK):
        sl = slice(q * KBLK, (q + 1) * KBLK)
        vat_ref[0, q] = va[:, sl].astype(BF16)
        vbt_ref[0, q] = vb[:, sl].astype(BF16)


def _project(x2d, gain, w_tok, w_feat, cs, sn, csf, snf, groups, tm):
    rows = x2d.shape[0]
    t = rows // groups
    nb = t // tm
    nkb = tm // KBLK
    row_spec = lambda n: pl.BlockSpec((tm, n), lambda i: (i, 0))
    tab_spec = pl.BlockSpec((tm, LANES), lambda i: (i % nb, 0))
    tabf_spec = pl.BlockSpec((HALF, tm), lambda i: (0, i % nb))
    fm_spec = lambda n: pl.BlockSpec((1, n, tm), lambda i: (i // nb, 0, i % nb))
    blk_spec = lambda n: pl.BlockSpec((1, nkb, n, KBLK), lambda i: (i // nb, i % nb, 0, 0))
    out_specs = [fm_spec(KV_A_W), fm_spec(IDX_DIM), fm_spec(KV_B_W),
                 row_spec(Q_A_W), row_spec(Q_I_W), fm_spec(N_IDX_HEADS), row_spec(Q_B_W), row_spec(2 * D_MODEL),
                 row_spec(LANES), row_spec(LANES), row_spec(Q_B_W),
                 blk_spec(N_KV_A * HEAD_DIM), blk_spec(Q_B_W)]
    fm_shape = lambda n, dt: jax.ShapeDtypeStruct((groups, n, t), dt)
    tk_shape = lambda n, dt: jax.ShapeDtypeStruct((rows, n), dt)
    blk_shape = lambda n: jax.ShapeDtypeStruct((groups, t // KBLK, n, KBLK), BF16)
    out_shape = [fm_shape(KV_A_W, F32), fm_shape(IDX_DIM, F32), fm_shape(KV_B_W, F32),
                 tk_shape(Q_A_W, BF16), tk_shape(Q_I_W, BF16), fm_shape(N_IDX_HEADS, F32),
                 tk_shape(Q_B_W, BF16), tk_shape(2 * D_MODEL, BF16),
                 tk_shape(LANES, BF16), tk_shape(LANES, BF16), tk_shape(Q_B_W, BF16),
                 blk_shape(N_KV_A * HEAD_DIM), blk_shape(Q_B_W)]
    return pl.pallas_call(
        _project_kernel,
        grid=(rows // tm,),
        in_specs=[row_spec(D_MODEL), _const_spec((1, D_MODEL)), _const_spec((D_MODEL, TK_W)),
                  _const_spec((FM_W, D_MODEL)), tab_spec, tab_spec, tabf_spec, tabf_spec],
        out_specs=out_specs,
        out_shape=out_shape,
        compiler_params=_cparams("parallel"),
        name="project",
    )(x2d, gain, w_tok, w_feat, cs, sn, csf, snf)


def _key_to_float(k):
    return lax.bitcast_convert_type(k ^ ((k >> 31) & jnp.int32(0x7FFFFFFF)), F32)


def _fold_rows(x, op, rows=SUBLANES):
    n = x.shape[0]
    assert n % rows == 0 and (n // rows) & (n // rows - 1) == 0
    while n > rows:
        n //= 2
        x = op(x[:n], x[n:])
    return x


def _count(key_ref, n_tiles, ts, key_axis, width, pred):
    if key_axis == 1:
        def body(j, acc):
            off = pl.multiple_of(j * ts, ts)
            kt = key_ref[:, pl.ds(off, ts)]
            kpos = j * ts + lax.broadcasted_iota(I32, (width, ts), 1)
            m = jnp.where(pred(kt, kpos, lambda v: v), 1.0, 0.0)
            s = m[:, 0:LANES]
            for c in range(1, ts // LANES):
                s = s + m[:, c * LANES:(c + 1) * LANES]
            return acc + s

        acc = lax.fori_loop(0, n_tiles, body, jnp.zeros((width, LANES), F32))
        return jnp.sum(acc, axis=1, keepdims=True)

    def body(j, acc):
        off = pl.multiple_of(j * ts, ts)
        kpos = j * ts + lax.broadcasted_iota(I32, (ts, LANES), 0)
        parts = []
        for g in range(width // LANES):
            gs = slice(g * LANES, (g + 1) * LANES)
            hit = pred(key_ref[pl.ds(off, ts), gs], kpos, lambda v, gs=gs: v[:, gs])
            parts.append(_fold_rows(jnp.where(hit, 1.0, 0.0), jnp.add))
        return acc + jnp.concatenate(parts, axis=1)

    acc = lax.fori_loop(0, n_tiles, body, jnp.zeros((SUBLANES, width), F32))
    return jnp.sum(acc, axis=0, keepdims=True)


def _topk_threshold(key_ref, n_tiles, ts, key_axis, width, k_sel, idx_bits):
    k_f = float(k_sel)
    vec = (width, 1) if key_axis == 1 else (1, width)
    count = functools.partial(_count, key_ref, n_tiles, ts, key_axis, width)

    def as_float(k):
        zero_band = (k >= -MIN_NORMAL_KEY) & (k < MIN_NORMAL_KEY)
        return _key_to_float(jnp.where(zero_band, 0, k))

    def bis(b, r):
        cand = r + lax.shift_left(jnp.int32(1), jnp.int32(31) - b)
        valid = (cand >= NEGINF_KEY) & (cand <= POSINF_KEY)
        cand_f = as_float(jnp.where(valid, cand, 0))
        return jnp.where(valid & (count(lambda st, kp, q: st >= q(cand_f)) >= k_f), cand, r)

    tau = as_float(lax.fori_loop(0, 32, bis, jnp.full(vec, NEGINF_KEY, I32)))
    need = k_f - count(lambda st, kp, q: st > q(tau))
    tie = (count(lambda st, kp, q: st == q(tau)) > need) & (tau > -jnp.inf)

    def resolve():
        def ib(b, x):
            c = x + lax.shift_left(jnp.int32(1), jnp.int32(idx_bits - 1) - b)
            cnt = count(lambda st, kp, q: (st == q(tau)) & (kp < q(c)))
            return jnp.where(cnt < need, c, x)

        x = lax.fori_loop(0, idx_bits, ib, jnp.zeros(vec, I32))
        return jnp.where(tie, x, IDX_ALL)

    any_tie = jnp.max(jnp.where(tie, 1.0, 0.0)) > 0.5
    tau_i = lax.cond(any_tie, resolve, lambda: jnp.full(vec, IDX_ALL, I32))
    return tau, tau_i


DSA_TS = 256
DSA_QBLK = 256


def _dsa_prompt_kernel(k_sel, idx_bits, qi_ref, wit_ref, qa_ref, ki_ref, ka_ref, vat_ref,
                       oa_ref, key_ref, rel0_ref, rel1_ref, s0_ref, s1_ref, p0_ref, p1_ref):
    QBLK = DSA_QBLK
    i = pl.program_id(1)
    ts = DSA_TS
    n_tiles = (i * QBLK + QBLK + ts - 1) // ts
    lane = lax.broadcasted_iota(I32, (QBLK, LANES), 1)
    left = lane < HEAD_DIM
    krow = lax.broadcasted_iota(I32, (ts, QBLK), 0)
    qpos = i * QBLK + lax.broadcasted_iota(I32, (ts, QBLK), 1)

    zero = jnp.zeros((QBLK, LANES), BF16)
    qi_parts = []
    for p in range(Q_I_W // LANES):
        chunk = qi_ref[:, p * LANES:(p + 1) * LANES]
        qi_parts.append(jnp.where(left, chunk, zero))
        qi_parts.append(jnp.where(left, zero, chunk))
    qi_all = jnp.concatenate(qi_parts, axis=0)
    wt = wit_ref[0]

    last = n_tiles - 1
    n_pairs = (n_tiles + 1) // 2

    def tile_off(j):
        return pl.multiple_of(jnp.minimum(j, last) * ts, ts)

    def rel_tile(j):
        return _dot_nt(ki_ref[pl.ds(tile_off(j), ts), :], qi_all)

    def score_step(j, cur_ref, nxt_ref):
        rel = cur_ref[...]
        nxt_ref[...] = rel_tile(j + 1)
        sc = jnp.zeros((ts, QBLK), F32)
        for h in range(N_IDX_HEADS):
            sc = sc + wt[h:h + 1, :] * jnp.maximum(rel[:, h * QBLK:(h + 1) * QBLK], 0.0)
        sc = jnp.where(j * ts + krow <= qpos, sc, -jnp.inf)
        key_ref[pl.ds(pl.multiple_of(j * ts, ts), ts), :] = sc

    def score_pair(jj, _):
        score_step(2 * jj, rel0_ref, rel1_ref)
        score_step(2 * jj + 1, rel1_ref, rel0_ref)
        return 0

    rel0_ref[...] = rel_tile(0)
    lax.fori_loop(0, n_pairs, score_pair, 0)
    tau, tau_i = _topk_threshold(key_ref, n_pairs, 2 * ts, 0, QBLK, k_sel, idx_bits)

    q4 = []
    for c in range(N_KV_A):
        parts = []
        for g in range(GROUP_A):
            chunk = qa_ref[:, g * LANES:(g + 1) * LANES]
            parts.append(jnp.where(left, chunk, zero) if c == 0 else jnp.where(left, zero, chunk))
        q4.append(jnp.concatenate(parts, axis=0))

    cols4 = GROUP_A * QBLK
    kb_per_tile = ts // KBLK

    def s_tile(j):
        k = ka_ref[pl.ds(tile_off(j), ts), :]
        return [_dot_nt(k, q4[c]) for c in range(N_KV_A)]

    ones_rows = jnp.ones((SUM_ROWS, ts), BF16)

    def values(j):
        jb = jnp.clip(j, 0, last) * kb_per_tile
        v = jnp.concatenate([vat_ref[0, jb + q] for q in range(kb_per_tile)], axis=1)
        return jnp.concatenate([v, ones_rows], axis=0)

    def pv(vals, p_ref, c):
        return _dot(vals, p_ref[c])

    def attn_step(j, carry, s_cur, s_nxt, p_cur, p_prev):
        kt = key_ref[pl.ds(tile_off(j), ts), :]
        kpos = j * ts + krow
        sel = ((kt > tau) | ((kt == tau) & (kpos <= tau_i))) & (kpos <= qpos)
        bias = jnp.where(sel, 0.0, -jnp.inf)
        bias4 = jnp.concatenate([bias] * GROUP_A, axis=1)
        nxt = s_tile(j + 1)
        vals = values(j - 1)
        out = []
        for c in range(N_KV_A):
            m, acc = carry[c]
            s = s_cur[c] + bias4
            s_nxt[c] = nxt[c]
            m_new = jnp.maximum(m, jnp.max(_fold_rows(s, jnp.maximum), axis=0, keepdims=True))
            m_safe = jnp.where(m_new == -jnp.inf, 0.0, m_new)
            acc = (acc + pv(vals, p_prev, c)) * jnp.exp2(m - m_safe)
            p_cur[c] = jnp.exp2(s - m_safe).astype(BF16)
            out.append((m_new, acc))
        return tuple(out)

    def attn_pair(jj, carry):
        carry = attn_step(2 * jj, carry, s0_ref, s1_ref, p0_ref, p1_ref)
        return attn_step(2 * jj + 1, carry, s1_ref, s0_ref, p1_ref, p0_ref)

    first = s_tile(0)
    for c in range(N_KV_A):
        s0_ref[c] = first[c]
    p1_ref[...] = jnp.zeros_like(p1_ref)
    init = tuple((jnp.full((1, cols4), -jnp.inf, F32), jnp.zeros((LANES + SUM_ROWS, cols4), F32))
                 for _ in range(N_KV_A))
    res = lax.fori_loop(0, n_pairs, attn_pair, init)
    vals = values(2 * n_pairs - 1)
    o = []
    for c in range(N_KV_A):
        acc = res[c][1] + pv(vals, p1_ref, c)
        o.append(acc[0:LANES] / acc[LANES:LANES + 1])
    top = lax.broadcasted_iota(I32, (LANES, QBLK), 0) < HEAD_DIM
    for g in range(GROUP_A):
        cs = slice(g * QBLK, (g + 1) * QBLK)
        oa_ref[:, g * LANES:(g + 1) * LANES] = jnp.where(top, o[0][:, cs], o[1][:, cs]).T


def _dsa_prompt(qi, wit, qa, kib, kab, vat, batch, seq):
    QBLK = DSA_QBLK
    nq = seq // QBLK
    assert seq % (2 * DSA_TS) == 0, "key tiles are visited in pairs"
    k_sel = max(1, min(TOPK_MAX, seq // 4))
    idx_bits = max(1, int(seq - 1).bit_length())
    qspec = lambda n: pl.BlockSpec((QBLK, n), lambda b, i: (b * nq + i, 0))
    kspec = pl.BlockSpec((seq, LANES), lambda b, i: (b, 0))
    return pl.pallas_call(
        functools.partial(_dsa_prompt_kernel, k_sel, idx_bits),
        grid=(batch, nq),
        in_specs=[qspec(Q_I_W), pl.BlockSpec((1, N_IDX_HEADS, QBLK), lambda b, i: (b, 0, i)), qspec(Q_A_W),
                  kspec, kspec,
                  pl.BlockSpec((1, seq // KBLK, N_KV_A * HEAD_DIM, KBLK), lambda b, i: (b, 0, 0, 0))],
        out_specs=qspec(Q_A_W),
        out_shape=jax.ShapeDtypeStruct((batch * seq, Q_A_W), F32),
        scratch_shapes=[pltpu.VMEM((seq, QBLK), F32)]
                       + [pltpu.VMEM((DSA_TS, N_IDX_HEADS * QBLK), F32)] * 2
                       + [pltpu.VMEM((N_KV_A, DSA_TS, GROUP_A * QBLK), F32)] * 2
                       + [pltpu.VMEM((N_KV_A, DSA_TS, GROUP_A * QBLK), BF16)] * 2,
        compiler_params=_cparams("parallel", "arbitrary"),
        name="dsa_prompt",
    )(qi, wit, qa, kib, kab, vat)


def _later_matrix(n, key_axis):
    r = lax.broadcasted_iota(I32, (n, n), 0)
    c = lax.broadcasted_iota(I32, (n, n), 1)
    return jnp.where(r > c if key_axis == 1 else c > r, 1.0, 0.0).astype(BF16)


def _sb_tile(z, strict, later_m, carry, key_axis):
    soft = jnp.log(1.0 + jnp.exp(-jnp.abs(z)))
    log_beta = jnp.minimum(z, 0.0) - soft
    log_rest = log_beta - z
    if strict is not None:
        log_rest = jnp.where(strict, log_rest, 0.0)
    hi = log_rest.astype(BF16)
    lo = (log_rest - hi.astype(F32)).astype(BF16)
    if key_axis == 1:
        later = _dot(hi, later_m) + _dot(lo, later_m)
        first_l, first_r = later[:, 0:1], log_rest[:, 0:1]
    else:
        later = _dot(later_m, hi) + _dot(later_m, lo)
        first_l, first_r = later[0:1, :], log_rest[0:1, :]
    a = jnp.exp(log_beta + later + carry)
    if strict is not None:
        a = jnp.where(strict, a, 0.0)
    return a, carry + first_l + first_r


SB_TS = 128


def _sb_prompt_kernel(qb_ref, kb_ref, vbt_ref, ob_ref, acc_ref, carry_ref, z_ref):
    i = pl.program_id(1)
    ts = SB_TS
    n_pairs = N_HEADS_B // 2
    pcols = 2 * QBLK
    cols = n_pairs * pcols
    lane = lax.broadcasted_iota(I32, (QBLK, LANES), 1)
    left = lane < HEAD_DIM
    zero = jnp.zeros((QBLK, LANES), BF16)
    later_m = _later_matrix(ts, 0)
    krow = lax.broadcasted_iota(I32, (ts, cols), 0)
    qpos = i * QBLK + (lax.broadcasted_iota(I32, (ts, cols), 1) & (QBLK - 1))
    top = lax.broadcasted_iota(I32, (LANES, QBLK), 0) < HEAD_DIM
    pair = [slice(p * LANES, (p + 1) * LANES) for p in range(n_pairs)]

    q2 = []
    for p in range(n_pairs):
        chunk = qb_ref[:, pair[p]]
        q2.append(jnp.concatenate([jnp.where(left, chunk, zero), jnp.where(left, zero, chunk)], axis=0))
    acc_ref[...] = jnp.zeros_like(acc_ref)
    carry_ref[...] = jnp.zeros_like(carry_ref)

    def cond(state):
        j, go = state
        return jnp.logical_and(j >= 0, go)

    def logits(j):
        off = pl.multiple_of(jnp.maximum(j, 0) * ts, ts)
        return jnp.concatenate([_dot_nt(kb_ref[pl.ds(off, ts), pair[p]], q2[p]) for p in range(n_pairs)],
                               axis=1)

    def body(state):
        j, _ = state
        z = z_ref[...]
        z_next = logits(j - 1)
        strict = (j * ts + krow) < qpos
        a, new_carry = _sb_tile(z, strict, later_m, carry_ref[0:1, :], 0)
        ab = a.astype(BF16)
        for p in range(n_pairs):
            acc_ref[p] += _dot(vbt_ref[0, j, pair[p], :], ab[:, p * pcols:(p + 1) * pcols])
        carry_ref[...] = jnp.broadcast_to(new_carry, carry_ref.shape)
        z_ref[...] = z_next
        return j - 1, jnp.max(new_carry) >= SB_STOP

    z_ref[...] = logits(i)
    lax.while_loop(cond, body, (i, jnp.bool_(True)))
    for p in range(n_pairs):
        acc = acc_ref[p]
        ob_ref[:, pair[p]] = jnp.where(top, acc[:, 0:QBLK], acc[:, QBLK:pcols]).T


def _sb_prompt(qb, kbb, vbt, batch, seq):
    nq = seq // QBLK
    qspec = pl.BlockSpec((QBLK, Q_B_W), lambda b, i: (b * nq + i, 0))
    return pl.pallas_call(
        _sb_prompt_kernel,
        grid=(batch, nq),
        in_specs=[qspec, pl.BlockSpec((seq, Q_B_W), lambda b, i: (b, 0)),
                  pl.BlockSpec((1, seq // KBLK, Q_B_W, KBLK), lambda b, i: (b, 0, 0, 0))],
        out_specs=qspec,
        out_shape=jax.ShapeDtypeStruct((batch * seq, Q_B_W), F32),
        scratch_shapes=[pltpu.VMEM((N_HEADS_B // 2, LANES, 2 * QBLK), F32),
                        pltpu.VMEM((SUBLANES, N_HEADS_B * QBLK), F32),
                        pltpu.VMEM((SB_TS, N_HEADS_B * QBLK), F32)],
        compiler_params=_cparams("parallel", "arbitrary"),
        name="sb_prompt",
    )(qb, kbb, vbt)


TSLOTS = 8
SROWS = TSLOTS * N_HEADS_A


def _largest_divisor(n, cap):
    return max(x for x in range(1, cap + 1) if n % x == 0)


def _page_copies(pool_hbm, pt_ref, seq, buf, slot, sem, n_pages, start):
    def body(p, _):
        cp = pltpu.make_async_copy(pool_hbm.at[pt_ref[seq, p]], buf.at[slot, p], sem.at[slot])
        if start:
            cp.start()
        else:
            cp.wait()
        return 0

    lax.fori_loop(0, n_pages, body, 0)


def _page_ring(pool_hbm, pt_ref, buf, sem, n_pages):
    b = pl.program_id(0)
    slot = b & 1

    @pl.when(b == 0)
    def _():
        _page_copies(pool_hbm, pt_ref, 0, buf, 0, sem, n_pages, True)

    @pl.when(b + 1 < pl.num_programs(0))
    def _():
        _page_copies(pool_hbm, pt_ref, b + 1, buf, 1 - slot, sem, n_pages, True)

    _page_copies(pool_hbm, pt_ref, b, buf, slot, sem, n_pages, False)
    return slot


def _dsa_sample_scores_kernel(n_pages, page, ppt, pt_ref, qi_ref, w_ref, kin_ref, kidx_hbm,
                              key_ref, buf, sem):
    slot = _page_ring(kidx_hbm, pt_ref, buf, sem, n_pages)
    ts = ppt * page
    qi = qi_ref[0]
    w = w_ref[0]

    def scores(kt):
        wrel = w * jnp.maximum(_dot(qi, kt), 0.0)
        return jnp.sum(wrel.reshape(TSLOTS, N_IDX_HEADS, kt.shape[1]), axis=1)

    for t in range(n_pages // ppt):
        kt = jnp.concatenate([buf[slot, t * ppt + q] for q in range(ppt)], axis=1).astype(BF16)
        key_ref[0, :, t * ts:(t + 1) * ts] = scores(kt)
    past = n_pages * page
    qslot = lax.broadcasted_iota(I32, (TSLOTS, page), 0)
    u = lax.broadcasted_iota(I32, (TSLOTS, page), 1)
    key_ref[0, :, past:past + page] = jnp.where(u <= qslot, scores(kin_ref[0]), -jnp.inf)


def _dsa_sample_select_kernel(k_sel, idx_bits, n_tiles, ts, key_ref, tau_ref, taui_ref):
    tau, tau_i = _topk_threshold(key_ref, n_tiles, ts, 1, key_ref.shape[0], k_sel, idx_bits)
    tau_ref[...] = jnp.broadcast_to(tau, tau_ref.shape)
    taui_ref[...] = jnp.broadcast_to(tau_i, taui_ref.shape)


def _dsa_sample_attend_kernel(n_pages, page, ppt, pt_ref, qa_ref, key_ref, tau_ref, taui_ref, kvn_ref, kva_hbm,
                              o_ref, buf, sem, s_ref):
    slot = _page_ring(kva_hbm, pt_ref, buf, sem, n_pages)
    ts = ppt * page
    past = n_pages * page
    kfeat = N_KV_A * HEAD_DIM
    qa = qa_ref[0]
    tau = tau_ref[:, 0:1]
    tau_i = taui_ref[:, 0:1]

    n_t = n_pages // ppt

    def masked_scores(kt, kpos0, k):
        n = kt.shape[1]
        kpos = kpos0 + lax.broadcasted_iota(I32, (TSLOTS, n), 1)
        qpos = past + lax.broadcasted_iota(I32, (TSLOTS, n), 0)
        sel = ((kt > tau) | ((kt == tau) & (kpos <= tau_i))) & (kpos <= qpos)
        bias = jnp.where(sel, 0.0, -jnp.inf)
        bias_rows = jnp.concatenate(
            [jnp.broadcast_to(bias[t:t + 1, :], (N_HEADS_A, n)) for t in range(TSLOTS)], axis=0)
        return _dot(qa, k) + bias_rows

    def pages(t, f0):
        return jnp.concatenate([buf[slot, t * ppt + q, f0:f0 + kfeat, :] for q in range(ppt)],
                               axis=1).astype(BF16)

    mx = jnp.full((SROWS, ts), -jnp.inf, F32)
    for t in range(n_t):
        s = masked_scores(key_ref[0, :, t * ts:(t + 1) * ts], t * ts, pages(t, 0))
        s_ref[:, t * ts:(t + 1) * ts] = s
        mx = jnp.maximum(mx, s)
    s_new = masked_scores(key_ref[0, :, past:past + page], past, kvn_ref[0, 0:kfeat, :])
    m = jnp.maximum(jnp.max(mx, axis=1, keepdims=True), jnp.max(s_new, axis=1, keepdims=True))
    m = jnp.where(m == -jnp.inf, 0.0, m)

    p_new = jnp.exp2(s_new - m)
    acc = _dot_nt(p_new.astype(BF16), kvn_ref[0, kfeat:2 * kfeat, :])
    psum = jnp.zeros((SROWS, ts), F32)
    for t in range(n_t):
        p = jnp.exp2(s_ref[:, t * ts:(t + 1) * ts] - m)
        psum = psum + p
        acc = acc + _dot_nt(p.astype(BF16), pages(t, kfeat))
    l = jnp.sum(psum, axis=1, keepdims=True) + jnp.sum(p_new, axis=1, keepdims=True)
    o_ref[0] = acc / jnp.where(l > 0.0, l, 1.0)


def _dsa_sample(qi_st, w_st, qa_st, kidx_pool, kva_pool, kin_t, kvn_t, page_table, dec_seq):
    n_seq, n_pages = page_table.shape
    page = kidx_pool.shape[2]
    nk = (n_pages + 1) * page
    ppt = _largest_divisor(n_pages, 8)
    sel_ts = _largest_divisor(n_pages + 1, 8) * page
    k_sel = max(1, min(TOPK_MAX, (n_pages * page + dec_seq) // 4))
    idx_bits = max(1, int(nk - 1).bit_length())
    seq_spec = lambda shp: pl.BlockSpec((1,) + shp, lambda b, pt: (b, 0, 0))
    hbm_spec = pl.BlockSpec(memory_space=pl.ANY)
    ring = lambda feat: [pltpu.VMEM((2, n_pages, feat, page), F32), pltpu.SemaphoreType.DMA((2,))]

    keys = pl.pallas_call(
        functools.partial(_dsa_sample_scores_kernel, n_pages, page, ppt),
        grid_spec=pltpu.PrefetchScalarGridSpec(
            num_scalar_prefetch=1, grid=(n_seq,),
            in_specs=[seq_spec((SROWS, IDX_DIM)), seq_spec((SROWS, 1)), seq_spec((IDX_DIM, page)), hbm_spec],
            out_specs=seq_spec((TSLOTS, nk)),
            scratch_shapes=ring(IDX_DIM)),
        out_shape=jax.ShapeDtypeStruct((n_seq, TSLOTS, nk), F32),
        compiler_params=_cparams("arbitrary"),
        name="dsa_sample_scores",
    )(page_table, qi_st, w_st, kin_t, kidx_pool)

    rows = n_seq * dec_seq
    tau, tau_i = pl.pallas_call(
        functools.partial(_dsa_sample_select_kernel, k_sel, idx_bits, nk // sel_ts, sel_ts),
        grid=(1,),
        in_specs=[_const_spec((rows, nk))],
        out_specs=[pl.BlockSpec((rows, LANES), lambda i: (0, 0))] * 2,
        out_shape=[jax.ShapeDtypeStruct((rows, LANES), F32), jax.ShapeDtypeStruct((rows, LANES), I32)],
        compiler_params=_cparams("arbitrary"),
        name="dsa_sample_select",
    )(keys[:, :dec_seq].reshape(rows, nk))
    pad_slots = lambda a: jnp.pad(a.reshape(n_seq, dec_seq, LANES),
                                  ((0, 0), (0, TSLOTS - dec_seq), (0, 0))).reshape(n_seq * TSLOTS, LANES)
    tau, tau_i = pad_slots(tau), pad_slots(tau_i)

    row_spec = pl.BlockSpec((TSLOTS, LANES), lambda b, pt: (b, 0))
    return pl.pallas_call(
        functools.partial(_dsa_sample_attend_kernel, n_pages, page, ppt),
        grid_spec=pltpu.PrefetchScalarGridSpec(
            num_scalar_prefetch=1, grid=(n_seq,),
            in_specs=[seq_spec((SROWS, LANES)), seq_spec((TSLOTS, nk)), row_spec, row_spec,
                      seq_spec((KV_A_W, page)), hbm_spec],
            out_specs=seq_spec((SROWS, LANES)),
            scratch_shapes=ring(KV_A_W) + [pltpu.VMEM((SROWS, n_pages * page), F32)]),
        out_shape=jax.ShapeDtypeStruct((n_seq, SROWS, LANES), F32),
        compiler_params=_cparams("arbitrary"),
        name="dsa_sample_attend",
    )(page_table, qa_st, keys, tau, tau_i, kvn_t, kva_pool)


def _sb_sample_kernel(n_pages, page, pt_ref, q_ref, kvn_ref, kvb_hbm, o_ref, buf, sem, acc_ref, carry_ref):
    b = pl.program_id(0)
    q = q_ref[0]
    later_m = _later_matrix(page, 1)
    last = n_pages - 1

    def page_copy(pg):
        slot = (last - pg) & 1
        return pltpu.make_async_copy(kvb_hbm.at[pt_ref[b, pg]], buf.at[slot], sem.at[slot])

    def tile(kt, vt, strict):
        a, new_carry = _sb_tile(_dot(q, kt), strict, later_m, carry_ref[:, 0:1], 1)
        acc_ref[...] += _dot_nt(a.astype(BF16), vt)
        carry_ref[...] = jnp.broadcast_to(new_carry, carry_ref.shape)
        return jnp.max(new_carry) >= SB_STOP

    page_copy(last).start()
    acc_ref[...] = jnp.zeros_like(acc_ref)
    carry_ref[...] = jnp.zeros_like(carry_ref)
    qslot = lax.broadcasted_iota(I32, (SROWS, page), 0) // N_HEADS_B
    u = lax.broadcasted_iota(I32, (SROWS, page), 1)
    go0 = tile(kvn_ref[0, 0:Q_B_W, :], kvn_ref[0, Q_B_W:2 * Q_B_W, :], u < qslot)

    def cond(state):
        pg, go = state
        return jnp.logical_and(pg >= 0, go)

    def body(state):
        pg, _ = state
        page_copy(pg).wait()

        @pl.when(pg > 0)
        def _():
            page_copy(pg - 1).start()

        slot = (last - pg) & 1
        go = tile(buf[slot, 0:Q_B_W, :].astype(BF16), buf[slot, Q_B_W:2 * Q_B_W, :].astype(BF16), None)
        return pg - 1, go

    pg_end, _ = lax.while_loop(cond, body, (jnp.int32(last), go0))

    @pl.when(pg_end >= 0)
    def _():
        page_copy(pg_end).wait()

    r = lax.broadcasted_iota(I32, (SROWS, Q_B_W), 0)
    c = lax.broadcasted_iota(I32, (SROWS, Q_B_W), 1)
    own = (r % N_HEADS_B) == (c // HEAD_DIM)
    diag = jnp.where(own, acc_ref[...], 0.0)
    o_ref[0] = jnp.sum(diag.reshape(TSLOTS, N_HEADS_B, Q_B_W), axis=1)


def _sb_sample(qb_bd, kvb_pool, kvn_t, page_table):
    n_seq, n_pages = page_table.shape
    page = kvb_pool.shape[2]
    seq_spec = lambda shp: pl.BlockSpec((1,) + shp, lambda b, pt: (b, 0, 0))
    return pl.pallas_call(
        functools.partial(_sb_sample_kernel, n_pages, page),
        grid_spec=pltpu.PrefetchScalarGridSpec(
            num_scalar_prefetch=1, grid=(n_seq,),
            in_specs=[seq_spec((SROWS, Q_B_W)), seq_spec((KV_B_W, page)), pl.BlockSpec(memory_space=pl.ANY)],
            out_specs=seq_spec((TSLOTS, Q_B_W)),
            scratch_shapes=[pltpu.VMEM((2, KV_B_W, page), F32), pltpu.SemaphoreType.DMA((2,)),
                            pltpu.VMEM((SROWS, Q_B_W), F32), pltpu.VMEM((SROWS, LANES), F32)]),
        out_shape=jax.ShapeDtypeStruct((n_seq, TSLOTS, Q_B_W), F32),
        compiler_params=_cparams("arbitrary"),
        name="sb_sample",
    )(page_table, qb_bd, kvn_t, kvb_pool)


def _out_kernel(oa_ref, ob_ref, gate_ref, x_ref, wa_ref, wb_ref, wo_ref, nf_ref,
                wg_ref, wu_ref, wd_ref, nfin_ref, y_ref):
    ba = _dot(oa_ref[...].astype(BF16), wa_ref[...])
    bb = _dot(ob_ref[...].astype(BF16), wb_ref[...])
    merged = (gate_ref[:, 0:D_MODEL].astype(F32) * ba
              + gate_ref[:, D_MODEL:2 * D_MODEL].astype(F32) * bb)
    h1 = x_ref[...] + _dot(merged.astype(BF16), wo_ref[...])
    ms = jnp.mean(h1 * h1, axis=-1, keepdims=True)
    hn = ((h1 * lax.rsqrt(ms + RMS_EPS)) * nf_ref[...]).astype(BF16)
    acc = jnp.zeros_like(h1)
    for c in range(D_FF // FFN_CHUNK):
        sl = slice(c * FFN_CHUNK, (c + 1) * FFN_CHUNK)
        gg = _dot(hn, wg_ref[:, sl])
        uu = _dot(hn, wu_ref[:, sl])
        act = (gg * (1.0 / (1.0 + jnp.exp(-gg)))) * uu
        acc = acc + _dot(act.astype(BF16), wd_ref[sl, :])
    h2 = h1 + acc
    ms2 = jnp.mean(h2 * h2, axis=-1, keepdims=True)
    y_ref[...] = (h2 * lax.rsqrt(ms2 + RMS_EPS)) * nfin_ref[...]


def _out(oa, ob, gates, x2d, wts, tm):
    rows = x2d.shape[0]
    row_spec = lambda n: pl.BlockSpec((tm, n), lambda i: (i, 0))
    return pl.pallas_call(
        _out_kernel,
        grid=(rows // tm,),
        in_specs=[row_spec(Q_A_W), row_spec(Q_B_W), row_spec(2 * D_MODEL), row_spec(D_MODEL)]
                 + [_const_spec(w.shape) for w in wts],
        out_specs=row_spec(D_MODEL),
        out_shape=jax.ShapeDtypeStruct((rows, D_MODEL), F32),
        compiler_params=_cparams("parallel"),
        name="out",
    )(oa, ob, gates, x2d, *wts)


def _rope_tables(pos):
    inv_freq = ROPE_THETA ** (-jnp.arange(HALF, dtype=F32) / HALF)
    ang = pos.astype(F32)[:, None] * inv_freq[None, :]
    cos, sin = jnp.cos(ang), jnp.sin(ang)
    cs = jnp.concatenate([cos, cos, cos, cos], axis=1)
    sn = jnp.concatenate([-sin, sin, -sin, sin], axis=1)
    return cs, sn, cos.T, sin.T


def _pack_w_in(w):
    d = w.shape[0]
    o = np.cumsum([0, Q_A_W, KV_A_W, Q_I_W, IDX_DIM, N_IDX_HEADS, Q_B_W, KV_B_W, 2 * D_MODEL])
    kw = N_KV_A * HEAD_DIM
    qa = w[:, o[0]:o[1]].reshape(d, N_HEADS_A, HEAD_DIM)[:, np.array(QA_HEAD_ORDER), :].reshape(d, Q_A_W)
    ka, va = w[:, o[1]:o[1] + kw], w[:, o[1] + kw:o[2]]
    qi, ki, wi, qb = w[:, o[2]:o[3]], w[:, o[3]:o[4]], w[:, o[4]:o[5]], w[:, o[5]:o[6]]
    kb, vb = w[:, o[6]:o[6] + Q_B_W], w[:, o[6] + Q_B_W:o[7]]
    g = w[:, o[7]:o[8]]
    w_tok = jnp.concatenate([qa, qi, qb, g], axis=1).astype(BF16)
    wi_pad = jnp.pad(wi, ((0, 0), (0, HEAD_DIM - N_IDX_HEADS)))
    w_feat = jnp.concatenate([ka, va, ki, wi_pad, kb, vb], axis=1).T.astype(BF16)
    return w_tok, w_feat


def kernel(x_prompt, x_sample, cache_kv_a, cache_k_idx, cache_kv_b, page_table, w_in, w_br_a, w_br_b,
           w_o, norm_attn, norm_ffn, w_ffn_gate, w_ffn_up, w_ffn_down, norm_final):
    depth = w_in.shape[0]
    assert depth == 1, "single-layer step"
    batch, seq, d = x_prompt.shape
    assert d == D_MODEL and seq % ROW_TILE == 0
    layer = 0

    w_tok, w_feat = _pack_w_in(w_in[layer])
    wa = w_br_a[layer].reshape(N_HEADS_A, HEAD_DIM, D_MODEL)[np.array(QA_HEAD_ORDER)].reshape(Q_A_W, D_MODEL)
    out_wts = (wa.astype(BF16), w_br_b[layer].astype(BF16), w_o[layer].astype(BF16),
               norm_ffn[layer].reshape(1, D_MODEL),
               w_ffn_gate[layer].astype(BF16), w_ffn_up[layer].astype(BF16), w_ffn_down[layer].astype(BF16),
               norm_final.reshape(1, D_MODEL))
    gain = norm_attn[layer].reshape(1, D_MODEL)

    xp = x_prompt.reshape(batch * seq, D_MODEL)
    (kva_p, kidx_p, kvb_p, qa, qi, wit, qb, gates, kab, kib, kbb, vat, vbt) = _project(
        xp, gain, w_tok, w_feat, *_rope_tables(jnp.arange(seq, dtype=I32)), batch, ROW_TILE)
    oa = _dsa_prompt(qi, wit, qa, kib, kab, vat, batch, seq)
    ob = _sb_prompt(qb, kbb, vbt, batch, seq)
    y_prompt = _out(oa, ob, gates, xp, out_wts, ROW_TILE).reshape(batch, seq, D_MODEL)

    new_kv_a_p = kva_p.reshape(depth, batch, 2, N_KV_A, HEAD_DIM, seq).transpose(0, 1, 5, 2, 3, 4)
    new_k_idx_p = kidx_p.reshape(depth, batch, IDX_DIM, seq).transpose(0, 1, 3, 2)
    new_kv_b_p = kvb_p.reshape(depth, batch, 2, N_HEADS_B, HEAD_DIM, seq).transpose(0, 1, 5, 2, 3, 4)

    n_seq, dec_seq, _ = x_sample.shape
    n_pages = page_table.shape[1]
    n_pool, page = cache_kv_a.shape[1], cache_kv_a.shape[2]
    past = n_pages * page
    rows_s = n_seq * dec_seq
    assert dec_seq <= TSLOTS and rows_s % KBLK == 0 and page == LANES
    xs = x_sample.reshape(rows_s, D_MODEL)
    cs_s, sn_s, csf_s, snf_s = _rope_tables(past + jnp.arange(dec_seq, dtype=I32))
    tabs_s = (jnp.tile(cs_s, (n_seq, 1)), jnp.tile(sn_s, (n_seq, 1)),
              jnp.tile(csf_s, (1, n_seq)), jnp.tile(snf_s, (1, n_seq)))
    (kva_s, kidx_s, kvb_s, qa_s, qi_s, wit_s, qb_s, gates_s, _, _, _, _, _) = _project(
        xs, gain, w_tok, w_feat, *tabs_s, 1, rows_s)

    def slots(a):
        return jnp.pad(a, ((0, 0), (0, TSLOTS - dec_seq)) + ((0, 0),) * (a.ndim - 2))

    def new_tile(a):
        t = a[0].reshape(a.shape[1], n_seq, dec_seq).transpose(1, 0, 2)
        return jnp.pad(t, ((0, 0), (0, 0), (0, page - dec_seq))).astype(BF16)

    qi_st = slots(qi_s.reshape(n_seq, dec_seq, N_IDX_HEADS, IDX_DIM)).reshape(n_seq, SROWS, IDX_DIM)
    w_st = slots(wit_s[0].T.reshape(n_seq, dec_seq, N_IDX_HEADS)).reshape(n_seq, SROWS, 1)
    qa5 = qa_s.reshape(n_seq, dec_seq, GROUP_A, N_KV_A, HEAD_DIM).transpose(0, 1, 3, 2, 4)
    eye2 = jnp.eye(N_KV_A, dtype=qa_s.dtype)
    qa_st = qa5[:, :, :, :, None, :] * eye2[None, None, :, None, :, None]
    qa_st = slots(qa_st.reshape(n_seq, dec_seq, N_HEADS_A, LANES)).reshape(n_seq, SROWS, LANES)
    kidx_pool = cache_k_idx[layer].transpose(0, 2, 1)
    kva_pool = cache_kv_a[layer].transpose(0, 2, 3, 4, 1).reshape(n_pool, KV_A_W, page)
    kvb_pool = cache_kv_b[layer].transpose(0, 2, 3, 4, 1).reshape(n_pool, KV_B_W, page)
    o_st = _dsa_sample(qi_st, w_st, qa_st, kidx_pool, kva_pool, new_tile(kidx_s), new_tile(kva_s),
                       page_table, dec_seq)
    o6 = o_st.reshape(n_seq, TSLOTS, N_KV_A, GROUP_A, N_KV_A, HEAD_DIM)[:, :dec_seq]
    oa_s = jnp.stack([o6[:, :, c, :, c, :] for c in range(N_KV_A)], axis=2).reshape(rows_s, Q_A_W)

    qb4 = qb_s.reshape(n_seq, dec_seq, N_HEADS_B, HEAD_DIM)
    eye8 = jnp.eye(N_HEADS_B, dtype=qb_s.dtype)
    qb_bd = qb4[:, :, :, None, :] * eye8[None, None, :, :, None]
    qb_bd = slots(qb_bd.reshape(n_seq, dec_seq, N_HEADS_B, Q_B_W)).reshape(n_seq, SROWS, Q_B_W)
    ob_st = _sb_sample(qb_bd, kvb_pool, new_tile(kvb_s), page_table)
    ob_s = ob_st[:, :dec_seq].reshape(rows_s, Q_B_W)

    out_wts_s = (w_br_a[layer].astype(BF16),) + out_wts[1:]
    y_sample = _out(oa_s, ob_s, gates_s, xs, out_wts_s, rows_s).reshape(n_seq, dec_seq, D_MODEL)

    return (y_prompt, y_sample, new_kv_a_p, new_k_idx_p, new_kv_b_p,
            kva_s[0].T.reshape(depth, n_seq, dec_seq, 2, N_KV_A, HEAD_DIM),
            kidx_s[0].T.reshape(depth, n_seq, dec_seq, IDX_DIM),
            kvb_s[0].T.reshape(depth, n_seq, dec_seq, 2, N_HEADS_B, HEAD_DIM))
```

```python
import functools

import jax
import jax.numpy as jnp
import numpy as np
from jax import lax
from jax.experimental import pallas as pl
from jax.experimental.pallas import tpu as pltpu

F32 = jnp.float32
BF16 = jnp.bfloat16
I32 = jnp.int32

D_MODEL = 1024
HEAD_DIM = 64
N_HEADS_A = 8
N_KV_A = 2
GROUP_A = N_HEADS_A // N_KV_A
N_IDX_HEADS = 8
IDX_DIM = 64
TOPK_MAX = 256
N_HEADS_B = 8
Q_A_W = N_HEADS_A * HEAD_DIM
KV_A_W = 2 * N_KV_A * HEAD_DIM
Q_I_W = N_IDX_HEADS * IDX_DIM
Q_B_W = N_HEADS_B * HEAD_DIM
KV_B_W = 2 * N_HEADS_B * HEAD_DIM
D_FF = -(-8 * D_MODEL // (3 * 256)) * 256
ROPE_THETA = 10000.0
RMS_EPS = 1e-6
QK_SCALE = HEAD_DIM ** -0.5
IDX_SCALE = IDX_DIM ** -0.5
HALF = HEAD_DIM // 2
LOG2_E = 1.4426950408889634

LANES = 128
SUBLANES = 8
QBLK = 256
KBLK = 128
SUM_ROWS = 16
FFN_CHUNK = 256
ROW_TILE = 512

TK_QA = 0
TK_QI = 512
TK_QB = 1024
TK_G = 1536
TK_W = 3584
FM_KA = 0
FM_VA = 128
FM_KI = 256
FM_WI = 320
FM_KB = 384
FM_VB = 896
FM_W = 1408
QA_HEAD_ORDER = (0, 4, 1, 5, 2, 6, 3, 7)

NEGINF_KEY = -2139095041
POSINF_KEY = 0x7F800000
MIN_NORMAL_KEY = 0x00800000
IDX_ALL = 1 << 30
SB_STOP = -105.0

VMEM_LIMIT = 60 * 1024 * 1024

NT_DIMS = (((1,), (1,)), ((), ()))


def _cparams(*sem):
    return pltpu.CompilerParams(dimension_semantics=sem, vmem_limit_bytes=VMEM_LIMIT)


def _const_spec(shape):
    nd = len(shape)
    return pl.BlockSpec(shape, lambda *_: (0,) * nd, pipeline_mode=pl.Buffered(1))


def _dot_nt(a, b):
    return lax.dot_general(a, b, NT_DIMS, preferred_element_type=F32)


def _dot(a, b):
    return jnp.dot(a, b, preferred_element_type=F32)


def _project_kernel(x_ref, g_ref, wt_ref, wf_ref, cs_ref, sn_ref, csf_ref, snf_ref,
                    kva_ref, kidx_ref, kvb_ref,
                    qa_ref, qi_ref, wit_ref, qb_ref, gate_ref,
                    kab_ref, kib_ref, kbb_ref, vat_ref, vbt_ref):
    x = x_ref[...]
    ms = jnp.mean(x * x, axis=-1, keepdims=True)
    xn = (x * lax.rsqrt(ms + RMS_EPS)) * g_ref[...]
    xb = xn.astype(BF16)
    tm = x.shape[0]
    cs = cs_ref[...]
    sn = sn_ref[...]
    lane = lax.broadcasted_iota(I32, (tm, LANES), 1)
    first_half = (lane & HALF) == 0

    def mm(c0, n):
        return _dot(xb, wt_ref[:, c0:c0 + n])

    def rope(y):
        partner = jnp.where(first_half, pltpu.roll(y, LANES - HALF, 1), pltpu.roll(y, HALF, 1))
        return y * cs + partner * sn

    ya = mm(TK_QA, Q_A_W)
    for c in range(Q_A_W // LANES):
        sl = slice(c * LANES, (c + 1) * LANES)
        qa_ref[:, sl] = (rope(ya[:, sl]) * (QK_SCALE * LOG2_E)).astype(BF16)
    yi = mm(TK_QI, Q_I_W)
    for c in range(Q_I_W // LANES):
        sl = slice(c * LANES, (c + 1) * LANES)
        qi_ref[:, sl] = rope(yi[:, sl]).astype(BF16)
    qb_ref[...] = (mm(TK_QB, Q_B_W) * QK_SCALE).astype(BF16)
    for c in range(2 * D_MODEL // 512):
        sl = slice(c * 512, (c + 1) * 512)
        y = mm(TK_G + c * 512, 512)
        gate_ref[:, sl] = (1.0 / (1.0 + jnp.exp(-y))).astype(BF16)

    csf = csf_ref[...]
    snf = snf_ref[...]

    def fm(r0, n):
        return _dot_nt(wf_ref[r0:r0 + n, :], xb)

    def rope_fm(y):
        lo, hi = y[0:HALF], y[HALF:HEAD_DIM]
        return jnp.concatenate([lo * csf - hi * snf, hi * csf + lo * snf], axis=0)

    ka = fm(FM_KA, N_KV_A * HEAD_DIM)
    ka = jnp.concatenate([rope_fm(ka[c * HEAD_DIM:(c + 1) * HEAD_DIM]) for c in range(N_KV_A)], axis=0)
    kva_ref[0, 0:N_KV_A * HEAD_DIM, :] = ka
    kab_ref[...] = ka.T.astype(BF16)
    va = fm(FM_VA, N_KV_A * HEAD_DIM)
    kva_ref[0, N_KV_A * HEAD_DIM:2 * N_KV_A * HEAD_DIM, :] = va
    ki = rope_fm(fm(FM_KI, IDX_DIM))
    kidx_ref[0] = ki
    kib_ref[...] = jnp.concatenate([ki, ki], axis=0).T.astype(BF16)
    wi = fm(FM_WI, HEAD_DIM)
    wit_ref[0] = (wi[0:N_IDX_HEADS] * (N_IDX_HEADS ** -0.5)) * IDX_SCALE
    kb = fm(FM_KB, Q_B_W)
    kvb_ref[0, 0:Q_B_W, :] = kb
    kbb_ref[...] = kb.T.astype(BF16)
    vb = fm(FM_VB, Q_B_W)
    kvb_ref[0, Q_B_W:2 * Q_B_W, :] = vb
    for q in range(tm // KBLK):
        sl = slice(q * KBLK, (q + 1) * KBLK)
        vat_ref[0, q] = va[:, sl].astype(BF16)
        vbt_ref[0, q] = vb[:, sl].astype(BF16)


def _project(x2d, gain, w_tok, w_feat, cs, sn, csf, snf, groups, tm):
    rows = x2d.shape[0]
    t = rows // groups
    nb = t // tm
    nkb = tm // KBLK
    row_spec = lambda n: pl.BlockSpec((tm, n), lambda i: (i, 0))
    tab_spec = pl.BlockSpec((tm, LANES), lambda i: (i % nb, 0))
    tabf_spec = pl.BlockSpec((HALF, tm), lambda i: (0, i % nb))
    fm_spec = lambda n: pl.BlockSpec((1, n, tm), lambda i: (i // nb, 0, i % nb))
    blk_spec = lambda n: pl.BlockSpec((1, nkb, n, KBLK), lambda i: (i // nb, i % nb, 0, 0))
    out_specs = [fm_spec(KV_A_W), fm_spec(IDX_DIM), fm_spec(KV_B_W),
                 row_spec(Q_A_W), row_spec(Q_I_W), fm_spec(N_IDX_HEADS), row_spec(Q_B_W), row_spec(2 * D_MODEL),
                 row_spec(LANES), row_spec(LANES), row_spec(Q_B_W),
                 blk_spec(N_KV_A * HEAD_DIM), blk_spec(Q_B_W)]
    fm_shape = lambda n, dt: jax.ShapeDtypeStruct((groups, n, t), dt)
    tk_shape = lambda n, dt: jax.ShapeDtypeStruct((rows, n), dt)
    blk_shape = lambda n: jax.ShapeDtypeStruct((groups, t // KBLK, n, KBLK), BF16)
    out_shape = [fm_shape(KV_A_W, F32), fm_shape(IDX_DIM, F32), fm_shape(KV_B_W, F32),
                 tk_shape(Q_A_W, BF16), tk_shape(Q_I_W, BF16), fm_shape(N_IDX_HEADS, F32),
                 tk_shape(Q_B_W, BF16), tk_shape(2 * D_MODEL, BF16),
                 tk_shape(LANES, BF16), tk_shape(LANES, BF16), tk_shape(Q_B_W, BF16),
                 blk_shape(N_KV_A * HEAD_DIM), blk_shape(Q_B_W)]
    return pl.pallas_call(
        _project_kernel,
        grid=(rows // tm,),
        in_specs=[row_spec(D_MODEL), _const_spec((1, D_MODEL)), _const_spec((D_MODEL, TK_W)),
                  _const_spec((FM_W, D_MODEL)), tab_spec, tab_spec, tabf_spec, tabf_spec],
        out_specs=out_specs,
        out_shape=out_shape,
        compiler_params=_cparams("parallel"),
        name="project",
    )(x2d, gain, w_tok, w_feat, cs, sn, csf, snf)


def _key_to_float(k):
    return lax.bitcast_convert_type(k ^ ((k >> 31) & jnp.int32(0x7FFFFFFF)), F32)


def _fold_rows(x, op, rows=SUBLANES):
    n = x.shape[0]
    assert n % rows == 0 and (n // rows) & (n // rows - 1) == 0
    while n > rows:
        n //= 2
        x = op(x[:n], x[n:])
    return x


def _count(key_ref, n_tiles, ts, key_axis, width, pred):
    if key_axis == 1:
        def body(j, acc):
            off = pl.multiple_of(j * ts, ts)
            kt = key_ref[:, pl.ds(off, ts)]
            kpos = j * ts + lax.broadcasted_iota(I32, (width, ts), 1)
            m = jnp.where(pred(kt, kpos, lambda v: v), 1.0, 0.0)
            s = m[:, 0:LANES]
            for c in range(1, ts // LANES):
                s = s + m[:, c * LANES:(c + 1) * LANES]
            return acc + s

        acc = lax.fori_loop(0, n_tiles, body, jnp.zeros((width, LANES), F32))
        return jnp.sum(acc, axis=1, keepdims=True)

    def body(j, acc):
        off = pl.multiple_of(j * ts, ts)
        kpos = j * ts + lax.broadcasted_iota(I32, (ts, LANES), 0)
        parts = []
        for g in range(width // LANES):
            gs = slice(g * LANES, (g + 1) * LANES)
            hit = pred(key_ref[pl.ds(off, ts), gs], kpos, lambda v, gs=gs: v[:, gs])
            parts.append(_fold_rows(jnp.where(hit, 1.0, 0.0), jnp.add))
        return acc + jnp.concatenate(parts, axis=1)

    acc = lax.fori_loop(0, n_tiles, body, jnp.zeros((SUBLANES, width), F32))
    return jnp.sum(acc, axis=0, keepdims=True)


def _topk_threshold(key_ref, n_tiles, ts, key_axis, width, k_sel, idx_bits):
    k_f = float(k_sel)
    vec = (width, 1) if key_axis == 1 else (1, width)
    count = functools.partial(_count, key_ref, n_tiles, ts, key_axis, width)

    def as_float(k):
        zero_band = (k >= -MIN_NORMAL_KEY) & (k < MIN_NORMAL_KEY)
        return _key_to_float(jnp.where(zero_band, 0, k))

    def bis(b, r):
        cand = r + lax.shift_left(jnp.int32(1), jnp.int32(31) - b)
        valid = (cand >= NEGINF_KEY) & (cand <= POSINF_KEY)
        cand_f = as_float(jnp.where(valid, cand, 0))
        return jnp.where(valid & (count(lambda st, kp, q: st >= q(cand_f)) >= k_f), cand, r)

    tau = as_float(lax.fori_loop(0, 32, bis, jnp.full(vec, NEGINF_KEY, I32)))
    need = k_f - count(lambda st, kp, q: st > q(tau))
    tie = (count(lambda st, kp, q: st == q(tau)) > need) & (tau > -jnp.inf)

    def resolve():
        def ib(b, x):
            c = x + lax.shift_left(jnp.int32(1), jnp.int32(idx_bits - 1) - b)
            cnt = count(lambda st, kp, q: (st == q(tau)) & (kp < q(c)))
            return jnp.where(cnt < need, c, x)

        x = lax.fori_loop(0, idx_bits, ib, jnp.zeros(vec, I32))
        return jnp.where(tie, x, IDX_ALL)

    any_tie = jnp.max(jnp.where(tie, 1.0, 0.0)) > 0.5
    tau_i = lax.cond(any_tie, resolve, lambda: jnp.full(vec, IDX_ALL, I32))
    return tau, tau_i


DSA_TS = 256


def _dsa_prompt_kernel(k_sel, idx_bits, qi_ref, wit_ref, qa_ref, ki_ref, ka_ref, vat_ref,
                       oa_ref, key_ref, rel0_ref, rel1_ref, s0_ref, s1_ref, p0_ref, p1_ref):
    i = pl.program_id(1)
    ts = DSA_TS
    n_tiles = (i * QBLK + QBLK + ts - 1) // ts
    lane = lax.broadcasted_iota(I32, (QBLK, LANES), 1)
    left = lane < HEAD_DIM
    krow = lax.broadcasted_iota(I32, (ts, QBLK), 0)
    qpos = i * QBLK + lax.broadcasted_iota(I32, (ts, QBLK), 1)

    zero = jnp.zeros((QBLK, LANES), BF16)
    qi_parts = []
    for p in range(Q_I_W // LANES):
        chunk = qi_ref[:, p * LANES:(p + 1) * LANES]
        qi_parts.append(jnp.where(left, chunk, zero))
        qi_parts.append(jnp.where(left, zero, chunk))
    qi_all = jnp.concatenate(qi_parts, axis=0)
    wt = wit_ref[0]

    last = n_tiles - 1
    n_pairs = (n_tiles + 1) // 2

    def tile_off(j):
        return pl.multiple_of(jnp.minimum(j, last) * ts, ts)

    def rel_tile(j):
        return _dot_nt(ki_ref[pl.ds(tile_off(j), ts), :], qi_all)

    def score_step(j, cur_ref, nxt_ref):
        rel = cur_ref[...]
        nxt_ref[...] = rel_tile(j + 1)
        sc = jnp.zeros((ts, QBLK), F32)
        for h in range(N_IDX_HEADS):
            sc = sc + wt[h:h + 1, :] * jnp.maximum(rel[:, h * QBLK:(h + 1) * QBLK], 0.0)
        sc = jnp.where(j * ts + krow <= qpos, sc, -jnp.inf)
        key_ref[pl.ds(pl.multiple_of(j * ts, ts), ts), :] = sc

    def score_pair(jj, _):
        score_step(2 * jj, rel0_ref, rel1_ref)
        score_step(2 * jj + 1, rel1_ref, rel0_ref)
        return 0

    q4 = []
    for c in range(N_KV_A):
        parts = []
        for g in range(GROUP_A):
            chunk = qa_ref[:, g * LANES:(g + 1) * LANES]
            parts.append(jnp.where(left, chunk, zero) if c == 0 else jnp.where(left, zero, chunk))
        q4.append(jnp.concatenate(parts, axis=0))

    cols4 = GROUP_A * QBLK
    kb_per_tile = ts // KBLK

    def s_tile(j):
        k = ka_ref[pl.ds(tile_off(j), ts), :]
        return [_dot_nt(k, q4[c]) for c in range(N_KV_A)]

    rel0_ref[...] = rel_tile(0)
    first = s_tile(0)
    for c in range(N_KV_A):
        s0_ref[c] = first[c]
    p1_ref[...] = jnp.zeros_like(p1_ref)
    lax.fori_loop(0, n_pairs, score_pair, 0)
    tau, tau_i = _topk_threshold(key_ref, n_pairs, 2 * ts, 0, QBLK, k_sel, idx_bits)

    ones_rows = jnp.ones((SUM_ROWS, ts), BF16)

    def values(j):
        jb = jnp.clip(j, 0, last) * kb_per_tile
        v = jnp.concatenate([vat_ref[0, jb + q] for q in range(kb_per_tile)], axis=1)
        return jnp.concatenate([v, ones_rows], axis=0)

    def pv(vals, p_ref, c):
        return _dot(vals, p_ref[c])

    def attn_step(j, carry, s_cur, s_nxt, p_cur, p_prev):
        kt = key_ref[pl.ds(tile_off(j), ts), :]
        kpos = j * ts + krow
        sel = ((kt > tau) | ((kt == tau) & (kpos <= tau_i))) & (kpos <= qpos)
        bias = jnp.where(sel, 0.0, -jnp.inf)
        bias4 = jnp.concatenate([bias] * GROUP_A, axis=1)
        nxt = s_tile(j + 1)
        vals = values(j - 1)
        out = []
        for c in range(N_KV_A):
            m, acc = carry[c]
            s = s_cur[c] + bias4
            s_nxt[c] = nxt[c]
            m_new = jnp.maximum(m, jnp.max(_fold_rows(s, jnp.maximum), axis=0, keepdims=True))
            m_safe = jnp.where(m_new == -jnp.inf, 0.0, m_new)
            acc = (acc + pv(vals, p_prev, c)) * jnp.exp2(m - m_safe)
            p_cur[c] = jnp.exp2(s - m_safe).astype(BF16)
            out.append((m_new, acc))
        return tuple(out)

    def attn_pair(jj, carry):
        carry = attn_step(2 * jj, carry, s0_ref, s1_ref, p0_ref, p1_ref)
        return attn_step(2 * jj + 1, carry, s1_ref, s0_ref, p1_ref, p0_ref)

    init = tuple((jnp.full((1, cols4), -jnp.inf, F32), jnp.zeros((LANES + SUM_ROWS, cols4), F32))
                 for _ in range(N_KV_A))
    res = lax.fori_loop(0, n_pairs, attn_pair, init)
    vals = values(2 * n_pairs - 1)
    o = []
    for c in range(N_KV_A):
        acc = res[c][1] + pv(vals, p1_ref, c)
        o.append(acc[0:LANES] / acc[LANES:LANES + 1])
    top = lax.broadcasted_iota(I32, (LANES, QBLK), 0) < HEAD_DIM
    for g in range(GROUP_A):
        cs = slice(g * QBLK, (g + 1) * QBLK)
        oa_ref[:, g * LANES:(g + 1) * LANES] = jnp.where(top, o[0][:, cs], o[1][:, cs]).T


def _dsa_prompt(qi, wit, qa, kib, kab, vat, batch, seq):
    nq = seq // QBLK
    assert seq % (2 * DSA_TS) == 0, "key tiles are visited in pairs"
    k_sel = max(1, min(TOPK_MAX, seq // 4))
    idx_bits = max(1, int(seq - 1).bit_length())
    qspec = lambda n: pl.BlockSpec((QBLK, n), lambda b, i: (b * nq + i, 0))
    kspec = pl.BlockSpec((seq, LANES), lambda b, i: (b, 0))
    return pl.pallas_call(
        functools.partial(_dsa_prompt_kernel, k_sel, idx_bits),
        grid=(batch, nq),
        in_specs=[qspec(Q_I_W), pl.BlockSpec((1, N_IDX_HEADS, QBLK), lambda b, i: (b, 0, i)), qspec(Q_A_W),
                  kspec, kspec,
                  pl.BlockSpec((1, seq // KBLK, N_KV_A * HEAD_DIM, KBLK), lambda b, i: (b, 0, 0, 0))],
        out_specs=qspec(Q_A_W),
        out_shape=jax.ShapeDtypeStruct((batch * seq, Q_A_W), F32),
        scratch_shapes=[pltpu.VMEM((seq, QBLK), F32)]
                       + [pltpu.VMEM((DSA_TS, N_IDX_HEADS * QBLK), F32)] * 2
                       + [pltpu.VMEM((N_KV_A, DSA_TS, GROUP_A * QBLK), F32)] * 2
                       + [pltpu.VMEM((N_KV_A, DSA_TS, GROUP_A * QBLK), BF16)] * 2,
        compiler_params=_cparams("parallel", "arbitrary"),
        name="dsa_prompt",
    )(qi, wit, qa, kib, kab, vat)


def _later_matrix(n, key_axis):
    r = lax.broadcasted_iota(I32, (n, n), 0)
    c = lax.broadcasted_iota(I32, (n, n), 1)
    return jnp.where(r > c if key_axis == 1 else c > r, 1.0, 0.0).astype(BF16)


def _sb_tile(z, strict, later_m, carry, key_axis):
    soft = jnp.log(1.0 + jnp.exp(-jnp.abs(z)))
    log_beta = jnp.minimum(z, 0.0) - soft
    log_rest = log_beta - z
    if strict is not None:
        log_rest = jnp.where(strict, log_rest, 0.0)
    hi = log_rest.astype(BF16)
    lo = (log_rest - hi.astype(F32)).astype(BF16)
    if key_axis == 1:
        later = _dot(hi, later_m) + _dot(lo, later_m)
        first_l, first_r = later[:, 0:1], log_rest[:, 0:1]
    else:
        later = _dot(later_m, hi) + _dot(later_m, lo)
        first_l, first_r = later[0:1, :], log_rest[0:1, :]
    a = jnp.exp(log_beta + later + carry)
    if strict is not None:
        a = jnp.where(strict, a, 0.0)
    return a, carry + first_l + first_r


SB_TS = 128
SB_QBLK = 128


def _sb_prompt_kernel(qb_ref, kb_ref, vbt_ref, ob_ref, acc_ref, carry_ref, z_ref):
    i = pl.program_id(1)
    ts = SB_TS
    n_pairs = N_HEADS_B // 2
    qb = SB_QBLK
    pcols = 2 * qb
    cols = n_pairs * pcols
    lane = lax.broadcasted_iota(I32, (qb, LANES), 1)
    left = lane < HEAD_DIM
    zero = jnp.zeros((qb, LANES), BF16)
    later_m = _later_matrix(ts, 0)
    krow = lax.broadcasted_iota(I32, (ts, cols), 0)
    qpos = i * qb + (lax.broadcasted_iota(I32, (ts, cols), 1) & (qb - 1))
    top = lax.broadcasted_iota(I32, (LANES, qb), 0) < HEAD_DIM
    pair = [slice(p * LANES, (p + 1) * LANES) for p in range(n_pairs)]

    q2 = []
    for p in range(n_pairs):
        chunk = qb_ref[:, pair[p]]
        q2.append(jnp.concatenate([jnp.where(left, chunk, zero), jnp.where(left, zero, chunk)], axis=0))
    acc_ref[...] = jnp.zeros_like(acc_ref)
    carry_ref[...] = jnp.zeros_like(carry_ref)

    def cond(state):
        j, go = state
        return jnp.logical_and(j >= 0, go)

    def logits(j):
        off = pl.multiple_of(jnp.maximum(j, 0) * ts, ts)
        return jnp.concatenate([_dot_nt(kb_ref[pl.ds(off, ts), pair[p]], q2[p]) for p in range(n_pairs)],
                               axis=1)

    def body(state):
        j, _ = state
        z = z_ref[...]
        z_next = logits(j - 1)
        strict = (j * ts + krow) < qpos
        a, new_carry = _sb_tile(z, strict, later_m, carry_ref[0:1, :], 0)
        ab = a.astype(BF16)
        for p in range(n_pairs):
            acc_ref[p] += _dot(vbt_ref[0, j, pair[p], :], ab[:, p * pcols:(p + 1) * pcols])
        carry_ref[...] = jnp.broadcast_to(new_carry, carry_ref.shape)
        z_ref[...] = z_next
        return j - 1, jnp.max(new_carry) >= SB_STOP

    j0 = (i * qb + qb) // ts - 1
    z_ref[...] = logits(j0)
    lax.while_loop(cond, body, (j0, jnp.bool_(True)))
    for p in range(n_pairs):
        acc = acc_ref[p]
        ob_ref[:, pair[p]] = jnp.where(top, acc[:, 0:qb], acc[:, qb:pcols]).T


def _sb_prompt(qb, kbb, vbt, batch, seq):
    nq = seq // SB_QBLK
    qspec = pl.BlockSpec((SB_QBLK, Q_B_W), lambda b, i: (b * nq + i, 0))
    return pl.pallas_call(
        _sb_prompt_kernel,
        grid=(batch, nq),
        in_specs=[qspec, pl.BlockSpec((seq, Q_B_W), lambda b, i: (b, 0)),
                  pl.BlockSpec((1, seq // KBLK, Q_B_W, KBLK), lambda b, i: (b, 0, 0, 0))],
        out_specs=qspec,
        out_shape=jax.ShapeDtypeStruct((batch * seq, Q_B_W), F32),
        scratch_shapes=[pltpu.VMEM((N_HEADS_B // 2, LANES, 2 * SB_QBLK), F32),
                        pltpu.VMEM((SUBLANES, N_HEADS_B * SB_QBLK), F32),
                        pltpu.VMEM((SB_TS, N_HEADS_B * SB_QBLK), F32)],
        compiler_params=_cparams("parallel", "arbitrary"),
        name="sb_prompt",
    )(qb, kbb, vbt)


TSLOTS = 8
SROWS = TSLOTS * N_HEADS_A


def _largest_divisor(n, cap):
    return max(x for x in range(1, cap + 1) if n % x == 0)


def _page_copies(pool_hbm, pt_ref, seq, buf, slot, sem, n_pages, start):
    def body(p, _):
        cp = pltpu.make_async_copy(pool_hbm.at[pt_ref[seq, p]], buf.at[slot, p], sem.at[slot])
        if start:
            cp.start()
        else:
            cp.wait()
        return 0

    lax.fori_loop(0, n_pages, body, 0)


def _page_ring(pool_hbm, pt_ref, buf, sem, n_pages):
    b = pl.program_id(0)
    slot = b & 1

    @pl.when(b == 0)
    def _():
        _page_copies(pool_hbm, pt_ref, 0, buf, 0, sem, n_pages, True)

    @pl.when(b + 1 < pl.num_programs(0))
    def _():
        _page_copies(pool_hbm, pt_ref, b + 1, buf, 1 - slot, sem, n_pages, True)

    _page_copies(pool_hbm, pt_ref, b, buf, slot, sem, n_pages, False)
    return slot


def _dsa_sample_scores_kernel(n_pages, page, ppt, pt_ref, qi_ref, w_ref, kin_ref, kidx_hbm,
                              key_ref, buf, sem):
    slot = _page_ring(kidx_hbm, pt_ref, buf, sem, n_pages)
    ts = ppt * page
    qi = qi_ref[0]
    w = w_ref[0]

    def scores(kt):
        wrel = w * jnp.maximum(_dot(qi, kt), 0.0)
        return jnp.sum(wrel.reshape(TSLOTS, N_IDX_HEADS, kt.shape[1]), axis=1)

    for t in range(n_pages // ppt):
        kt = jnp.concatenate([buf[slot, t * ppt + q] for q in range(ppt)], axis=1).astype(BF16)
        key_ref[0, :, t * ts:(t + 1) * ts] = scores(kt)
    past = n_pages * page
    qslot = lax.broadcasted_iota(I32, (TSLOTS, page), 0)
    u = lax.broadcasted_iota(I32, (TSLOTS, page), 1)
    key_ref[0, :, past:past + page] = jnp.where(u <= qslot, scores(kin_ref[0]), -jnp.inf)


def _dsa_sample_select_kernel(k_sel, idx_bits, n_tiles, ts, key_ref, tau_ref, taui_ref):
    tau, tau_i = _topk_threshold(key_ref, n_tiles, ts, 1, key_ref.shape[0], k_sel, idx_bits)
    tau_ref[...] = jnp.broadcast_to(tau, tau_ref.shape)
    taui_ref[...] = jnp.broadcast_to(tau_i, taui_ref.shape)


def _dsa_sample_attend_kernel(n_pages, page, ppt, pt_ref, qa_ref, key_ref, tau_ref, taui_ref, kvn_ref, kva_hbm,
                              o_ref, buf, sem, s_ref):
    slot = _page_ring(kva_hbm, pt_ref, buf, sem, n_pages)
    ts = ppt * page
    past = n_pages * page
    kfeat = N_KV_A * HEAD_DIM
    qa = qa_ref[0]
    tau = tau_ref[:, 0:1]
    tau_i = taui_ref[:, 0:1]

    n_t = n_pages // ppt

    def masked_scores(kt, kpos0, k):
        n = kt.shape[1]
        kpos = kpos0 + lax.broadcasted_iota(I32, (TSLOTS, n), 1)
        qpos = past + lax.broadcasted_iota(I32, (TSLOTS, n), 0)
        sel = ((kt > tau) | ((kt == tau) & (kpos <= tau_i))) & (kpos <= qpos)
        bias = jnp.where(sel, 0.0, -jnp.inf)
        bias_rows = jnp.concatenate(
            [jnp.broadcast_to(bias[t:t + 1, :], (N_HEADS_A, n)) for t in range(TSLOTS)], axis=0)
        return _dot(qa, k) + bias_rows

    def pages(t, f0):
        return jnp.concatenate([buf[slot, t * ppt + q, f0:f0 + kfeat, :] for q in range(ppt)],
                               axis=1).astype(BF16)

    mx = jnp.full((SROWS, ts), -jnp.inf, F32)
    for t in range(n_t):
        s = masked_scores(key_ref[0, :, t * ts:(t + 1) * ts], t * ts, pages(t, 0))
        s_ref[:, t * ts:(t + 1) * ts] = s
        mx = jnp.maximum(mx, s)
    s_new = masked_scores(key_ref[0, :, past:past + page], past, kvn_ref[0, 0:kfeat, :])
    m = jnp.maximum(jnp.max(mx, axis=1, keepdims=True), jnp.max(s_new, axis=1, keepdims=True))
    m = jnp.where(m == -jnp.inf, 0.0, m)

    p_new = jnp.exp2(s_new - m)
    acc = _dot_nt(p_new.astype(BF16), kvn_ref[0, kfeat:2 * kfeat, :])
    psum = jnp.zeros((SROWS, ts), F32)
    for t in range(n_t):
        p = jnp.exp2(s_ref[:, t * ts:(t + 1) * ts] - m)
        psum = psum + p
        acc = acc + _dot_nt(p.astype(BF16), pages(t, kfeat))
    l = jnp.sum(psum, axis=1, keepdims=True) + jnp.sum(p_new, axis=1, keepdims=True)
    o_ref[0] = acc / jnp.where(l > 0.0, l, 1.0)


def _dsa_sample(qi_st, w_st, qa_st, kidx_pool, kva_pool, kin_t, kvn_t, page_table, dec_seq):
    n_seq, n_pages = page_table.shape
    page = kidx_pool.shape[2]
    nk = (n_pages + 1) * page
    ppt = _largest_divisor(n_pages, 8)
    sel_ts = _largest_divisor(n_pages + 1, 8) * page
    k_sel = max(1, min(TOPK_MAX, (n_pages * page + dec_seq) // 4))
    idx_bits = max(1, int(nk - 1).bit_length())
    seq_spec = lambda shp: pl.BlockSpec((1,) + shp, lambda b, pt: (b, 0, 0))
    hbm_spec = pl.BlockSpec(memory_space=pl.ANY)
    ring = lambda feat: [pltpu.VMEM((2, n_pages, feat, page), F32), pltpu.SemaphoreType.DMA((2,))]

    keys = pl.pallas_call(
        functools.partial(_dsa_sample_scores_kernel, n_pages, page, ppt),
        grid_spec=pltpu.PrefetchScalarGridSpec(
            num_scalar_prefetch=1, grid=(n_seq,),
            in_specs=[seq_spec((SROWS, IDX_DIM)), seq_spec((SROWS, 1)), seq_spec((IDX_DIM, page)), hbm_spec],
            out_specs=seq_spec((TSLOTS, nk)),
            scratch_shapes=ring(IDX_DIM)),
        out_shape=jax.ShapeDtypeStruct((n_seq, TSLOTS, nk), F32),
        compiler_params=_cparams("arbitrary"),
        name="dsa_sample_scores",
    )(page_table, qi_st, w_st, kin_t, kidx_pool)

    rows = n_seq * dec_seq
    tau, tau_i = pl.pallas_call(
        functools.partial(_dsa_sample_select_kernel, k_sel, idx_bits, nk // sel_ts, sel_ts),
        grid=(1,),
        in_specs=[_const_spec((rows, nk))],
        out_specs=[pl.BlockSpec((rows, LANES), lambda i: (0, 0))] * 2,
        out_shape=[jax.ShapeDtypeStruct((rows, LANES), F32), jax.ShapeDtypeStruct((rows, LANES), I32)],
        compiler_params=_cparams("arbitrary"),
        name="dsa_sample_select",
    )(keys[:, :dec_seq].reshape(rows, nk))
    pad_slots = lambda a: jnp.pad(a.reshape(n_seq, dec_seq, LANES),
                                  ((0, 0), (0, TSLOTS - dec_seq), (0, 0))).reshape(n_seq * TSLOTS, LANES)
    tau, tau_i = pad_slots(tau), pad_slots(tau_i)

    row_spec = pl.BlockSpec((TSLOTS, LANES), lambda b, pt: (b, 0))
    return pl.pallas_call(
        functools.partial(_dsa_sample_attend_kernel, n_pages, page, ppt),
        grid_spec=pltpu.PrefetchScalarGridSpec(
            num_scalar_prefetch=1, grid=(n_seq,),
            in_specs=[seq_spec((SROWS, LANES)), seq_spec((TSLOTS, nk)), row_spec, row_spec,
                      seq_spec((KV_A_W, page)), hbm_spec],
            out_specs=seq_spec((SROWS, LANES)),
            scratch_shapes=ring(KV_A_W) + [pltpu.VMEM((SROWS, n_pages * page), F32)]),
        out_shape=jax.ShapeDtypeStruct((n_seq, SROWS, LANES), F32),
        compiler_params=_cparams("arbitrary"),
        name="dsa_sample_attend",
    )(page_table, qa_st, keys, tau, tau_i, kvn_t, kva_pool)


def _sb_sample_kernel(n_pages, page, pt_ref, q_ref, kvn_ref, kvb_hbm, o_ref, buf, sem, acc_ref, carry_ref):
    b = pl.program_id(0)
    q = q_ref[0]
    later_m = _later_matrix(page, 1)
    last = n_pages - 1

    def page_copy(pg):
        slot = (last - pg) & 1
        return pltpu.make_async_copy(kvb_hbm.at[pt_ref[b, pg]], buf.at[slot], sem.at[slot])

    def tile(kt, vt, strict):
        a, new_carry = _sb_tile(_dot(q, kt), strict, later_m, carry_ref[:, 0:1], 1)
        acc_ref[...] += _dot_nt(a.astype(BF16), vt)
        carry_ref[...] = jnp.broadcast_to(new_carry, carry_ref.shape)
        return jnp.max(new_carry) >= SB_STOP

    page_copy(last).start()
    acc_ref[...] = jnp.zeros_like(acc_ref)
    carry_ref[...] = jnp.zeros_like(carry_ref)
    qslot = lax.broadcasted_iota(I32, (SROWS, page), 0) // N_HEADS_B
    u = lax.broadcasted_iota(I32, (SROWS, page), 1)
    go0 = tile(kvn_ref[0, 0:Q_B_W, :], kvn_ref[0, Q_B_W:2 * Q_B_W, :], u < qslot)

    def cond(state):
        pg, go = state
        return jnp.logical_and(pg >= 0, go)

    def body(state):
        pg, _ = state
        page_copy(pg).wait()

        @pl.when(pg > 0)
        def _():
            page_copy(pg - 1).start()

        slot = (last - pg) & 1
        go = tile(buf[slot, 0:Q_B_W, :].astype(BF16), buf[slot, Q_B_W:2 * Q_B_W, :].astype(BF16), None)
        return pg - 1, go

    pg_end, _ = lax.while_loop(cond, body, (jnp.int32(last), go0))

    @pl.when(pg_end >= 0)
    def _():
        page_copy(pg_end).wait()

    r = lax.broadcasted_iota(I32, (SROWS, Q_B_W), 0)
    c = lax.broadcasted_iota(I32, (SROWS, Q_B_W), 1)
    own = (r % N_HEADS_B) == (c // HEAD_DIM)
    diag = jnp.where(own, acc_ref[...], 0.0)
    o_ref[0] = jnp.sum(diag.reshape(TSLOTS, N_HEADS_B, Q_B_W), axis=1)


def _sb_sample(qb_bd, kvb_pool, kvn_t, page_table):
    n_seq, n_pages = page_table.shape
    page = kvb_pool.shape[2]
    seq_spec = lambda shp: pl.BlockSpec((1,) + shp, lambda b, pt: (b, 0, 0))
    return pl.pallas_call(
        functools.partial(_sb_sample_kernel, n_pages, page),
        grid_spec=pltpu.PrefetchScalarGridSpec(
            num_scalar_prefetch=1, grid=(n_seq,),
            in_specs=[seq_spec((SROWS, Q_B_W)), seq_spec((KV_B_W, page)), pl.BlockSpec(memory_space=pl.ANY)],
            out_specs=seq_spec((TSLOTS, Q_B_W)),
            scratch_shapes=[pltpu.VMEM((2, KV_B_W, page), F32), pltpu.SemaphoreType.DMA((2,)),
                            pltpu.VMEM((SROWS, Q_B_W), F32), pltpu.VMEM((SROWS, LANES), F32)]),
        out_shape=jax.ShapeDtypeStruct((n_seq, TSLOTS, Q_B_W), F32),
        compiler_params=_cparams("arbitrary"),
        name="sb_sample",
    )(page_table, qb_bd, kvn_t, kvb_pool)


def _out_kernel(oa_ref, ob_ref, gate_ref, x_ref, wa_ref, wb_ref, wo_ref, nf_ref,
                wg_ref, wu_ref, wd_ref, nfin_ref, y_ref):
    ba = _dot(oa_ref[...].astype(BF16), wa_ref[...])
    bb = _dot(ob_ref[...].astype(BF16), wb_ref[...])
    merged = (gate_ref[:, 0:D_MODEL].astype(F32) * ba
              + gate_ref[:, D_MODEL:2 * D_MODEL].astype(F32) * bb)
    h1 = x_ref[...] + _dot(merged.astype(BF16), wo_ref[...])
    ms = jnp.mean(h1 * h1, axis=-1, keepdims=True)
    hn = ((h1 * lax.rsqrt(ms + RMS_EPS)) * nf_ref[...]).astype(BF16)
    acc = jnp.zeros_like(h1)
    for c in range(D_FF // FFN_CHUNK):
        sl = slice(c * FFN_CHUNK, (c + 1) * FFN_CHUNK)
        gg = _dot(hn, wg_ref[:, sl])
        uu = _dot(hn, wu_ref[:, sl])
        act = (gg * (1.0 / (1.0 + jnp.exp(-gg)))) * uu
        acc = acc + _dot(act.astype(BF16), wd_ref[sl, :])
    h2 = h1 + acc
    ms2 = jnp.mean(h2 * h2, axis=-1, keepdims=True)
    y_ref[...] = (h2 * lax.rsqrt(ms2 + RMS_EPS)) * nfin_ref[...]


def _out(oa, ob, gates, x2d, wts, tm):
    rows = x2d.shape[0]
    row_spec = lambda n: pl.BlockSpec((tm, n), lambda i: (i, 0))
    return pl.pallas_call(
        _out_kernel,
        grid=(rows // tm,),
        in_specs=[row_spec(Q_A_W), row_spec(Q_B_W), row_spec(2 * D_MODEL), row_spec(D_MODEL)]
                 + [_const_spec(w.shape) for w in wts],
        out_specs=row_spec(D_MODEL),
        out_shape=jax.ShapeDtypeStruct((rows, D_MODEL), F32),
        compiler_params=_cparams("parallel"),
        name="out",
    )(oa, ob, gates, x2d, *wts)


def _rope_tables(pos):
    inv_freq = ROPE_THETA ** (-jnp.arange(HALF, dtype=F32) / HALF)
    ang = pos.astype(F32)[:, None] * inv_freq[None, :]
    cos, sin = jnp.cos(ang), jnp.sin(ang)
    cs = jnp.concatenate([cos, cos, cos, cos], axis=1)
    sn = jnp.concatenate([-sin, sin, -sin, sin], axis=1)
    return cs, sn, cos.T, sin.T


def _pack_w_in(w):
    d = w.shape[0]
    o = np.cumsum([0, Q_A_W, KV_A_W, Q_I_W, IDX_DIM, N_IDX_HEADS, Q_B_W, KV_B_W, 2 * D_MODEL])
    kw = N_KV_A * HEAD_DIM
    qa = w[:, o[0]:o[1]].reshape(d, N_HEADS_A, HEAD_DIM)[:, np.array(QA_HEAD_ORDER), :].reshape(d, Q_A_W)
    ka, va = w[:, o[1]:o[1] + kw], w[:, o[1] + kw:o[2]]
    qi, ki, wi, qb = w[:, o[2]:o[3]], w[:, o[3]:o[4]], w[:, o[4]:o[5]], w[:, o[5]:o[6]]
    kb, vb = w[:, o[6]:o[6] + Q_B_W], w[:, o[6] + Q_B_W:o[7]]
    g = w[:, o[7]:o[8]]
    w_tok = jnp.concatenate([qa, qi, qb, g], axis=1).astype(BF16)
    wi_pad = jnp.pad(wi, ((0, 0), (0, HEAD_DIM - N_IDX_HEADS)))
    w_feat = jnp.concatenate([ka, va, ki, wi_pad, kb, vb], axis=1).T.astype(BF16)
    return w_tok, w_feat


def kernel(x_prompt, x_sample, cache_kv_a, cache_k_idx, cache_kv_b, page_table, w_in, w_br_a, w_br_b,
           w_o, norm_attn, norm_ffn, w_ffn_gate, w_ffn_up, w_ffn_down, norm_final):
    depth = w_in.shape[0]
    assert depth == 1, "single-layer step"
    batch, seq, d = x_prompt.shape
    assert d == D_MODEL and seq % ROW_TILE == 0
    layer = 0

    w_tok, w_feat = _pack_w_in(w_in[layer])
    wa = w_br_a[layer].reshape(N_HEADS_A, HEAD_DIM, D_MODEL)[np.array(QA_HEAD_ORDER)].reshape(Q_A_W, D_MODEL)
    out_wts = (wa.astype(BF16), w_br_b[layer].astype(BF16), w_o[layer].astype(BF16),
               norm_ffn[layer].reshape(1, D_MODEL),
               w_ffn_gate[layer].astype(BF16), w_ffn_up[layer].astype(BF16), w_ffn_down[layer].astype(BF16),
               norm_final.reshape(1, D_MODEL))
    gain = norm_attn[layer].reshape(1, D_MODEL)

    xp = x_prompt.reshape(batch * seq, D_MODEL)
    (kva_p, kidx_p, kvb_p, qa, qi, wit, qb, gates, kab, kib, kbb, vat, vbt) = _project(
        xp, gain, w_tok, w_feat, *_rope_tables(jnp.arange(seq, dtype=I32)), batch, ROW_TILE)
    oa = _dsa_prompt(qi, wit, qa, kib, kab, vat, batch, seq)
    ob = _sb_prompt(qb, kbb, vbt, batch, seq)
    y_prompt = _out(oa, ob, gates, xp, out_wts, ROW_TILE).reshape(batch, seq, D_MODEL)

    new_kv_a_p = kva_p.reshape(depth, batch, 2, N_KV_A, HEAD_DIM, seq).transpose(0, 1, 5, 2, 3, 4)
    new_k_idx_p = kidx_p.reshape(depth, batch, IDX_DIM, seq).transpose(0, 1, 3, 2)
    new_kv_b_p = kvb_p.reshape(depth, batch, 2, N_HEADS_B, HEAD_DIM, seq).transpose(0, 1, 5, 2, 3, 4)

    n_seq, dec_seq, _ = x_sample.shape
    n_pages = page_table.shape[1]
    n_pool, page = cache_kv_a.shape[1], cache_kv_a.shape[2]
    past = n_pages * page
    rows_s = n_seq * dec_seq
    assert dec_seq <= TSLOTS and rows_s % KBLK == 0 and page == LANES
    xs = x_sample.reshape(rows_s, D_MODEL)
    cs_s, sn_s, csf_s, snf_s = _rope_tables(past + jnp.arange(dec_seq, dtype=I32))
    tabs_s = (jnp.tile(cs_s, (n_seq, 1)), jnp.tile(sn_s, (n_seq, 1)),
              jnp.tile(csf_s, (1, n_seq)), jnp.tile(snf_s, (1, n_seq)))
    (kva_s, kidx_s, kvb_s, qa_s, qi_s, wit_s, qb_s, gates_s, _, _, _, _, _) = _project(
        xs, gain, w_tok, w_feat, *tabs_s, 1, rows_s)

    def slots(a):
        return jnp.pad(a, ((0, 0), (0, TSLOTS - dec_seq)) + ((0, 0),) * (a.ndim - 2))

    def new_tile(a):
        t = a[0].reshape(a.shape[1], n_seq, dec_seq).transpose(1, 0, 2)
        return jnp.pad(t, ((0, 0), (0, 0), (0, page - dec_seq))).astype(BF16)

    qi_st = slots(qi_s.reshape(n_seq, dec_seq, N_IDX_HEADS, IDX_DIM)).reshape(n_seq, SROWS, IDX_DIM)
    w_st = slots(wit_s[0].T.reshape(n_seq, dec_seq, N_IDX_HEADS)).reshape(n_seq, SROWS, 1)
    qa5 = qa_s.reshape(n_seq, dec_seq, GROUP_A, N_KV_A, HEAD_DIM).transpose(0, 1, 3, 2, 4)
    eye2 = jnp.eye(N_KV_A, dtype=qa_s.dtype)
    qa_st = qa5[:, :, :, :, None, :] * eye2[None, None, :, None, :, None]
    qa_st = slots(qa_st.reshape(n_seq, dec_seq, N_HEADS_A, LANES)).reshape(n_seq, SROWS, LANES)
    kidx_pool = cache_k_idx[layer].transpose(0, 2, 1)
    kva_pool = cache_kv_a[layer].transpose(0, 2, 3, 4, 1).reshape(n_pool, KV_A_W, page)
    kvb_pool = cache_kv_b[layer].transpose(0, 2, 3, 4, 1).reshape(n_pool, KV_B_W, page)
    o_st = _dsa_sample(qi_st, w_st, qa_st, kidx_pool, kva_pool, new_tile(kidx_s), new_tile(kva_s),
                       page_table, dec_seq)
    o6 = o_st.reshape(n_seq, TSLOTS, N_KV_A, GROUP_A, N_KV_A, HEAD_DIM)[:, :dec_seq]
    oa_s = jnp.stack([o6[:, :, c, :, c, :] for c in range(N_KV_A)], axis=2).reshape(rows_s, Q_A_W)

    qb4 = qb_s.reshape(n_seq, dec_seq, N_HEADS_B, HEAD_DIM)
    eye8 = jnp.eye(N_HEADS_B, dtype=qb_s.dtype)
    qb_bd = qb4[:, :, :, None, :] * eye8[None, None, :, :, None]
    qb_bd = slots(qb_bd.reshape(n_seq, dec_seq, N_HEADS_B, Q_B_W)).reshape(n_seq, SROWS, Q_B_W)
    ob_st = _sb_sample(qb_bd, kvb_pool, new_tile(kvb_s), page_table)
    ob_s = ob_st[:, :dec_seq].reshape(rows_s, Q_B_W)

    out_wts_s = (w_br_a[layer].astype(BF16),) + out_wts[1:]
    y_sample = _out(oa_s, ob_s, gates_s, xs, out_wts_s, rows_s).reshape(n_seq, dec_seq, D_MODEL)

    return (y_prompt, y_sample, new_kv_a_p, new_k_idx_p, new_kv_b_p,
            kva_s[0].T.reshape(depth, n_seq, dec_seq, 2, N_KV_A, HEAD_DIM),
            kidx_s[0].T.reshape(depth, n_seq, dec_seq, IDX_DIM),
            kvb_s[0].T.reshape(depth, n_seq, dec_seq, 2, N_HEADS_B, HEAD_DIM))
```

```python
import functools

import jax
import jax.numpy as jnp
import numpy as np
from jax import lax
from jax.experimental import pallas as pl
from jax.experimental.pallas import tpu as pltpu

F32 = jnp.float32
BF16 = jnp.bfloat16
I32 = jnp.int32

D_MODEL = 1024
HEAD_DIM = 64
N_HEADS_A = 8
N_KV_A = 2
GROUP_A = N_HEADS_A // N_KV_A
N_IDX_HEADS = 8
IDX_DIM = 64
TOPK_MAX = 256
N_HEADS_B = 8
Q_A_W = N_HEADS_A * HEAD_DIM
KV_A_W = 2 * N_KV_A * HEAD_DIM
Q_I_W = N_IDX_HEADS * IDX_DIM
Q_B_W = N_HEADS_B * HEAD_DIM
KV_B_W = 2 * N_HEADS_B * HEAD_DIM
D_FF = -(-8 * D_MODEL // (3 * 256)) * 256
ROPE_THETA = 10000.0
RMS_EPS = 1e-6
QK_SCALE = HEAD_DIM ** -0.5
IDX_SCALE = IDX_DIM ** -0.5
HALF = HEAD_DIM // 2
LOG2_E = 1.4426950408889634

LANES = 128
SUBLANES = 8
QBLK = 256
KBLK = 128
SUM_ROWS = 16
FFN_CHUNK = 256
ROW_TILE = 512

TK_QA = 0
TK_QI = 512
TK_QB = 1024
TK_G = 1536
TK_W = 3584
FM_KA = 0
FM_VA = 128
FM_KI = 256
FM_WI = 320
FM_KB = 384
FM_VB = 896
FM_W = 1408
QA_HEAD_ORDER = (0, 4, 1, 5, 2, 6, 3, 7)

NEGINF_KEY = -2139095041
POSINF_KEY = 0x7F800000
MIN_NORMAL_KEY = 0x00800000
IDX_ALL = 1 << 30
SB_STOP = -105.0

VMEM_LIMIT = 60 * 1024 * 1024

NT_DIMS = (((1,), (1,)), ((), ()))


def _cparams(*sem):
    return pltpu.CompilerParams(dimension_semantics=sem, vmem_limit_bytes=VMEM_LIMIT)


def _const_spec(shape):
    nd = len(shape)
    return pl.BlockSpec(shape, lambda *_: (0,) * nd, pipeline_mode=pl.Buffered(1))


def _dot_nt(a, b):
    return lax.dot_general(a, b, NT_DIMS, preferred_element_type=F32)


def _dot(a, b):
    return jnp.dot(a, b, preferred_element_type=F32)


def _project_kernel(x_ref, g_ref, wt_ref, wf_ref, cs_ref, sn_ref, csf_ref, snf_ref,
                    kva_ref, kidx_ref, kvb_ref,
                    qa_ref, qi_ref, wit_ref, qb_ref, gate_ref,
                    kab_ref, kib_ref, kbb_ref, vat_ref, vbt_ref):
    x = x_ref[...]
    ms = jnp.mean(x * x, axis=-1, keepdims=True)
    xn = (x * lax.rsqrt(ms + RMS_EPS)) * g_ref[...]
    xb = xn.astype(BF16)
    tm = x.shape[0]
    cs = cs_ref[...]
    sn = sn_ref[...]
    lane = lax.broadcasted_iota(I32, (tm, LANES), 1)
    first_half = (lane & HALF) == 0

    def mm(c0, n):
        return _dot(xb, wt_ref[:, c0:c0 + n])

    def rope(y):
        partner = jnp.where(first_half, pltpu.roll(y, LANES - HALF, 1), pltpu.roll(y, HALF, 1))
        return y * cs + partner * sn

    ya = mm(TK_QA, Q_A_W)
    for c in range(Q_A_W // LANES):
        sl = slice(c * LANES, (c + 1) * LANES)
        qa_ref[:, sl] = (rope(ya[:, sl]) * (QK_SCALE * LOG2_E)).astype(BF16)
    yi = mm(TK_QI, Q_I_W)
    for c in range(Q_I_W // LANES):
        sl = slice(c * LANES, (c + 1) * LANES)
        qi_ref[:, sl] = rope(yi[:, sl]).astype(BF16)
    qb_ref[...] = (mm(TK_QB, Q_B_W) * QK_SCALE).astype(BF16)
    for c in range(2 * D_MODEL // 512):
        sl = slice(c * 512, (c + 1) * 512)
        y = mm(TK_G + c * 512, 512)
        gate_ref[:, sl] = (1.0 / (1.0 + jnp.exp(-y))).astype(BF16)

    csf = csf_ref[...]
    snf = snf_ref[...]

    def fm(r0, n):
        return _dot_nt(wf_ref[r0:r0 + n, :], xb)

    def rope_fm(y):
        lo, hi = y[0:HALF], y[HALF:HEAD_DIM]
        return jnp.concatenate([lo * csf - hi * snf, hi * csf + lo * snf], axis=0)

    ka = fm(FM_KA, N_KV_A * HEAD_DIM)
    ka = jnp.concatenate([rope_fm(ka[c * HEAD_DIM:(c + 1) * HEAD_DIM]) for c in range(N_KV_A)], axis=0)
    kva_ref[0, 0:N_KV_A * HEAD_DIM, :] = ka
    kab_ref[...] = ka.T.astype(BF16)
    va = fm(FM_VA, N_KV_A * HEAD_DIM)
    kva_ref[0, N_KV_A * HEAD_DIM:2 * N_KV_A * HEAD_DIM, :] = va
    ki = rope_fm(fm(FM_KI, IDX_DIM))
    kidx_ref[0] = ki
    kib_ref[...] = jnp.concatenate([ki, ki], axis=0).T.astype(BF16)
    wi = fm(FM_WI, HEAD_DIM)
    wit_ref[0] = (wi[0:N_IDX_HEADS] * (N_IDX_HEADS ** -0.5)) * IDX_SCALE
    kb = fm(FM_KB, Q_B_W)
    kvb_ref[0, 0:Q_B_W, :] = kb
    kbb_ref[...] = kb.T.astype(BF16)
    vb = fm(FM_VB, Q_B_W)
    kvb_ref[0, Q_B_W:2 * Q_B_W, :] = vb
    for q in range(tm // KBLK):
        sl = slice(q * KBLK, (q + 1) * KBLK)
        vat_ref[0, q] = va[:, sl].astype(BF16)
        vbt_ref[0, q] = vb[:, sl].astype(BF16)


def _project(x2d, gain, w_tok, w_feat, cs, sn, csf, snf, groups, tm):
    rows = x2d.shape[0]
    t = rows // groups
    nb = t // tm
    nkb = tm // KBLK
    row_spec = lambda n: pl.BlockSpec((tm, n), lambda i: (i, 0))
    tab_spec = pl.BlockSpec((tm, LANES), lambda i: (i % nb, 0))
    tabf_spec = pl.BlockSpec((HALF, tm), lambda i: (0, i % nb))
    fm_spec = lambda n: pl.BlockSpec((1, n, tm), lambda i: (i // nb, 0, i % nb))
    blk_spec = lambda n: pl.BlockSpec((1, nkb, n, KBLK), lambda i: (i // nb, i % nb, 0, 0))
    out_specs = [fm_spec(KV_A_W), fm_spec(IDX_DIM), fm_spec(KV_B_W),
                 row_spec(Q_A_W), row_spec(Q_I_W), fm_spec(N_IDX_HEADS), row_spec(Q_B_W), row_spec(2 * D_MODEL),
                 row_spec(LANES), row_spec(LANES), row_spec(Q_B_W),
                 blk_spec(N_KV_A * HEAD_DIM), blk_spec(Q_B_W)]
    fm_shape = lambda n, dt: jax.ShapeDtypeStruct((groups, n, t), dt)
    tk_shape = lambda n, dt: jax.ShapeDtypeStruct((rows, n), dt)
    blk_shape = lambda n: jax.ShapeDtypeStruct((groups, t // KBLK, n, KBLK), BF16)
    out_shape = [fm_shape(KV_A_W, F32), fm_shape(IDX_DIM, F32), fm_shape(KV_B_W, F32),
                 tk_shape(Q_A_W, BF16), tk_shape(Q_I_W, BF16), fm_shape(N_IDX_HEADS, F32),
                 tk_shape(Q_B_W, BF16), tk_shape(2 * D_MODEL, BF16),
                 tk_shape(LANES, BF16), tk_shape(LANES, BF16), tk_shape(Q_B_W, BF16),
                 blk_shape(N_KV_A * HEAD_DIM), blk_shape(Q_B_W)]
    return pl.pallas_call(
        _project_kernel,
        grid=(rows // tm,),
        in_specs=[row_spec(D_MODEL), _const_spec((1, D_MODEL)), _const_spec((D_MODEL, TK_W)),
                  _const_spec((FM_W, D_MODEL)), tab_spec, tab_spec, tabf_spec, tabf_spec],
        out_specs=out_specs,
        out_shape=out_shape,
        compiler_params=_cparams("parallel"),
        name="project",
    )(x2d, gain, w_tok, w_feat, cs, sn, csf, snf)


def _key_to_float(k):
    return lax.bitcast_convert_type(k ^ ((k >> 31) & jnp.int32(0x7FFFFFFF)), F32)


def _fold_rows(x, op, rows=SUBLANES):
    n = x.shape[0]
    assert n % rows == 0 and (n // rows) & (n // rows - 1) == 0
    while n > rows:
        n //= 2
        x = op(x[:n], x[n:])
    return x


def _count(key_ref, n_tiles, ts, key_axis, width, pred):
    if key_axis == 1:
        def body(j, acc):
            off = pl.multiple_of(j * ts, ts)
            kt = key_ref[:, pl.ds(off, ts)]
            kpos = j * ts + lax.broadcasted_iota(I32, (width, ts), 1)
            m = jnp.where(pred(kt, kpos, lambda v: v), 1.0, 0.0)
            s = m[:, 0:LANES]
            for c in range(1, ts // LANES):
                s = s + m[:, c * LANES:(c + 1) * LANES]
            return acc + s

        acc = lax.fori_loop(0, n_tiles, body, jnp.zeros((width, LANES), F32))
        return jnp.sum(acc, axis=1, keepdims=True)

    def body(j, acc):
        off = pl.multiple_of(j * ts, ts)
        kpos = j * ts + lax.broadcasted_iota(I32, (ts, LANES), 0)
        parts = []
        for g in range(width // LANES):
            gs = slice(g * LANES, (g + 1) * LANES)
            hit = pred(key_ref[pl.ds(off, ts), gs], kpos, lambda v, gs=gs: v[:, gs])
            parts.append(_fold_rows(jnp.where(hit, 1.0, 0.0), jnp.add))
        return acc + jnp.concatenate(parts, axis=1)

    acc = lax.fori_loop(0, n_tiles, body, jnp.zeros((SUBLANES, width), F32))
    return jnp.sum(acc, axis=0, keepdims=True)


def _topk_threshold(key_ref, n_tiles, ts, key_axis, width, k_sel, idx_bits):
    k_f = float(k_sel)
    vec = (width, 1) if key_axis == 1 else (1, width)
    count = functools.partial(_count, key_ref, n_tiles, ts, key_axis, width)

    def as_float(k):
        zero_band = (k >= -MIN_NORMAL_KEY) & (k < MIN_NORMAL_KEY)
        return _key_to_float(jnp.where(zero_band, 0, k))

    def bis(b, r):
        cand = r + lax.shift_left(jnp.int32(1), jnp.int32(31) - b)
        valid = (cand >= NEGINF_KEY) & (cand <= POSINF_KEY)
        cand_f = as_float(jnp.where(valid, cand, 0))
        return jnp.where(valid & (count(lambda st, kp, q: st >= q(cand_f)) >= k_f), cand, r)

    tau = as_float(lax.fori_loop(0, 32, bis, jnp.full(vec, NEGINF_KEY, I32)))
    need = k_f - count(lambda st, kp, q: st > q(tau))
    tie = (count(lambda st, kp, q: st == q(tau)) > need) & (tau > -jnp.inf)

    def resolve():
        def ib(b, x):
            c = x + lax.shift_left(jnp.int32(1), jnp.int32(idx_bits - 1) - b)
            cnt = count(lambda st, kp, q: (st == q(tau)) & (kp < q(c)))
            return jnp.where(cnt < need, c, x)

        x = lax.fori_loop(0, idx_bits, ib, jnp.zeros(vec, I32))
        return jnp.where(tie, x, IDX_ALL)

    any_tie = jnp.max(jnp.where(tie, 1.0, 0.0)) > 0.5
    tau_i = lax.cond(any_tie, resolve, lambda: jnp.full(vec, IDX_ALL, I32))
    return tau, tau_i


DSA_TS = 256


def _dsa_prompt_kernel(k_sel, idx_bits, qi_ref, wit_ref, qa_ref, ki_ref, ka_ref, vat_ref,
                       oa_ref, key_ref, rel0_ref, rel1_ref, s0_ref, s1_ref, p0_ref, p1_ref):
    i = pl.program_id(1)
    ts = DSA_TS
    n_tiles = (i * QBLK + QBLK + ts - 1) // ts
    lane = lax.broadcasted_iota(I32, (QBLK, LANES), 1)
    left = lane < HEAD_DIM
    krow = lax.broadcasted_iota(I32, (ts, QBLK), 0)
    qpos = i * QBLK + lax.broadcasted_iota(I32, (ts, QBLK), 1)

    zero = jnp.zeros((QBLK, LANES), BF16)
    qi_parts = []
    for p in range(Q_I_W // LANES):
        chunk = qi_ref[:, p * LANES:(p + 1) * LANES]
        qi_parts.append(jnp.where(left, chunk, zero))
        qi_parts.append(jnp.where(left, zero, chunk))
    qi_all = jnp.concatenate(qi_parts, axis=0)
    wt = wit_ref[0]

    last = n_tiles - 1
    n_pairs = (n_tiles + 1) // 2

    def tile_off(j):
        return pl.multiple_of(jnp.minimum(j, last) * ts, ts)

    def rel_tile(j):
        return _dot_nt(ki_ref[pl.ds(tile_off(j), ts), :], qi_all)

    def score_step(j, cur_ref, nxt_ref):
        rel = cur_ref[...]
        nxt_ref[...] = rel_tile(j + 1)
        sc = jnp.zeros((ts, QBLK), F32)
        for h in range(N_IDX_HEADS):
            sc = sc + wt[h:h + 1, :] * jnp.maximum(rel[:, h * QBLK:(h + 1) * QBLK], 0.0)
        sc = jnp.where(j * ts + krow <= qpos, sc, -jnp.inf)
        key_ref[pl.ds(pl.multiple_of(j * ts, ts), ts), :] = sc

    def score_pair(jj, _):
        score_step(2 * jj, rel0_ref, rel1_ref)
        score_step(2 * jj + 1, rel1_ref, rel0_ref)
        return 0

    q4 = []
    for c in range(N_KV_A):
        parts = []
        for g in range(GROUP_A):
            chunk = qa_ref[:, g * LANES:(g + 1) * LANES]
            parts.append(jnp.where(left, chunk, zero) if c == 0 else jnp.where(left, zero, chunk))
        q4.append(jnp.concatenate(parts, axis=0))

    cols4 = GROUP_A * QBLK
    kb_per_tile = ts // KBLK

    def s_tile(j):
        k = ka_ref[pl.ds(tile_off(j), ts), :]
        return [_dot_nt(k, q4[c]) for c in range(N_KV_A)]

    rel0_ref[...] = rel_tile(0)
    first = s_tile(0)
    for c in range(N_KV_A):
        s0_ref[c] = first[c]
    p1_ref[...] = jnp.zeros_like(p1_ref)
    lax.fori_loop(0, n_pairs, score_pair, 0)
    tau, tau_i = _topk_threshold(key_ref, n_pairs, 2 * ts, 0, QBLK, k_sel, idx_bits)

    ones_rows = jnp.ones((SUM_ROWS, ts), BF16)

    def values(j):
        jb = jnp.clip(j, 0, last) * kb_per_tile
        v = jnp.concatenate([vat_ref[0, jb + q] for q in range(kb_per_tile)], axis=1)
        return jnp.concatenate([v, ones_rows], axis=0)

    def pv(vals, p_ref, c):
        return _dot(vals, p_ref[c])

    def attn_step(j, carry, s_cur, s_nxt, p_cur, p_prev):
        kt = key_ref[pl.ds(tile_off(j), ts), :]
        kpos = j * ts + krow
        sel = ((kt > tau) | ((kt == tau) & (kpos <= tau_i))) & (kpos <= qpos)
        bias = jnp.where(sel, 0.0, -jnp.inf)
        bias4 = jnp.concatenate([bias] * GROUP_A, axis=1)
        nxt = s_tile(j + 1)
        vals = values(j - 1)
        out = []
        for c in range(N_KV_A):
            m, acc = carry[c]
            s = s_cur[c] + bias4
            s_nxt[c] = nxt[c]
            m_new = jnp.maximum(m, jnp.max(_fold_rows(s, jnp.maximum), axis=0, keepdims=True))
            m_safe = jnp.where(m_new == -jnp.inf, 0.0, m_new)
            acc = (acc + pv(vals, p_prev, c)) * jnp.exp2(m - m_safe)
            p_cur[c] = jnp.exp2(s - m_safe).astype(BF16)
            out.append((m_new, acc))
        return tuple(out)

    def attn_pair(jj, carry):
        carry = attn_step(2 * jj, carry, s0_ref, s1_ref, p0_ref, p1_ref)
        return attn_step(2 * jj + 1, carry, s1_ref, s0_ref, p1_ref, p0_ref)

    init = tuple((jnp.full((1, cols4), -jnp.inf, F32), jnp.zeros((LANES + SUM_ROWS, cols4), F32))
                 for _ in range(N_KV_A))
    res = lax.fori_loop(0, n_pairs, attn_pair, init)
    vals = values(2 * n_pairs - 1)
    o = []
    for c in range(N_KV_A):
        acc = res[c][1] + pv(vals, p1_ref, c)
        o.append(acc[0:LANES] / acc[LANES:LANES + 1])
    top = lax.broadcasted_iota(I32, (LANES, QBLK), 0) < HEAD_DIM
    for g in range(GROUP_A):
        cs = slice(g * QBLK, (g + 1) * QBLK)
        oa_ref[:, g * LANES:(g + 1) * LANES] = jnp.where(top, o[0][:, cs], o[1][:, cs]).T


def _dsa_prompt(qi, wit, qa, kib, kab, vat, batch, seq):
    nq = seq // QBLK
    assert seq % (2 * DSA_TS) == 0, "key tiles are visited in pairs"
    k_sel = max(1, min(TOPK_MAX, seq // 4))
    idx_bits = max(1, int(seq - 1).bit_length())
    qspec = lambda n: pl.BlockSpec((QBLK, n), lambda b, i: (b * nq + i, 0))
    kspec = pl.BlockSpec((seq, LANES), lambda b, i: (b, 0))
    return pl.pallas_call(
        functools.partial(_dsa_prompt_kernel, k_sel, idx_bits),
        grid=(batch, nq),
        in_specs=[qspec(Q_I_W), pl.BlockSpec((1, N_IDX_HEADS, QBLK), lambda b, i: (b, 0, i)), qspec(Q_A_W),
                  kspec, kspec,
                  pl.BlockSpec((1, seq // KBLK, N_KV_A * HEAD_DIM, KBLK), lambda b, i: (b, 0, 0, 0))],
        out_specs=qspec(Q_A_W),
        out_shape=jax.ShapeDtypeStruct((batch * seq, Q_A_W), F32),
        scratch_shapes=[pltpu.VMEM((seq, QBLK), F32)]
                       + [pltpu.VMEM((DSA_TS, N_IDX_HEADS * QBLK), F32)] * 2
                       + [pltpu.VMEM((N_KV_A, DSA_TS, GROUP_A * QBLK), F32)] * 2
                       + [pltpu.VMEM((N_KV_A, DSA_TS, GROUP_A * QBLK), BF16)] * 2,
        compiler_params=_cparams("parallel", "arbitrary"),
        name="dsa_prompt",
    )(qi, wit, qa, kib, kab, vat)


def _later_matrix(n, key_axis):
    r = lax.broadcasted_iota(I32, (n, n), 0)
    c = lax.broadcasted_iota(I32, (n, n), 1)
    return jnp.where(r > c if key_axis == 1 else c > r, 1.0, 0.0).astype(BF16)


def _sb_tile(z, strict, later_m, carry, key_axis):
    soft = jnp.log(1.0 + jnp.exp(-jnp.abs(z)))
    log_beta = jnp.minimum(z, 0.0) - soft
    log_rest = log_beta - z
    if strict is not None:
        log_rest = jnp.where(strict, log_rest, 0.0)
    hi = log_rest.astype(BF16)
    lo = (log_rest - hi.astype(F32)).astype(BF16)
    if key_axis == 1:
        later = _dot(hi, later_m) + _dot(lo, later_m)
        first_l, first_r = later[:, 0:1], log_rest[:, 0:1]
    else:
        later = _dot(later_m, hi) + _dot(later_m, lo)
        first_l, first_r = later[0:1, :], log_rest[0:1, :]
    a = jnp.exp(log_beta + later + carry)
    if strict is not None:
        a = jnp.where(strict, a, 0.0)
    return a, carry + first_l + first_r


SB_TS = 128
SB_QBLK = 128


def _sb_prompt_kernel(qb_ref, kb_ref, vbt_ref, ob_ref, acc_ref, carry_ref, z_ref):
    i = pl.program_id(1)
    ts = SB_TS
    n_pairs = N_HEADS_B // 2
    qb = SB_QBLK
    pcols = 2 * qb
    cols = n_pairs * pcols
    lane = lax.broadcasted_iota(I32, (qb, LANES), 1)
    left = lane < HEAD_DIM
    zero = jnp.zeros((qb, LANES), BF16)
    later_m = _later_matrix(ts, 0)
    krow = lax.broadcasted_iota(I32, (ts, cols), 0)
    qpos = i * qb + (lax.broadcasted_iota(I32, (ts, cols), 1) & (qb - 1))
    top = lax.broadcasted_iota(I32, (LANES, qb), 0) < HEAD_DIM
    pair = [slice(p * LANES, (p + 1) * LANES) for p in range(n_pairs)]

    q2 = []
    for p in range(n_pairs):
        chunk = qb_ref[:, pair[p]]
        q2.append(jnp.concatenate([jnp.where(left, chunk, zero), jnp.where(left, zero, chunk)], axis=0))
    acc_ref[...] = jnp.zeros_like(acc_ref)
    carry_ref[...] = jnp.zeros_like(carry_ref)

    def cond(state):
        j, go = state
        return jnp.logical_and(j >= 0, go)

    def logits(j):
        off = pl.multiple_of(jnp.maximum(j, 0) * ts, ts)
        return jnp.concatenate([_dot_nt(kb_ref[pl.ds(off, ts), pair[p]], q2[p]) for p in range(n_pairs)],
                               axis=1)

    def body(state):
        j, _ = state
        z = z_ref[...]
        z_next = logits(j - 1)
        strict = (j * ts + krow) < qpos
        a, new_carry = _sb_tile(z, strict, later_m, carry_ref[0:1, :], 0)
        ab = a.astype(BF16)
        for p in range(n_pairs):
            acc_ref[p] += _dot(vbt_ref[0, j, pair[p], :], ab[:, p * pcols:(p + 1) * pcols])
        carry_ref[...] = jnp.broadcast_to(new_carry, carry_ref.shape)
        z_ref[...] = z_next
        return j - 1, jnp.max(new_carry) >= SB_STOP

    j0 = (i * qb + qb) // ts - 1
    z_ref[...] = logits(j0)
    lax.while_loop(cond, body, (j0, jnp.bool_(True)))
    for p in range(n_pairs):
        acc = acc_ref[p]
        ob_ref[:, pair[p]] = jnp.where(top, acc[:, 0:qb], acc[:, qb:pcols]).T


def _sb_prompt(qb, kbb, vbt, batch, seq):
    nq = seq // SB_QBLK
    qspec = pl.BlockSpec((SB_QBLK, Q_B_W), lambda b, i: (b * nq + i, 0))
    return pl.pallas_call(
        _sb_prompt_kernel,
        grid=(batch, nq),
        in_specs=[qspec, pl.BlockSpec((seq, Q_B_W), lambda b, i: (b, 0)),
                  pl.BlockSpec((1, seq // KBLK, Q_B_W, KBLK), lambda b, i: (b, 0, 0, 0))],
        out_specs=qspec,
        out_shape=jax.ShapeDtypeStruct((batch * seq, Q_B_W), F32),
        scratch_shapes=[pltpu.VMEM((N_HEADS_B // 2, LANES, 2 * SB_QBLK), F32),
                        pltpu.VMEM((SUBLANES, N_HEADS_B * SB_QBLK), F32),
                        pltpu.VMEM((SB_TS, N_HEADS_B * SB_QBLK), F32)],
        compiler_params=_cparams("parallel", "arbitrary"),
        name="sb_prompt",
    )(qb, kbb, vbt)


TSLOTS = 8
SROWS = TSLOTS * N_HEADS_A


def _largest_divisor(n, cap):
    return max(x for x in range(1, cap + 1) if n % x == 0)


def _page_copies(pool_hbm, pt_ref, seq, buf, slot, sem, n_pages, start):
    def body(p, _):
        cp = pltpu.make_async_copy(pool_hbm.at[pt_ref[seq, p]], buf.at[slot, p], sem.at[slot])
        if start:
            cp.start()
        else:
            cp.wait()
        return 0

    lax.fori_loop(0, n_pages, body, 0)


def _page_ring(pool_hbm, pt_ref, buf, sem, n_pages):
    b = pl.program_id(0)
    slot = b & 1

    @pl.when(b == 0)
    def _():
        _page_copies(pool_hbm, pt_ref, 0, buf, 0, sem, n_pages, True)

    @pl.when(b + 1 < pl.num_programs(0))
    def _():
        _page_copies(pool_hbm, pt_ref, b + 1, buf, 1 - slot, sem, n_pages, True)

    _page_copies(pool_hbm, pt_ref, b, buf, slot, sem, n_pages, False)
    return slot


def _dsa_sample_scores_kernel(n_pages, page, ppt, pt_ref, qi_ref, w_ref, kin_ref, kidx_hbm,
                              key_ref, buf, sem):
    slot = _page_ring(kidx_hbm, pt_ref, buf, sem, n_pages)
    ts = ppt * page
    qi = qi_ref[0]
    w = w_ref[0]

    def scores(kt):
        wrel = w * jnp.maximum(_dot(qi, kt), 0.0)
        return jnp.sum(wrel.reshape(TSLOTS, N_IDX_HEADS, kt.shape[1]), axis=1)

    for t in range(n_pages // ppt):
        kt = jnp.concatenate([buf[slot, t * ppt + q] for q in range(ppt)], axis=1).astype(BF16)
        key_ref[0, :, t * ts:(t + 1) * ts] = scores(kt)
    past = n_pages * page
    qslot = lax.broadcasted_iota(I32, (TSLOTS, page), 0)
    u = lax.broadcasted_iota(I32, (TSLOTS, page), 1)
    key_ref[0, :, past:past + page] = jnp.where(u <= qslot, scores(kin_ref[0]), -jnp.inf)


def _dsa_sample_select_kernel(k_sel, idx_bits, n_tiles, ts, key_ref, tau_ref, taui_ref):
    tau, tau_i = _topk_threshold(key_ref, n_tiles, ts, 1, key_ref.shape[0], k_sel, idx_bits)
    tau_ref[...] = jnp.broadcast_to(tau, tau_ref.shape)
    taui_ref[...] = jnp.broadcast_to(tau_i, taui_ref.shape)


def _dsa_sample_attend_kernel(n_pages, page, ppt, pt_ref, qa_ref, key_ref, tau_ref, taui_ref, kvn_ref, kva_hbm,
                              o_ref, buf, sem, s_ref):
    slot = _page_ring(kva_hbm, pt_ref, buf, sem, n_pages)
    ts = ppt * page
    past = n_pages * page
    kfeat = N_KV_A * HEAD_DIM
    qa = qa_ref[0]
    tau = tau_ref[:, 0:1]
    tau_i = taui_ref[:, 0:1]

    n_t = n_pages // ppt

    def masked_scores(kt, kpos0, k):
        n = kt.shape[1]
        kpos = kpos0 + lax.broadcasted_iota(I32, (TSLOTS, n), 1)
        qpos = past + lax.broadcasted_iota(I32, (TSLOTS, n), 0)
        sel = ((kt > tau) | ((kt == tau) & (kpos <= tau_i))) & (kpos <= qpos)
        bias = jnp.where(sel, 0.0, -jnp.inf)
        bias_rows = jnp.concatenate(
            [jnp.broadcast_to(bias[t:t + 1, :], (N_HEADS_A, n)) for t in range(TSLOTS)], axis=0)
        return _dot(qa, k) + bias_rows

    def pages(t, f0):
        return jnp.concatenate([buf[slot, t * ppt + q, f0:f0 + kfeat, :] for q in range(ppt)],
                               axis=1).astype(BF16)

    mx = jnp.full((SROWS, ts), -jnp.inf, F32)
    for t in range(n_t):
        s = masked_scores(key_ref[0, :, t * ts:(t + 1) * ts], t * ts, pages(t, 0))
        s_ref[:, t * ts:(t + 1) * ts] = s
        mx = jnp.maximum(mx, s)
    s_new = masked_scores(key_ref[0, :, past:past + page], past, kvn_ref[0, 0:kfeat, :])
    m = jnp.maximum(jnp.max(mx, axis=1, keepdims=True), jnp.max(s_new, axis=1, keepdims=True))
    m = jnp.where(m == -jnp.inf, 0.0, m)

    p_new = jnp.exp2(s_new - m)
    acc = _dot_nt(p_new.astype(BF16), kvn_ref[0, kfeat:2 * kfeat, :])
    psum = jnp.zeros((SROWS, ts), F32)
    for t in range(n_t):
        p = jnp.exp2(s_ref[:, t * ts:(t + 1) * ts] - m)
        psum = psum + p
        acc = acc + _dot_nt(p.astype(BF16), pages(t, kfeat))
    l = jnp.sum(psum, axis=1, keepdims=True) + jnp.sum(p_new, axis=1, keepdims=True)
    o_ref[0] = acc / jnp.where(l > 0.0, l, 1.0)


def _dsa_sample(qi_st, w_st, qa_st, kidx_pool, kva_pool, kin_t, kvn_t, page_table, dec_seq):
    n_seq, n_pages = page_table.shape
    page = kidx_pool.shape[2]
    nk = (n_pages + 1) * page
    ppt = _largest_divisor(n_pages, 8)
    sel_ts = _largest_divisor(n_pages + 1, 8) * page
    k_sel = max(1, min(TOPK_MAX, (n_pages * page + dec_seq) // 4))
    idx_bits = max(1, int(nk - 1).bit_length())
    seq_spec = lambda shp: pl.BlockSpec((1,) + shp, lambda b, pt: (b, 0, 0))
    hbm_spec = pl.BlockSpec(memory_space=pl.ANY)
    ring = lambda feat: [pltpu.VMEM((2, n_pages, feat, page), F32), pltpu.SemaphoreType.DMA((2,))]

    keys = pl.pallas_call(
        functools.partial(_dsa_sample_scores_kernel, n_pages, page, ppt),
        grid_spec=pltpu.PrefetchScalarGridSpec(
            num_scalar_prefetch=1, grid=(n_seq,),
            in_specs=[seq_spec((SROWS, IDX_DIM)), seq_spec((SROWS, 1)), seq_spec((IDX_DIM, page)), hbm_spec],
            out_specs=seq_spec((TSLOTS, nk)),
            scratch_shapes=ring(IDX_DIM)),
        out_shape=jax.ShapeDtypeStruct((n_seq, TSLOTS, nk), F32),
        compiler_params=_cparams("arbitrary"),
        name="dsa_sample_scores",
    )(page_table, qi_st, w_st, kin_t, kidx_pool)

    rows = n_seq * dec_seq
    tau, tau_i = pl.pallas_call(
        functools.partial(_dsa_sample_select_kernel, k_sel, idx_bits, nk // sel_ts, sel_ts),
        grid=(1,),
        in_specs=[_const_spec((rows, nk))],
        out_specs=[pl.BlockSpec((rows, LANES), lambda i: (0, 0))] * 2,
        out_shape=[jax.ShapeDtypeStruct((rows, LANES), F32), jax.ShapeDtypeStruct((rows, LANES), I32)],
        compiler_params=_cparams("arbitrary"),
        name="dsa_sample_select",
    )(keys[:, :dec_seq].reshape(rows, nk))
    pad_slots = lambda a: jnp.pad(a.reshape(n_seq, dec_seq, LANES),
                                  ((0, 0), (0, TSLOTS - dec_seq), (0, 0))).reshape(n_seq * TSLOTS, LANES)
    tau, tau_i = pad_slots(tau), pad_slots(tau_i)

    row_spec = pl.BlockSpec((TSLOTS, LANES), lambda b, pt: (b, 0))
    return pl.pallas_call(
        functools.partial(_dsa_sample_attend_kernel, n_pages, page, ppt),
        grid_spec=pltpu.PrefetchScalarGridSpec(
            num_scalar_prefetch=1, grid=(n_seq,),
            in_specs=[seq_spec((SROWS, LANES)), seq_spec((TSLOTS, nk)), row_spec, row_spec,
                      seq_spec((KV_A_W, page)), hbm_spec],
            out_specs=seq_spec((SROWS, LANES)),
            scratch_shapes=ring(KV_A_W) + [pltpu.VMEM((SROWS, n_pages * page), F32)]),
        out_shape=jax.ShapeDtypeStruct((n_seq, SROWS, LANES), F32),
        compiler_params=_cparams("arbitrary"),
        name="dsa_sample_attend",
    )(page_table, qa_st, keys, tau, tau_i, kvn_t, kva_pool)


def _sb_sample_kernel(n_pages, page, pt_ref, q_ref, kvn_ref, kvb_hbm, o_ref, newest, newest_sem, buf, sem,
                      acc_ref, carry_ref):
    b = pl.program_id(0)
    q = q_ref[0]
    later_m = _later_matrix(page, 1)
    last = n_pages - 1
    walk = last - 1

    def newest_copy(seq):
        slot = seq & 1
        return pltpu.make_async_copy(kvb_hbm.at[pt_ref[seq, last]], newest.at[slot], newest_sem.at[slot])

    def page_copy(pg):
        slot = (walk - pg) & 1
        return pltpu.make_async_copy(kvb_hbm.at[pt_ref[b, pg]], buf.at[slot], sem.at[slot])

    def tile(kt, vt, strict):
        a, new_carry = _sb_tile(_dot(q, kt), strict, later_m, carry_ref[:, 0:1], 1)
        acc_ref[...] += _dot_nt(a.astype(BF16), vt)
        carry_ref[...] = jnp.broadcast_to(new_carry, carry_ref.shape)
        return jnp.max(new_carry) >= SB_STOP

    @pl.when(b == 0)
    def _():
        newest_copy(b).start()

    @pl.when(b + 1 < pl.num_programs(0))
    def _():
        newest_copy(b + 1).start()

    if walk >= 0:
        page_copy(walk).start()
    acc_ref[...] = jnp.zeros_like(acc_ref)
    carry_ref[...] = jnp.zeros_like(carry_ref)
    qslot = lax.broadcasted_iota(I32, (SROWS, page), 0) // N_HEADS_B
    u = lax.broadcasted_iota(I32, (SROWS, page), 1)
    tile(kvn_ref[0, 0:Q_B_W, :], kvn_ref[0, Q_B_W:2 * Q_B_W, :], u < qslot)
    newest_copy(b).wait()
    ns = b & 1
    go = tile(newest[ns, 0:Q_B_W, :].astype(BF16), newest[ns, Q_B_W:2 * Q_B_W, :].astype(BF16), None)

    if walk >= 0:
        def cond(state):
            pg, go = state
            return jnp.logical_and(pg >= 0, go)

        def body(state):
            pg, _ = state
            page_copy(pg).wait()

            @pl.when(pg > 0)
            def _():
                page_copy(pg - 1).start()

            slot = (walk - pg) & 1
            go = tile(buf[slot, 0:Q_B_W, :].astype(BF16), buf[slot, Q_B_W:2 * Q_B_W, :].astype(BF16), None)
            return pg - 1, go

        pg_end, _ = lax.while_loop(cond, body, (jnp.int32(walk), go))

        @pl.when(pg_end >= 0)
        def _():
            page_copy(pg_end).wait()

    r = lax.broadcasted_iota(I32, (SROWS, Q_B_W), 0)
    c = lax.broadcasted_iota(I32, (SROWS, Q_B_W), 1)
    own = (r % N_HEADS_B) == (c // HEAD_DIM)
    diag = jnp.where(own, acc_ref[...], 0.0)
    o_ref[0] = jnp.sum(diag.reshape(TSLOTS, N_HEADS_B, Q_B_W), axis=1)


def _sb_sample(qb_bd, kvb_pool, kvn_t, page_table):
    n_seq, n_pages = page_table.shape
    page = kvb_pool.shape[2]
    seq_spec = lambda shp: pl.BlockSpec((1,) + shp, lambda b, pt: (b, 0, 0))
    return pl.pallas_call(
        functools.partial(_sb_sample_kernel, n_pages, page),
        grid_spec=pltpu.PrefetchScalarGridSpec(
            num_scalar_prefetch=1, grid=(n_seq,),
            in_specs=[seq_spec((SROWS, Q_B_W)), seq_spec((KV_B_W, page)), pl.BlockSpec(memory_space=pl.ANY)],
            out_specs=seq_spec((TSLOTS, Q_B_W)),
            scratch_shapes=[pltpu.VMEM((2, KV_B_W, page), F32), pltpu.SemaphoreType.DMA((2,)),
                            pltpu.VMEM((2, KV_B_W, page), F32), pltpu.SemaphoreType.DMA((2,)),
                            pltpu.VMEM((SROWS, Q_B_W), F32), pltpu.VMEM((SROWS, LANES), F32)]),
        out_shape=jax.ShapeDtypeStruct((n_seq, TSLOTS, Q_B_W), F32),
        compiler_params=_cparams("arbitrary"),
        name="sb_sample",
    )(page_table, qb_bd, kvn_t, kvb_pool)


def _out_kernel(oa_ref, ob_ref, gate_ref, x_ref, wa_ref, wb_ref, wo_ref, nf_ref,
                wg_ref, wu_ref, wd_ref, nfin_ref, y_ref):
    ba = _dot(oa_ref[...].astype(BF16), wa_ref[...])
    bb = _dot(ob_ref[...].astype(BF16), wb_ref[...])
    merged = (gate_ref[:, 0:D_MODEL].astype(F32) * ba
              + gate_ref[:, D_MODEL:2 * D_MODEL].astype(F32) * bb)
    h1 = x_ref[...] + _dot(merged.astype(BF16), wo_ref[...])
    ms = jnp.mean(h1 * h1, axis=-1, keepdims=True)
    hn = ((h1 * lax.rsqrt(ms + RMS_EPS)) * nf_ref[...]).astype(BF16)
    acc = jnp.zeros_like(h1)
    for c in range(D_FF // FFN_CHUNK):
        sl = slice(c * FFN_CHUNK, (c + 1) * FFN_CHUNK)
        gg = _dot(hn, wg_ref[:, sl])
        uu = _dot(hn, wu_ref[:, sl])
        act = (gg * (1.0 / (1.0 + jnp.exp(-gg)))) * uu
        acc = acc + _dot(act.astype(BF16), wd_ref[sl, :])
    h2 = h1 + acc
    ms2 = jnp.mean(h2 * h2, axis=-1, keepdims=True)
    y_ref[...] = (h2 * lax.rsqrt(ms2 + RMS_EPS)) * nfin_ref[...]


def _out(oa, ob, gates, x2d, wts, tm):
    rows = x2d.shape[0]
    row_spec = lambda n: pl.BlockSpec((tm, n), lambda i: (i, 0))
    return pl.pallas_call(
        _out_kernel,
        grid=(rows // tm,),
        in_specs=[row_spec(Q_A_W), row_spec(Q_B_W), row_spec(2 * D_MODEL), row_spec(D_MODEL)]
                 + [_const_spec(w.shape) for w in wts],
        out_specs=row_spec(D_MODEL),
        out_shape=jax.ShapeDtypeStruct((rows, D_MODEL), F32),
        compiler_params=_cparams("parallel"),
        name="out",
    )(oa, ob, gates, x2d, *wts)


def _rope_tables(pos):
    inv_freq = ROPE_THETA ** (-jnp.arange(HALF, dtype=F32) / HALF)
    ang = pos.astype(F32)[:, None] * inv_freq[None, :]
    cos, sin = jnp.cos(ang), jnp.sin(ang)
    cs = jnp.concatenate([cos, cos, cos, cos], axis=1)
    sn = jnp.concatenate([-sin, sin, -sin, sin], axis=1)
    return cs, sn, cos.T, sin.T


def _pack_w_in(w):
    d = w.shape[0]
    o = np.cumsum([0, Q_A_W, KV_A_W, Q_I_W, IDX_DIM, N_IDX_HEADS, Q_B_W, KV_B_W, 2 * D_MODEL])
    kw = N_KV_A * HEAD_DIM
    qa = w[:, o[0]:o[1]].reshape(d, N_HEADS_A, HEAD_DIM)[:, np.array(QA_HEAD_ORDER), :].reshape(d, Q_A_W)
    ka, va = w[:, o[1]:o[1] + kw], w[:, o[1] + kw:o[2]]
    qi, ki, wi, qb = w[:, o[2]:o[3]], w[:, o[3]:o[4]], w[:, o[4]:o[5]], w[:, o[5]:o[6]]
    kb, vb = w[:, o[6]:o[6] + Q_B_W], w[:, o[6] + Q_B_W:o[7]]
    g = w[:, o[7]:o[8]]
    w_tok = jnp.concatenate([qa, qi, qb, g], axis=1).astype(BF16)
    wi_pad = jnp.pad(wi, ((0, 0), (0, HEAD_DIM - N_IDX_HEADS)))
    w_feat = jnp.concatenate([ka, va, ki, wi_pad, kb, vb], axis=1).T.astype(BF16)
    return w_tok, w_feat


def kernel(x_prompt, x_sample, cache_kv_a, cache_k_idx, cache_kv_b, page_table, w_in, w_br_a, w_br_b,
           w_o, norm_attn, norm_ffn, w_ffn_gate, w_ffn_up, w_ffn_down, norm_final):
    depth = w_in.shape[0]
    assert depth == 1, "single-layer step"
    batch, seq, d = x_prompt.shape
    assert d == D_MODEL and seq % ROW_TILE == 0
    layer = 0

    w_tok, w_feat = _pack_w_in(w_in[layer])
    wa = w_br_a[layer].reshape(N_HEADS_A, HEAD_DIM, D_MODEL)[np.array(QA_HEAD_ORDER)].reshape(Q_A_W, D_MODEL)
    out_wts = (wa.astype(BF16), w_br_b[layer].astype(BF16), w_o[layer].astype(BF16),
               norm_ffn[layer].reshape(1, D_MODEL),
               w_ffn_gate[layer].astype(BF16), w_ffn_up[layer].astype(BF16), w_ffn_down[layer].astype(BF16),
               norm_final.reshape(1, D_MODEL))
    gain = norm_attn[layer].reshape(1, D_MODEL)

    xp = x_prompt.reshape(batch * seq, D_MODEL)
    (kva_p, kidx_p, kvb_p, qa, qi, wit, qb, gates, kab, kib, kbb, vat, vbt) = _project(
        xp, gain, w_tok, w_feat, *_rope_tables(jnp.arange(seq, dtype=I32)), batch, ROW_TILE)
    oa = _dsa_prompt(qi, wit, qa, kib, kab, vat, batch, seq)
    ob = _sb_prompt(qb, kbb, vbt, batch, seq)
    y_prompt = _out(oa, ob, gates, xp, out_wts, ROW_TILE).reshape(batch, seq, D_MODEL)

    new_kv_a_p = kva_p.reshape(depth, batch, 2, N_KV_A, HEAD_DIM, seq).transpose(0, 1, 5, 2, 3, 4)
    new_k_idx_p = kidx_p.reshape(depth, batch, IDX_DIM, seq).transpose(0, 1, 3, 2)
    new_kv_b_p = kvb_p.reshape(depth, batch, 2, N_HEADS_B, HEAD_DIM, seq).transpose(0, 1, 5, 2, 3, 4)

    n_seq, dec_seq, _ = x_sample.shape
    n_pages = page_table.shape[1]
    n_pool, page = cache_kv_a.shape[1], cache_kv_a.shape[2]
    past = n_pages * page
    rows_s = n_seq * dec_seq
    assert dec_seq <= TSLOTS and rows_s % KBLK == 0 and page == LANES
    xs = x_sample.reshape(rows_s, D_MODEL)
    cs_s, sn_s, csf_s, snf_s = _rope_tables(past + jnp.arange(dec_seq, dtype=I32))
    tabs_s = (jnp.tile(cs_s, (n_seq, 1)), jnp.tile(sn_s, (n_seq, 1)),
              jnp.tile(csf_s, (1, n_seq)), jnp.tile(snf_s, (1, n_seq)))
    (kva_s, kidx_s, kvb_s, qa_s, qi_s, wit_s, qb_s, gates_s, _, _, _, _, _) = _project(
        xs, gain, w_tok, w_feat, *tabs_s, 1, rows_s)

    def slots(a):
        return jnp.pad(a, ((0, 0), (0, TSLOTS - dec_seq)) + ((0, 0),) * (a.ndim - 2))

    def new_tile(a):
        t = a[0].reshape(a.shape[1], n_seq, dec_seq).transpose(1, 0, 2)
        return jnp.pad(t, ((0, 0), (0, 0), (0, page - dec_seq))).astype(BF16)

    qi_st = slots(qi_s.reshape(n_seq, dec_seq, N_IDX_HEADS, IDX_DIM)).reshape(n_seq, SROWS, IDX_DIM)
    w_st = slots(wit_s[0].T.reshape(n_seq, dec_seq, N_IDX_HEADS)).reshape(n_seq, SROWS, 1)
    qa5 = qa_s.reshape(n_seq, dec_seq, GROUP_A, N_KV_A, HEAD_DIM).transpose(0, 1, 3, 2, 4)
    eye2 = jnp.eye(N_KV_A, dtype=qa_s.dtype)
    qa_st = qa5[:, :, :, :, None, :] * eye2[None, None, :, None, :, None]
    qa_st = slots(qa_st.reshape(n_seq, dec_seq, N_HEADS_A, LANES)).reshape(n_seq, SROWS, LANES)
    kidx_pool = cache_k_idx[layer].transpose(0, 2, 1)
    kva_pool = cache_kv_a[layer].transpose(0, 2, 3, 4, 1).reshape(n_pool, KV_A_W, page)
    kvb_pool = cache_kv_b[layer].transpose(0, 2, 3, 4, 1).reshape(n_pool, KV_B_W, page)
    o_st = _dsa_sample(qi_st, w_st, qa_st, kidx_pool, kva_pool, new_tile(kidx_s), new_tile(kva_s),
                       page_table, dec_seq)
    o6 = o_st.reshape(n_seq, TSLOTS, N_KV_A, GROUP_A, N_KV_A, HEAD_DIM)[:, :dec_seq]
    oa_s = jnp.stack([o6[:, :, c, :, c, :] for c in range(N_KV_A)], axis=2).reshape(rows_s, Q_A_W)

    qb4 = qb_s.reshape(n_seq, dec_seq, N_HEADS_B, HEAD_DIM)
    eye8 = jnp.eye(N_HEADS_B, dtype=qb_s.dtype)
    qb_bd = qb4[:, :, :, None, :] * eye8[None, None, :, :, None]
    qb_bd = slots(qb_bd.reshape(n_seq, dec_seq, N_HEADS_B, Q_B_W)).reshape(n_seq, SROWS, Q_B_W)
    ob_st = _sb_sample(qb_bd, kvb_pool, new_tile(kvb_s), page_table)
    ob_s = ob_st[:, :dec_seq].reshape(rows_s, Q_B_W)

    out_wts_s = (w_br_a[layer].astype(BF16),) + out_wts[1:]
    y_sample = _out(oa_s, ob_s, gates_s, xs, out_wts_s, rows_s).reshape(n_seq, dec_seq, D_MODEL)

    return (y_prompt, y_sample, new_kv_a_p, new_k_idx_p, new_kv_b_p,
            kva_s[0].T.reshape(depth, n_seq, dec_seq, 2, N_KV_A, HEAD_DIM),
            kidx_s[0].T.reshape(depth, n_seq, dec_seq, IDX_DIM),
            kvb_s[0].T.reshape(depth, n_seq, dec_seq, 2, N_HEADS_B, HEAD_DIM))
```

```python
import functools

import jax
import jax.numpy as jnp
import numpy as np
from jax import lax
from jax.experimental import pallas as pl
from jax.experimental.pallas import tpu as pltpu

F32 = jnp.float32
BF16 = jnp.bfloat16
I32 = jnp.int32

D_MODEL = 1024
HEAD_DIM = 64
N_HEADS_A = 8
N_KV_A = 2
GROUP_A = N_HEADS_A // N_KV_A
N_IDX_HEADS = 8
IDX_DIM = 64
TOPK_MAX = 256
N_HEADS_B = 8
Q_A_W = N_HEADS_A * HEAD_DIM
KV_A_W = 2 * N_KV_A * HEAD_DIM
Q_I_W = N_IDX_HEADS * IDX_DIM
Q_B_W = N_HEADS_B * HEAD_DIM
KV_B_W = 2 * N_HEADS_B * HEAD_DIM
D_FF = -(-8 * D_MODEL // (3 * 256)) * 256
ROPE_THETA = 10000.0
RMS_EPS = 1e-6
QK_SCALE = HEAD_DIM ** -0.5
IDX_SCALE = IDX_DIM ** -0.5
HALF = HEAD_DIM // 2
LOG2_E = 1.4426950408889634

LANES = 128
SUBLANES = 8
QBLK = 256
KBLK = 128
SUM_ROWS = 16
FFN_CHUNK = 256
ROW_TILE = 512

TK_QA = 0
TK_QI = 512
TK_QB = 1024
TK_G = 1536
TK_W = 3584
FM_KA = 0
FM_VA = 128
FM_KI = 256
FM_WI = 320
FM_KB = 384
FM_VB = 896
FM_W = 1408
QA_HEAD_ORDER = (0, 4, 1, 5, 2, 6, 3, 7)

NEGINF_KEY = -2139095041
POSINF_KEY = 0x7F800000
MIN_NORMAL_KEY = 0x00800000
IDX_ALL = 1 << 30
SEARCH_GROUP = 4
SB_STOP = -105.0

VMEM_LIMIT = 60 * 1024 * 1024

NT_DIMS = (((1,), (1,)), ((), ()))


def _cparams(*sem):
    return pltpu.CompilerParams(dimension_semantics=sem, vmem_limit_bytes=VMEM_LIMIT)


def _const_spec(shape):
    nd = len(shape)
    return pl.BlockSpec(shape, lambda *_: (0,) * nd, pipeline_mode=pl.Buffered(1))


def _dot_nt(a, b):
    return lax.dot_general(a, b, NT_DIMS, preferred_element_type=F32)


def _dot(a, b):
    return jnp.dot(a, b, preferred_element_type=F32)


def _project_kernel(x_ref, g_ref, wt_ref, wf_ref, cs_ref, sn_ref, csf_ref, snf_ref,
                    kva_ref, kidx_ref, kvb_ref,
                    qa_ref, qi_ref, wit_ref, qb_ref, gate_ref,
                    kab_ref, kib_ref, kbb_ref, vat_ref, vbt_ref):
    x = x_ref[...]
    ms = jnp.mean(x * x, axis=-1, keepdims=True)
    xn = (x * lax.rsqrt(ms + RMS_EPS)) * g_ref[...]
    xb = xn.astype(BF16)
    tm = x.shape[0]
    cs = cs_ref[...]
    sn = sn_ref[...]
    lane = lax.broadcasted_iota(I32, (tm, LANES), 1)
    first_half = (lane & HALF) == 0

    def mm(c0, n):
        return _dot(xb, wt_ref[:, c0:c0 + n])

    def rope(y):
        partner = jnp.where(first_half, pltpu.roll(y, LANES - HALF, 1), pltpu.roll(y, HALF, 1))
        return y * cs + partner * sn

    ya = mm(TK_QA, Q_A_W)
    for c in range(Q_A_W // LANES):
        sl = slice(c * LANES, (c + 1) * LANES)
        qa_ref[:, sl] = (rope(ya[:, sl]) * (QK_SCALE * LOG2_E)).astype(BF16)
    yi = mm(TK_QI, Q_I_W)
    for c in range(Q_I_W // LANES):
        sl = slice(c * LANES, (c + 1) * LANES)
        qi_ref[:, sl] = rope(yi[:, sl]).astype(BF16)
    qb_ref[...] = (mm(TK_QB, Q_B_W) * QK_SCALE).astype(BF16)
    for c in range(2 * D_MODEL // 512):
        sl = slice(c * 512, (c + 1) * 512)
        y = mm(TK_G + c * 512, 512)
        gate_ref[:, sl] = (1.0 / (1.0 + jnp.exp(-y))).astype(BF16)

    csf = csf_ref[...]
    snf = snf_ref[...]

    def fm(r0, n):
        return _dot_nt(wf_ref[r0:r0 + n, :], xb)

    def rope_fm(y):
        lo, hi = y[0:HALF], y[HALF:HEAD_DIM]
        return jnp.concatenate([lo * csf - hi * snf, hi * csf + lo * snf], axis=0)

    ka = fm(FM_KA, N_KV_A * HEAD_DIM)
    ka = jnp.concatenate([rope_fm(ka[c * HEAD_DIM:(c + 1) * HEAD_DIM]) for c in range(N_KV_A)], axis=0)
    kva_ref[0, 0:N_KV_A * HEAD_DIM, :] = ka
    kab_ref[...] = ka.T.astype(BF16)
    va = fm(FM_VA, N_KV_A * HEAD_DIM)
    kva_ref[0, N_KV_A * HEAD_DIM:2 * N_KV_A * HEAD_DIM, :] = va
    ki = rope_fm(fm(FM_KI, IDX_DIM))
    kidx_ref[0] = ki
    kib_ref[...] = jnp.concatenate([ki, ki], axis=0).T.astype(BF16)
    wi = fm(FM_WI, HEAD_DIM)
    wit_ref[0] = (wi[0:N_IDX_HEADS] * (N_IDX_HEADS ** -0.5)) * IDX_SCALE
    kb = fm(FM_KB, Q_B_W)
    kvb_ref[0, 0:Q_B_W, :] = kb
    kbb_ref[...] = kb.T.astype(BF16)
    vb = fm(FM_VB, Q_B_W)
    kvb_ref[0, Q_B_W:2 * Q_B_W, :] = vb
    for q in range(tm // KBLK):
        sl = slice(q * KBLK, (q + 1) * KBLK)
        vat_ref[0, q] = va[:, sl].astype(BF16)
        vbt_ref[0, q] = vb[:, sl].astype(BF16)


def _project(x2d, gain, w_tok, w_feat, cs, sn, csf, snf, groups, tm):
    rows = x2d.shape[0]
    t = rows // groups
    nb = t // tm
    nkb = tm // KBLK
    row_spec = lambda n: pl.BlockSpec((tm, n), lambda i: (i, 0))
    tab_spec = pl.BlockSpec((tm, LANES), lambda i: (i % nb, 0))
    tabf_spec = pl.BlockSpec((HALF, tm), lambda i: (0, i % nb))
    fm_spec = lambda n: pl.BlockSpec((1, n, tm), lambda i: (i // nb, 0, i % nb))
    blk_spec = lambda n: pl.BlockSpec((1, nkb, n, KBLK), lambda i: (i // nb, i % nb, 0, 0))
    out_specs = [fm_spec(KV_A_W), fm_spec(IDX_DIM), fm_spec(KV_B_W),
                 row_spec(Q_A_W), row_spec(Q_I_W), fm_spec(N_IDX_HEADS), row_spec(Q_B_W), row_spec(2 * D_MODEL),
                 row_spec(LANES), row_spec(LANES), row_spec(Q_B_W),
                 blk_spec(N_KV_A * HEAD_DIM), blk_spec(Q_B_W)]
    fm_shape = lambda n, dt: jax.ShapeDtypeStruct((groups, n, t), dt)
    tk_shape = lambda n, dt: jax.ShapeDtypeStruct((rows, n), dt)
    blk_shape = lambda n: jax.ShapeDtypeStruct((groups, t // KBLK, n, KBLK), BF16)
    out_shape = [fm_shape(KV_A_W, F32), fm_shape(IDX_DIM, F32), fm_shape(KV_B_W, F32),
                 tk_shape(Q_A_W, BF16), tk_shape(Q_I_W, BF16), fm_shape(N_IDX_HEADS, F32),
                 tk_shape(Q_B_W, BF16), tk_shape(2 * D_MODEL, BF16),
                 tk_shape(LANES, BF16), tk_shape(LANES, BF16), tk_shape(Q_B_W, BF16),
                 blk_shape(N_KV_A * HEAD_DIM), blk_shape(Q_B_W)]
    return pl.pallas_call(
        _project_kernel,
        grid=(rows // tm,),
        in_specs=[row_spec(D_MODEL), _const_spec((1, D_MODEL)), _const_spec((D_MODEL, TK_W)),
                  _const_spec((FM_W, D_MODEL)), tab_spec, tab_spec, tabf_spec, tabf_spec],
        out_specs=out_specs,
        out_shape=out_shape,
        compiler_params=_cparams("parallel"),
        name="project",
    )(x2d, gain, w_tok, w_feat, cs, sn, csf, snf)


def _key_to_float(k):
    return lax.bitcast_convert_type(k ^ ((k >> 31) & jnp.int32(0x7FFFFFFF)), F32)


def _fold_rows(x, op, rows=SUBLANES):
    n = x.shape[0]
    assert n % rows == 0 and (n // rows) & (n // rows - 1) == 0
    while n > rows:
        n //= 2
        x = op(x[:n], x[n:])
    return x


def _count(key_ref, n_tiles, ts, key_axis, width, pred):
    if key_axis == 1:
        def body(j, acc):
            off = pl.multiple_of(j * ts, ts)
            kt = key_ref[:, pl.ds(off, ts)]
            kpos = j * ts + lax.broadcasted_iota(I32, (width, ts), 1)
            m = jnp.where(pred(kt, kpos, lambda v: v), 1.0, 0.0)
            s = m[:, 0:LANES]
            for c in range(1, ts // LANES):
                s = s + m[:, c * LANES:(c + 1) * LANES]
            return acc + s

        acc = lax.fori_loop(0, n_tiles, body, jnp.zeros((width, LANES), F32))
        return jnp.sum(acc, axis=1, keepdims=True)

    def body(j, acc):
        off = pl.multiple_of(j * ts, ts)
        kpos = j * ts + lax.broadcasted_iota(I32, (ts, LANES), 0)
        parts = []
        for g in range(width // LANES):
            gs = slice(g * LANES, (g + 1) * LANES)
            hit = pred(key_ref[pl.ds(off, ts), gs], kpos, lambda v, gs=gs: v[:, gs])
            parts.append(_fold_rows(jnp.where(hit, 1.0, 0.0), jnp.add))
        return acc + jnp.concatenate(parts, axis=1)

    acc = lax.fori_loop(0, n_tiles, body, jnp.zeros((SUBLANES, width), F32))
    return jnp.sum(acc, axis=0, keepdims=True)


def _topk_threshold(key_ref, n_tiles, ts, key_axis, width, k_sel, idx_bits):
    k_f = float(k_sel)
    vec = (width, 1) if key_axis == 1 else (1, width)
    count = functools.partial(_count, key_ref, n_tiles, ts, key_axis, width)

    def as_float(k):
        zero_band = (k >= -MIN_NORMAL_KEY) & (k < MIN_NORMAL_KEY)
        return _key_to_float(jnp.where(zero_band, 0, k))

    def bis(b, r, settled):
        cand = r + lax.shift_left(jnp.int32(1), jnp.int32(31) - b)
        valid = (cand >= NEGINF_KEY) & (cand <= POSINF_KEY)
        cand_f = as_float(jnp.where(valid, cand, 0))
        cnt = count(lambda st, kp, q: st >= q(cand_f))
        r = jnp.where(valid & (cnt >= k_f) & (settled == 0), cand, r)
        return r, jnp.where(valid & (cnt == k_f), 1, settled)

    def group(state):
        g, r, settled, _ = state
        for k in range(SEARCH_GROUP):
            r, settled = bis(g * SEARCH_GROUP + k, r, settled)
        return g + 1, r, settled, jnp.min(settled.astype(F32)) > 0.5

    def more(state):
        g, _, _, done = state
        return jnp.logical_and(g < 32 // SEARCH_GROUP, jnp.logical_not(done))

    init = (jnp.int32(0), jnp.full(vec, NEGINF_KEY, I32), jnp.zeros(vec, I32), jnp.bool_(False))
    tau = as_float(lax.while_loop(more, group, init)[1])
    need = k_f - count(lambda st, kp, q: st > q(tau))
    tie = (count(lambda st, kp, q: st == q(tau)) > need) & (tau > -jnp.inf)

    def resolve():
        def ib(b, x):
            c = x + lax.shift_left(jnp.int32(1), jnp.int32(idx_bits - 1) - b)
            cnt = count(lambda st, kp, q: (st == q(tau)) & (kp < q(c)))
            return jnp.where(cnt < need, c, x)

        x = lax.fori_loop(0, idx_bits, ib, jnp.zeros(vec, I32))
        return jnp.where(tie, x, IDX_ALL)

    any_tie = jnp.max(jnp.where(tie, 1.0, 0.0)) > 0.5
    tau_i = lax.cond(any_tie, resolve, lambda: jnp.full(vec, IDX_ALL, I32))
    return tau, tau_i


DSA_TS = 256


def _dsa_prompt_kernel(k_sel, idx_bits, qi_ref, wit_ref, qa_ref, ki_ref, ka_ref, vat_ref,
                       oa_ref, key_ref, rel0_ref, rel1_ref, s0_ref, s1_ref, p0_ref, p1_ref):
    i = pl.program_id(1)
    ts = DSA_TS
    n_tiles = (i * QBLK + QBLK + ts - 1) // ts
    lane = lax.broadcasted_iota(I32, (QBLK, LANES), 1)
    left = lane < HEAD_DIM
    krow = lax.broadcasted_iota(I32, (ts, QBLK), 0)
    qpos = i * QBLK + lax.broadcasted_iota(I32, (ts, QBLK), 1)

    zero = jnp.zeros((QBLK, LANES), BF16)
    qi_parts = []
    for p in range(Q_I_W // LANES):
        chunk = qi_ref[:, p * LANES:(p + 1) * LANES]
        qi_parts.append(jnp.where(left, chunk, zero))
        qi_parts.append(jnp.where(left, zero, chunk))
    qi_all = jnp.concatenate(qi_parts, axis=0)
    wt = wit_ref[0]

    last = n_tiles - 1
    n_pairs = (n_tiles + 1) // 2

    def tile_off(j):
        return pl.multiple_of(jnp.minimum(j, last) * ts, ts)

    def rel_tile(j):
        return _dot_nt(ki_ref[pl.ds(tile_off(j), ts), :], qi_all)

    def score_step(j, cur_ref, nxt_ref):
        rel = cur_ref[...]
        nxt_ref[...] = rel_tile(j + 1)
        sc = jnp.zeros((ts, QBLK), F32)
        for h in range(N_IDX_HEADS):
            sc = sc + wt[h:h + 1, :] * jnp.maximum(rel[:, h * QBLK:(h + 1) * QBLK], 0.0)
        sc = jnp.where(j * ts + krow <= qpos, sc, -jnp.inf)
        key_ref[pl.ds(pl.multiple_of(j * ts, ts), ts), :] = sc

    def score_pair(jj, _):
        score_step(2 * jj, rel0_ref, rel1_ref)
        score_step(2 * jj + 1, rel1_ref, rel0_ref)
        return 0

    q4 = []
    for c in range(N_KV_A):
        parts = []
        for g in range(GROUP_A):
            chunk = qa_ref[:, g * LANES:(g + 1) * LANES]
            parts.append(jnp.where(left, chunk, zero) if c == 0 else jnp.where(left, zero, chunk))
        q4.append(jnp.concatenate(parts, axis=0))

    cols4 = GROUP_A * QBLK
    kb_per_tile = ts // KBLK

    def s_tile(j):
        k = ka_ref[pl.ds(tile_off(j), ts), :]
        return [_dot_nt(k, q4[c]) for c in range(N_KV_A)]

    rel0_ref[...] = rel_tile(0)
    first = s_tile(0)
    for c in range(N_KV_A):
        s0_ref[c] = first[c]
    p1_ref[...] = jnp.zeros_like(p1_ref)
    lax.fori_loop(0, n_pairs, score_pair, 0)
    tau, tau_i = _topk_threshold(key_ref, n_pairs, 2 * ts, 0, QBLK, k_sel, idx_bits)

    ones_rows = jnp.ones((SUM_ROWS, ts), BF16)

    def values(j):
        jb = jnp.clip(j, 0, last) * kb_per_tile
        v = jnp.concatenate([vat_ref[0, jb + q] for q in range(kb_per_tile)], axis=1)
        return jnp.concatenate([v, ones_rows], axis=0)

    def pv(vals, p_ref, c):
        return _dot(vals, p_ref[c])

    def attn_step(j, carry, s_cur, s_nxt, p_cur, p_prev):
        kt = key_ref[pl.ds(tile_off(j), ts), :]
        kpos = j * ts + krow
        sel = ((kt > tau) | ((kt == tau) & (kpos <= tau_i))) & (kpos <= qpos)
        bias = jnp.where(sel, 0.0, -jnp.inf)
        bias4 = jnp.concatenate([bias] * GROUP_A, axis=1)
        nxt = s_tile(j + 1)
        vals = values(j - 1)
        out = []
        for c in range(N_KV_A):
            m, acc = carry[c]
            s = s_cur[c] + bias4
            s_nxt[c] = nxt[c]
            m_new = jnp.maximum(m, jnp.max(_fold_rows(s, jnp.maximum), axis=0, keepdims=True))
            m_safe = jnp.where(m_new == -jnp.inf, 0.0, m_new)
            acc = (acc + pv(vals, p_prev, c)) * jnp.exp2(m - m_safe)
            p_cur[c] = jnp.exp2(s - m_safe).astype(BF16)
            out.append((m_new, acc))
        return tuple(out)

    def attn_pair(jj, carry):
        carry = attn_step(2 * jj, carry, s0_ref, s1_ref, p0_ref, p1_ref)
        return attn_step(2 * jj + 1, carry, s1_ref, s0_ref, p1_ref, p0_ref)

    init = tuple((jnp.full((1, cols4), -jnp.inf, F32), jnp.zeros((LANES + SUM_ROWS, cols4), F32))
                 for _ in range(N_KV_A))
    res = lax.fori_loop(0, n_pairs, attn_pair, init)
    vals = values(2 * n_pairs - 1)
    o = []
    for c in range(N_KV_A):
        acc = res[c][1] + pv(vals, p1_ref, c)
        o.append(acc[0:LANES] / acc[LANES:LANES + 1])
    top = lax.broadcasted_iota(I32, (LANES, QBLK), 0) < HEAD_DIM
    for g in range(GROUP_A):
        cs = slice(g * QBLK, (g + 1) * QBLK)
        oa_ref[:, g * LANES:(g + 1) * LANES] = jnp.where(top, o[0][:, cs], o[1][:, cs]).T


def _dsa_prompt(qi, wit, qa, kib, kab, vat, batch, seq):
    nq = seq // QBLK
    assert seq % (2 * DSA_TS) == 0, "key tiles are visited in pairs"
    k_sel = max(1, min(TOPK_MAX, seq // 4))
    idx_bits = max(1, int(seq - 1).bit_length())
    qspec = lambda n: pl.BlockSpec((QBLK, n), lambda b, i: (b * nq + i, 0))
    kspec = pl.BlockSpec((seq, LANES), lambda b, i: (b, 0))
    return pl.pallas_call(
        functools.partial(_dsa_prompt_kernel, k_sel, idx_bits),
        grid=(batch, nq),
        in_specs=[qspec(Q_I_W), pl.BlockSpec((1, N_IDX_HEADS, QBLK), lambda b, i: (b, 0, i)), qspec(Q_A_W),
                  kspec, kspec,
                  pl.BlockSpec((1, seq // KBLK, N_KV_A * HEAD_DIM, KBLK), lambda b, i: (b, 0, 0, 0))],
        out_specs=qspec(Q_A_W),
        out_shape=jax.ShapeDtypeStruct((batch * seq, Q_A_W), F32),
        scratch_shapes=[pltpu.VMEM((seq, QBLK), F32)]
                       + [pltpu.VMEM((DSA_TS, N_IDX_HEADS * QBLK), F32)] * 2
                       + [pltpu.VMEM((N_KV_A, DSA_TS, GROUP_A * QBLK), F32)] * 2
                       + [pltpu.VMEM((N_KV_A, DSA_TS, GROUP_A * QBLK), BF16)] * 2,
        compiler_params=_cparams("parallel", "arbitrary"),
        name="dsa_prompt",
    )(qi, wit, qa, kib, kab, vat)


def _later_matrix(n, key_axis):
    r = lax.broadcasted_iota(I32, (n, n), 0)
    c = lax.broadcasted_iota(I32, (n, n), 1)
    return jnp.where(r > c if key_axis == 1 else c > r, 1.0, 0.0).astype(BF16)


def _sb_tile(z, strict, later_m, carry, key_axis):
    soft = jnp.log(1.0 + jnp.exp(-jnp.abs(z)))
    log_beta = jnp.minimum(z, 0.0) - soft
    log_rest = log_beta - z
    if strict is not None:
        log_rest = jnp.where(strict, log_rest, 0.0)
    hi = log_rest.astype(BF16)
    lo = (log_rest - hi.astype(F32)).astype(BF16)
    if key_axis == 1:
        later = _dot(hi, later_m) + _dot(lo, later_m)
        first_l, first_r = later[:, 0:1], log_rest[:, 0:1]
    else:
        later = _dot(later_m, hi) + _dot(later_m, lo)
        first_l, first_r = later[0:1, :], log_rest[0:1, :]
    a = jnp.exp(log_beta + later + carry)
    if strict is not None:
        a = jnp.where(strict, a, 0.0)
    return a, carry + first_l + first_r


SB_TS = 128
SB_QBLK = 128


def _sb_prompt_kernel(qb_ref, kb_ref, vbt_ref, ob_ref, acc_ref, carry_ref, z_ref):
    i = pl.program_id(1)
    ts = SB_TS
    n_pairs = N_HEADS_B // 2
    qb = SB_QBLK
    pcols = 2 * qb
    cols = n_pairs * pcols
    lane = lax.broadcasted_iota(I32, (qb, LANES), 1)
    left = lane < HEAD_DIM
    zero = jnp.zeros((qb, LANES), BF16)
    later_m = _later_matrix(ts, 0)
    krow = lax.broadcasted_iota(I32, (ts, cols), 0)
    qpos = i * qb + (lax.broadcasted_iota(I32, (ts, cols), 1) & (qb - 1))
    top = lax.broadcasted_iota(I32, (LANES, qb), 0) < HEAD_DIM
    pair = [slice(p * LANES, (p + 1) * LANES) for p in range(n_pairs)]

    q2 = []
    for p in range(n_pairs):
        chunk = qb_ref[:, pair[p]]
        q2.append(jnp.concatenate([jnp.where(left, chunk, zero), jnp.where(left, zero, chunk)], axis=0))
    acc_ref[...] = jnp.zeros_like(acc_ref)
    carry_ref[...] = jnp.zeros_like(carry_ref)

    def cond(state):
        j, go = state
        return jnp.logical_and(j >= 0, go)

    def logits(j):
        off = pl.multiple_of(jnp.maximum(j, 0) * ts, ts)
        return jnp.concatenate([_dot_nt(kb_ref[pl.ds(off, ts), pair[p]], q2[p]) for p in range(n_pairs)],
                               axis=1)

    def body(state):
        j, _ = state
        z = z_ref[...]
        z_next = logits(j - 1)
        strict = (j * ts + krow) < qpos
        a, new_carry = _sb_tile(z, strict, later_m, carry_ref[0:1, :], 0)
        ab = a.astype(BF16)
        for p in range(n_pairs):
            acc_ref[p] += _dot(vbt_ref[0, j, pair[p], :], ab[:, p * pcols:(p + 1) * pcols])
        carry_ref[...] = jnp.broadcast_to(new_carry, carry_ref.shape)
        z_ref[...] = z_next
        return j - 1, jnp.max(new_carry) >= SB_STOP

    j0 = (i * qb + qb) // ts - 1
    z_ref[...] = logits(j0)
    lax.while_loop(cond, body, (j0, jnp.bool_(True)))
    for p in range(n_pairs):
        acc = acc_ref[p]
        ob_ref[:, pair[p]] = jnp.where(top, acc[:, 0:qb], acc[:, qb:pcols]).T


def _sb_prompt(qb, kbb, vbt, batch, seq):
    nq = seq // SB_QBLK
    qspec = pl.BlockSpec((SB_QBLK, Q_B_W), lambda b, i: (b * nq + i, 0))
    return pl.pallas_call(
        _sb_prompt_kernel,
        grid=(batch, nq),
        in_specs=[qspec, pl.BlockSpec((seq, Q_B_W), lambda b, i: (b, 0)),
                  pl.BlockSpec((1, seq // KBLK, Q_B_W, KBLK), lambda b, i: (b, 0, 0, 0))],
        out_specs=qspec,
        out_shape=jax.ShapeDtypeStruct((batch * seq, Q_B_W), F32),
        scratch_shapes=[pltpu.VMEM((N_HEADS_B // 2, LANES, 2 * SB_QBLK), F32),
                        pltpu.VMEM((SUBLANES, N_HEADS_B * SB_QBLK), F32),
                        pltpu.VMEM((SB_TS, N_HEADS_B * SB_QBLK), F32)],
        compiler_params=_cparams("parallel", "arbitrary"),
        name="sb_prompt",
    )(qb, kbb, vbt)


TSLOTS = 8
SROWS = TSLOTS * N_HEADS_A


def _largest_divisor(n, cap):
    return max(x for x in range(1, cap + 1) if n % x == 0)


def _page_copies(pool_hbm, pt_ref, seq, buf, slot, sem, n_pages, start):
    def body(p, _):
        cp = pltpu.make_async_copy(pool_hbm.at[pt_ref[seq, p]], buf.at[slot, p], sem.at[slot])
        if start:
            cp.start()
        else:
            cp.wait()
        return 0

    lax.fori_loop(0, n_pages, body, 0)


def _page_ring(pool_hbm, pt_ref, buf, sem, n_pages):
    b = pl.program_id(0)
    slot = b & 1

    @pl.when(b == 0)
    def _():
        _page_copies(pool_hbm, pt_ref, 0, buf, 0, sem, n_pages, True)

    @pl.when(b + 1 < pl.num_programs(0))
    def _():
        _page_copies(pool_hbm, pt_ref, b + 1, buf, 1 - slot, sem, n_pages, True)

    _page_copies(pool_hbm, pt_ref, b, buf, slot, sem, n_pages, False)
    return slot


def _dsa_sample_scores_kernel(n_pages, page, ppt, pt_ref, qi_ref, w_ref, kin_ref, kidx_hbm,
                              key_ref, buf, sem):
    slot = _page_ring(kidx_hbm, pt_ref, buf, sem, n_pages)
    ts = ppt * page
    qi = qi_ref[0]
    w = w_ref[0]

    def scores(kt):
        wrel = w * jnp.maximum(_dot(qi, kt), 0.0)
        return jnp.sum(wrel.reshape(TSLOTS, N_IDX_HEADS, kt.shape[1]), axis=1)

    for t in range(n_pages // ppt):
        kt = jnp.concatenate([buf[slot, t * ppt + q] for q in range(ppt)], axis=1).astype(BF16)
        key_ref[0, :, t * ts:(t + 1) * ts] = scores(kt)
    past = n_pages * page
    qslot = lax.broadcasted_iota(I32, (TSLOTS, page), 0)
    u = lax.broadcasted_iota(I32, (TSLOTS, page), 1)
    key_ref[0, :, past:past + page] = jnp.where(u <= qslot, scores(kin_ref[0]), -jnp.inf)


def _dsa_sample_select_kernel(k_sel, idx_bits, n_tiles, ts, key_ref, tau_ref, taui_ref):
    tau, tau_i = _topk_threshold(key_ref, n_tiles, ts, 1, key_ref.shape[0], k_sel, idx_bits)
    tau_ref[...] = jnp.broadcast_to(tau, tau_ref.shape)
    taui_ref[...] = jnp.broadcast_to(tau_i, taui_ref.shape)


def _dsa_sample_attend_kernel(n_pages, page, ppt, pt_ref, qa_ref, key_ref, tau_ref, taui_ref, kvn_ref, kva_hbm,
                              o_ref, buf, sem, s_ref):
    slot = _page_ring(kva_hbm, pt_ref, buf, sem, n_pages)
    ts = ppt * page
    past = n_pages * page
    kfeat = N_KV_A * HEAD_DIM
    qa = qa_ref[0]
    tau = tau_ref[:, 0:1]
    tau_i = taui_ref[:, 0:1]

    n_t = n_pages // ppt

    def masked_scores(kt, kpos0, k):
        n = kt.shape[1]
        kpos = kpos0 + lax.broadcasted_iota(I32, (TSLOTS, n), 1)
        qpos = past + lax.broadcasted_iota(I32, (TSLOTS, n), 0)
        sel = ((kt > tau) | ((kt == tau) & (kpos <= tau_i))) & (kpos <= qpos)
        bias = jnp.where(sel, 0.0, -jnp.inf)
        bias_rows = jnp.concatenate(
            [jnp.broadcast_to(bias[t:t + 1, :], (N_HEADS_A, n)) for t in range(TSLOTS)], axis=0)
        return _dot(qa, k) + bias_rows

    def pages(t, f0):
        return jnp.concatenate([buf[slot, t * ppt + q, f0:f0 + kfeat, :] for q in range(ppt)],
                               axis=1).astype(BF16)

    mx = jnp.full((SROWS, ts), -jnp.inf, F32)
    for t in range(n_t):
        s = masked_scores(key_ref[0, :, t * ts:(t + 1) * ts], t * ts, pages(t, 0))
        s_ref[:, t * ts:(t + 1) * ts] = s
        mx = jnp.maximum(mx, s)
    s_new = masked_scores(key_ref[0, :, past:past + page], past, kvn_ref[0, 0:kfeat, :])
    m = jnp.maximum(jnp.max(mx, axis=1, keepdims=True), jnp.max(s_new, axis=1, keepdims=True))
    m = jnp.where(m == -jnp.inf, 0.0, m)

    p_new = jnp.exp2(s_new - m)
    acc = _dot_nt(p_new.astype(BF16), kvn_ref[0, kfeat:2 * kfeat, :])
    psum = jnp.zeros((SROWS, ts), F32)
    for t in range(n_t):
        p = jnp.exp2(s_ref[:, t * ts:(t + 1) * ts] - m)
        psum = psum + p
        acc = acc + _dot_nt(p.astype(BF16), pages(t, kfeat))
    l = jnp.sum(psum, axis=1, keepdims=True) + jnp.sum(p_new, axis=1, keepdims=True)
    o_ref[0] = acc / jnp.where(l > 0.0, l, 1.0)


def _dsa_sample(qi_st, w_st, qa_st, kidx_pool, kva_pool, kin_t, kvn_t, page_table, dec_seq):
    n_seq, n_pages = page_table.shape
    page = kidx_pool.shape[2]
    nk = (n_pages + 1) * page
    ppt = _largest_divisor(n_pages, 8)
    sel_ts = _largest_divisor(n_pages + 1, 8) * page
    k_sel = max(1, min(TOPK_MAX, (n_pages * page + dec_seq) // 4))
    idx_bits = max(1, int(nk - 1).bit_length())
    seq_spec = lambda shp: pl.BlockSpec((1,) + shp, lambda b, pt: (b, 0, 0))
    hbm_spec = pl.BlockSpec(memory_space=pl.ANY)
    ring = lambda feat: [pltpu.VMEM((2, n_pages, feat, page), F32), pltpu.SemaphoreType.DMA((2,))]

    keys = pl.pallas_call(
        functools.partial(_dsa_sample_scores_kernel, n_pages, page, ppt),
        grid_spec=pltpu.PrefetchScalarGridSpec(
            num_scalar_prefetch=1, grid=(n_seq,),
            in_specs=[seq_spec((SROWS, IDX_DIM)), seq_spec((SROWS, 1)), seq_spec((IDX_DIM, page)), hbm_spec],
            out_specs=seq_spec((TSLOTS, nk)),
            scratch_shapes=ring(IDX_DIM)),
        out_shape=jax.ShapeDtypeStruct((n_seq, TSLOTS, nk), F32),
        compiler_params=_cparams("arbitrary"),
        name="dsa_sample_scores",
    )(page_table, qi_st, w_st, kin_t, kidx_pool)

    rows = n_seq * dec_seq
    tau, tau_i = pl.pallas_call(
        functools.partial(_dsa_sample_select_kernel, k_sel, idx_bits, nk // sel_ts, sel_ts),
        grid=(1,),
        in_specs=[_const_spec((rows, nk))],
        out_specs=[pl.BlockSpec((rows, LANES), lambda i: (0, 0))] * 2,
        out_shape=[jax.ShapeDtypeStruct((rows, LANES), F32), jax.ShapeDtypeStruct((rows, LANES), I32)],
        compiler_params=_cparams("arbitrary"),
        name="dsa_sample_select",
    )(keys[:, :dec_seq].reshape(rows, nk))
    pad_slots = lambda a: jnp.pad(a.reshape(n_seq, dec_seq, LANES),
                                  ((0, 0), (0, TSLOTS - dec_seq), (0, 0))).reshape(n_seq * TSLOTS, LANES)
    tau, tau_i = pad_slots(tau), pad_slots(tau_i)

    row_spec = pl.BlockSpec((TSLOTS, LANES), lambda b, pt: (b, 0))
    return pl.pallas_call(
        functools.partial(_dsa_sample_attend_kernel, n_pages, page, ppt),
        grid_spec=pltpu.PrefetchScalarGridSpec(
            num_scalar_prefetch=1, grid=(n_seq,),
            in_specs=[seq_spec((SROWS, LANES)), seq_spec((TSLOTS, nk)), row_spec, row_spec,
                      seq_spec((KV_A_W, page)), hbm_spec],
            out_specs=seq_spec((SROWS, LANES)),
            scratch_shapes=ring(KV_A_W) + [pltpu.VMEM((SROWS, n_pages * page), F32)]),
        out_shape=jax.ShapeDtypeStruct((n_seq, SROWS, LANES), F32),
        compiler_params=_cparams("arbitrary"),
        name="dsa_sample_attend",
    )(page_table, qa_st, keys, tau, tau_i, kvn_t, kva_pool)


def _sb_sample_kernel(n_pages, page, pt_ref, q_ref, kvn_ref, kvb_hbm, o_ref, newest, newest_sem, buf, sem,
                      acc_ref, carry_ref):
    b = pl.program_id(0)
    q = q_ref[0]
    later_m = _later_matrix(page, 1)
    last = n_pages - 1
    walk = last - 1

    def newest_copy(seq):
        slot = seq & 1
        return pltpu.make_async_copy(kvb_hbm.at[pt_ref[seq, last]], newest.at[slot], newest_sem.at[slot])

    def page_copy(pg):
        slot = (walk - pg) & 1
        return pltpu.make_async_copy(kvb_hbm.at[pt_ref[b, pg]], buf.at[slot], sem.at[slot])

    def tile(kt, vt, strict):
        a, new_carry = _sb_tile(_dot(q, kt), strict, later_m, carry_ref[:, 0:1], 1)
        acc_ref[...] += _dot_nt(a.astype(BF16), vt)
        carry_ref[...] = jnp.broadcast_to(new_carry, carry_ref.shape)
        return jnp.max(new_carry) >= SB_STOP

    @pl.when(b == 0)
    def _():
        newest_copy(b).start()

    @pl.when(b + 1 < pl.num_programs(0))
    def _():
        newest_copy(b + 1).start()

    if walk >= 0:
        page_copy(walk).start()
    acc_ref[...] = jnp.zeros_like(acc_ref)
    carry_ref[...] = jnp.zeros_like(carry_ref)
    qslot = lax.broadcasted_iota(I32, (SROWS, page), 0) // N_HEADS_B
    u = lax.broadcasted_iota(I32, (SROWS, page), 1)
    tile(kvn_ref[0, 0:Q_B_W, :], kvn_ref[0, Q_B_W:2 * Q_B_W, :], u < qslot)
    newest_copy(b).wait()
    ns = b & 1
    go = tile(newest[ns, 0:Q_B_W, :].astype(BF16), newest[ns, Q_B_W:2 * Q_B_W, :].astype(BF16), None)

    if walk >= 0:
        def cond(state):
            pg, go = state
            return jnp.logical_and(pg >= 0, go)

        def body(state):
            pg, _ = state
            page_copy(pg).wait()

            @pl.when(pg > 0)
            def _():
                page_copy(pg - 1).start()

            slot = (walk - pg) & 1
            go = tile(buf[slot, 0:Q_B_W, :].astype(BF16), buf[slot, Q_B_W:2 * Q_B_W, :].astype(BF16), None)
            return pg - 1, go

        pg_end, _ = lax.while_loop(cond, body, (jnp.int32(walk), go))

        @pl.when(pg_end >= 0)
        def _():
            page_copy(pg_end).wait()

    r = lax.broadcasted_iota(I32, (SROWS, Q_B_W), 0)
    c = lax.broadcasted_iota(I32, (SROWS, Q_B_W), 1)
    own = (r % N_HEADS_B) == (c // HEAD_DIM)
    diag = jnp.where(own, acc_ref[...], 0.0)
    o_ref[0] = jnp.sum(diag.reshape(TSLOTS, N_HEADS_B, Q_B_W), axis=1)


def _sb_sample(qb_bd, kvb_pool, kvn_t, page_table):
    n_seq, n_pages = page_table.shape
    page = kvb_pool.shape[2]
    seq_spec = lambda shp: pl.BlockSpec((1,) + shp, lambda b, pt: (b, 0, 0))
    return pl.pallas_call(
        functools.partial(_sb_sample_kernel, n_pages, page),
        grid_spec=pltpu.PrefetchScalarGridSpec(
            num_scalar_prefetch=1, grid=(n_seq,),
            in_specs=[seq_spec((SROWS, Q_B_W)), seq_spec((KV_B_W, page)), pl.BlockSpec(memory_space=pl.ANY)],
            out_specs=seq_spec((TSLOTS, Q_B_W)),
            scratch_shapes=[pltpu.VMEM((2, KV_B_W, page), F32), pltpu.SemaphoreType.DMA((2,)),
                            pltpu.VMEM((2, KV_B_W, page), F32), pltpu.SemaphoreType.DMA((2,)),
                            pltpu.VMEM((SROWS, Q_B_W), F32), pltpu.VMEM((SROWS, LANES), F32)]),
        out_shape=jax.ShapeDtypeStruct((n_seq, TSLOTS, Q_B_W), F32),
        compiler_params=_cparams("arbitrary"),
        name="sb_sample",
    )(page_table, qb_bd, kvn_t, kvb_pool)


def _out_kernel(oa_ref, ob_ref, gate_ref, x_ref, wa_ref, wb_ref, wo_ref, nf_ref,
                wg_ref, wu_ref, wd_ref, nfin_ref, y_ref):
    ba = _dot(oa_ref[...].astype(BF16), wa_ref[...])
    bb = _dot(ob_ref[...].astype(BF16), wb_ref[...])
    merged = (gate_ref[:, 0:D_MODEL].astype(F32) * ba
              + gate_ref[:, D_MODEL:2 * D_MODEL].astype(F32) * bb)
    h1 = x_ref[...] + _dot(merged.astype(BF16), wo_ref[...])
    ms = jnp.mean(h1 * h1, axis=-1, keepdims=True)
    hn = ((h1 * lax.rsqrt(ms + RMS_EPS)) * nf_ref[...]).astype(BF16)
    acc = jnp.zeros_like(h1)
    for c in range(D_FF // FFN_CHUNK):
        sl = slice(c * FFN_CHUNK, (c + 1) * FFN_CHUNK)
        gg = _dot(hn, wg_ref[:, sl])
        uu = _dot(hn, wu_ref[:, sl])
        act = (gg * (1.0 / (1.0 + jnp.exp(-gg)))) * uu
        acc = acc + _dot(act.astype(BF16), wd_ref[sl, :])
    h2 = h1 + acc
    ms2 = jnp.mean(h2 * h2, axis=-1, keepdims=True)
    y_ref[...] = (h2 * lax.rsqrt(ms2 + RMS_EPS)) * nfin_ref[...]


def _out(oa, ob, gates, x2d, wts, tm):
    rows = x2d.shape[0]
    row_spec = lambda n: pl.BlockSpec((tm, n), lambda i: (i, 0))
    return pl.pallas_call(
        _out_kernel,
        grid=(rows // tm,),
        in_specs=[row_spec(Q_A_W), row_spec(Q_B_W), row_spec(2 * D_MODEL), row_spec(D_MODEL)]
                 + [_const_spec(w.shape) for w in wts],
        out_specs=row_spec(D_MODEL),
        out_shape=jax.ShapeDtypeStruct((rows, D_MODEL), F32),
        compiler_params=_cparams("parallel"),
        name="out",
    )(oa, ob, gates, x2d, *wts)


def _rope_tables(pos):
    inv_freq = ROPE_THETA ** (-jnp.arange(HALF, dtype=F32) / HALF)
    ang = pos.astype(F32)[:, None] * inv_freq[None, :]
    cos, sin = jnp.cos(ang), jnp.sin(ang)
    cs = jnp.concatenate([cos, cos, cos, cos], axis=1)
    sn = jnp.concatenate([-sin, sin, -sin, sin], axis=1)
    return cs, sn, cos.T, sin.T


def _pack_w_in(w):
    d = w.shape[0]
    o = np.cumsum([0, Q_A_W, KV_A_W, Q_I_W, IDX_DIM, N_IDX_HEADS, Q_B_W, KV_B_W, 2 * D_MODEL])
    kw = N_KV_A * HEAD_DIM
    qa = w[:, o[0]:o[1]].reshape(d, N_HEADS_A, HEAD_DIM)[:, np.array(QA_HEAD_ORDER), :].reshape(d, Q_A_W)
    ka, va = w[:, o[1]:o[1] + kw], w[:, o[1] + kw:o[2]]
    qi, ki, wi, qb = w[:, o[2]:o[3]], w[:, o[3]:o[4]], w[:, o[4]:o[5]], w[:, o[5]:o[6]]
    kb, vb = w[:, o[6]:o[6] + Q_B_W], w[:, o[6] + Q_B_W:o[7]]
    g = w[:, o[7]:o[8]]
    w_tok = jnp.concatenate([qa, qi, qb, g], axis=1).astype(BF16)
    wi_pad = jnp.pad(wi, ((0, 0), (0, HEAD_DIM - N_IDX_HEADS)))
    w_feat = jnp.concatenate([ka, va, ki, wi_pad, kb, vb], axis=1).T.astype(BF16)
    return w_tok, w_feat


def kernel(x_prompt, x_sample, cache_kv_a, cache_k_idx, cache_kv_b, page_table, w_in, w_br_a, w_br_b,
           w_o, norm_attn, norm_ffn, w_ffn_gate, w_ffn_up, w_ffn_down, norm_final):
    depth = w_in.shape[0]
    assert depth == 1, "single-layer step"
    batch, seq, d = x_prompt.shape
    assert d == D_MODEL and seq % ROW_TILE == 0
    layer = 0

    w_tok, w_feat = _pack_w_in(w_in[layer])
    wa = w_br_a[layer].reshape(N_HEADS_A, HEAD_DIM, D_MODEL)[np.array(QA_HEAD_ORDER)].reshape(Q_A_W, D_MODEL)
    out_wts = (wa.astype(BF16), w_br_b[layer].astype(BF16), w_o[layer].astype(BF16),
               norm_ffn[layer].reshape(1, D_MODEL),
               w_ffn_gate[layer].astype(BF16), w_ffn_up[layer].astype(BF16), w_ffn_down[layer].astype(BF16),
               norm_final.reshape(1, D_MODEL))
    gain = norm_attn[layer].reshape(1, D_MODEL)

    xp = x_prompt.reshape(batch * seq, D_MODEL)
    (kva_p, kidx_p, kvb_p, qa, qi, wit, qb, gates, kab, kib, kbb, vat, vbt) = _project(
        xp, gain, w_tok, w_feat, *_rope_tables(jnp.arange(seq, dtype=I32)), batch, ROW_TILE)
    oa = _dsa_prompt(qi, wit, qa, kib, kab, vat, batch, seq)
    ob = _sb_prompt(qb, kbb, vbt, batch, seq)
    y_prompt = _out(oa, ob, gates, xp, out_wts, ROW_TILE).reshape(batch, seq, D_MODEL)

    new_kv_a_p = kva_p.reshape(depth, batch, 2, N_KV_A, HEAD_DIM, seq).transpose(0, 1, 5, 2, 3, 4)
    new_k_idx_p = kidx_p.reshape(depth, batch, IDX_DIM, seq).transpose(0, 1, 3, 2)
    new_kv_b_p = kvb_p.reshape(depth, batch, 2, N_HEADS_B, HEAD_DIM, seq).transpose(0, 1, 5, 2, 3, 4)

    n_seq, dec_seq, _ = x_sample.shape
    n_pages = page_table.shape[1]
    n_pool, page = cache_kv_a.shape[1], cache_kv_a.shape[2]
    past = n_pages * page
    rows_s = n_seq * dec_seq
    assert dec_seq <= TSLOTS and rows_s % KBLK == 0 and page == LANES
    xs = x_sample.reshape(rows_s, D_MODEL)
    cs_s, sn_s, csf_s, snf_s = _rope_tables(past + jnp.arange(dec_seq, dtype=I32))
    tabs_s = (jnp.tile(cs_s, (n_seq, 1)), jnp.tile(sn_s, (n_seq, 1)),
              jnp.tile(csf_s, (1, n_seq)), jnp.tile(snf_s, (1, n_seq)))
    (kva_s, kidx_s, kvb_s, qa_s, qi_s, wit_s, qb_s, gates_s, _, _, _, _, _) = _project(
        xs, gain, w_tok, w_feat, *tabs_s, 1, rows_s)

    def slots(a):
        return jnp.pad(a, ((0, 0), (0, TSLOTS - dec_seq)) + ((0, 0),) * (a.ndim - 2))

    def new_tile(a):
        t = a[0].reshape(a.shape[1], n_seq, dec_seq).transpose(1, 0, 2)
        return jnp.pad(t, ((0, 0), (0, 0), (0, page - dec_seq))).astype(BF16)

    qi_st = slots(qi_s.reshape(n_seq, dec_seq, N_IDX_HEADS, IDX_DIM)).reshape(n_seq, SROWS, IDX_DIM)
    w_st = slots(wit_s[0].T.reshape(n_seq, dec_seq, N_IDX_HEADS)).reshape(n_seq, SROWS, 1)
    qa5 = qa_s.reshape(n_seq, dec_seq, GROUP_A, N_KV_A, HEAD_DIM).transpose(0, 1, 3, 2, 4)
    eye2 = jnp.eye(N_KV_A, dtype=qa_s.dtype)
    qa_st = qa5[:, :, :, :, None, :] * eye2[None, None, :, None, :, None]
    qa_st = slots(qa_st.reshape(n_seq, dec_seq, N_HEADS_A, LANES)).reshape(n_seq, SROWS, LANES)
    kidx_pool = cache_k_idx[layer].transpose(0, 2, 1)
    kva_pool = cache_kv_a[layer].transpose(0, 2, 3, 4, 1).reshape(n_pool, KV_A_W, page)
    kvb_pool = cache_kv_b[layer].transpose(0, 2, 3, 4, 1).reshape(n_pool, KV_B_W, page)
    o_st = _dsa_sample(qi_st, w_st, qa_st, kidx_pool, kva_pool, new_tile(kidx_s), new_tile(kva_s),
                       page_table, dec_seq)
    o6 = o_st.reshape(n_seq, TSLOTS, N_KV_A, GROUP_A, N_KV_A, HEAD_DIM)[:, :dec_seq]
    oa_s = jnp.stack([o6[:, :, c, :, c, :] for c in range(N_KV_A)], axis=2).reshape(rows_s, Q_A_W)

    qb4 = qb_s.reshape(n_seq, dec_seq, N_HEADS_B, HEAD_DIM)
    eye8 = jnp.eye(N_HEADS_B, dtype=qb_s.dtype)
    qb_bd = qb4[:, :, :, None, :] * eye8[None, None, :, :, None]
    qb_bd = slots(qb_bd.reshape(n_seq, dec_seq, N_HEADS_B, Q_B_W)).reshape(n_seq, SROWS, Q_B_W)
    ob_st = _sb_sample(qb_bd, kvb_pool, new_tile(kvb_s), page_table)
    ob_s = ob_st[:, :dec_seq].reshape(rows_s, Q_B_W)

    out_wts_s = (w_br_a[layer].astype(BF16),) + out_wts[1:]
    y_sample = _out(oa_s, ob_s, gates_s, xs, out_wts_s, rows_s).reshape(n_seq, dec_seq, D_MODEL)

    return (y_prompt, y_sample, new_kv_a_p, new_k_idx_p, new_kv_b_p,
            kva_s[0].T.reshape(depth, n_seq, dec_seq, 2, N_KV_A, HEAD_DIM),
            kidx_s[0].T.reshape(depth, n_seq, dec_seq, IDX_DIM),
            kvb_s[0].T.reshape(depth, n_seq, dec_seq, 2, N_HEADS_B, HEAD_DIM))
```

```python
import functools

import jax
import jax.numpy as jnp
import numpy as np
from jax import lax
from jax.experimental import pallas as pl
from jax.experimental.pallas import tpu as pltpu

F32 = jnp.float32
BF16 = jnp.bfloat16
I32 = jnp.int32

D_MODEL = 1024
HEAD_DIM = 64
N_HEADS_A = 8
N_KV_A = 2
GROUP_A = N_HEADS_A // N_KV_A
N_IDX_HEADS = 8
IDX_DIM = 64
TOPK_MAX = 256
N_HEADS_B = 8
Q_A_W = N_HEADS_A * HEAD_DIM
KV_A_W = 2 * N_KV_A * HEAD_DIM
Q_I_W = N_IDX_HEADS * IDX_DIM
Q_B_W = N_HEADS_B * HEAD_DIM
KV_B_W = 2 * N_HEADS_B * HEAD_DIM
D_FF = -(-8 * D_MODEL // (3 * 256)) * 256
ROPE_THETA = 10000.0
RMS_EPS = 1e-6
QK_SCALE = HEAD_DIM ** -0.5
IDX_SCALE = IDX_DIM ** -0.5
HALF = HEAD_DIM // 2
LOG2_E = 1.4426950408889634

LANES = 128
SUBLANES = 8
QBLK = 256
KBLK = 128
SUM_ROWS = 16
FFN_CHUNK = 256
ROW_TILE = 512

TK_QA = 0
TK_QI = 512
TK_QB = 1024
TK_G = 1536
TK_W = 3584
FM_KA = 0
FM_VA = 128
FM_KI = 256
FM_WI = 320
FM_KB = 384
FM_VB = 896
FM_W = 1408
QA_HEAD_ORDER = (0, 4, 1, 5, 2, 6, 3, 7)

NEGINF_KEY = -2139095041
POSINF_KEY = 0x7F800000
MIN_NORMAL_KEY = 0x00800000
IDX_ALL = 1 << 30
SB_STOP = -105.0

VMEM_LIMIT = 60 * 1024 * 1024

NT_DIMS = (((1,), (1,)), ((), ()))


def _cparams(*sem):
    return pltpu.CompilerParams(dimension_semantics=sem, vmem_limit_bytes=VMEM_LIMIT)


def _const_spec(shape):
    nd = len(shape)
    return pl.BlockSpec(shape, lambda *_: (0,) * nd, pipeline_mode=pl.Buffered(1))


def _dot_nt(a, b):
    return lax.dot_general(a, b, NT_DIMS, preferred_element_type=F32)


def _dot(a, b):
    return jnp.dot(a, b, preferred_element_type=F32)


def _project_kernel(x_ref, g_ref, wt_ref, wf_ref, cs_ref, sn_ref, csf_ref, snf_ref,
                    kva_ref, kidx_ref, kvb_ref,
                    qa_ref, qi_ref, wit_ref, qb_ref, gate_ref,
                    kab_ref, kib_ref, kbb_ref, vat_ref, vbt_ref):
    x = x_ref[...]
    ms = jnp.mean(x * x, axis=-1, keepdims=True)
    xn = (x * lax.rsqrt(ms + RMS_EPS)) * g_ref[...]
    xb = xn.astype(BF16)
    tm = x.shape[0]
    cs = cs_ref[...]
    sn = sn_ref[...]
    lane = lax.broadcasted_iota(I32, (tm, LANES), 1)
    first_half = (lane & HALF) == 0

    def mm(c0, n):
        return _dot(xb, wt_ref[:, c0:c0 + n])

    def rope(y):
        partner = jnp.where(first_half, pltpu.roll(y, LANES - HALF, 1), pltpu.roll(y, HALF, 1))
        return y * cs + partner * sn

    ya = mm(TK_QA, Q_A_W)
    for c in range(Q_A_W // LANES):
        sl = slice(c * LANES, (c + 1) * LANES)
        qa_ref[:, sl] = (rope(ya[:, sl]) * (QK_SCALE * LOG2_E)).astype(BF16)
    yi = mm(TK_QI, Q_I_W)
    for c in range(Q_I_W // LANES):
        sl = slice(c * LANES, (c + 1) * LANES)
        qi_ref[:, sl] = rope(yi[:, sl]).astype(BF16)
    qb_ref[...] = (mm(TK_QB, Q_B_W) * QK_SCALE).astype(BF16)
    for c in range(2 * D_MODEL // 512):
        sl = slice(c * 512, (c + 1) * 512)
        y = mm(TK_G + c * 512, 512)
        gate_ref[:, sl] = (1.0 / (1.0 + jnp.exp(-y))).astype(BF16)

    csf = csf_ref[...]
    snf = snf_ref[...]

    def fm(r0, n):
        return _dot_nt(wf_ref[r0:r0 + n, :], xb)

    def rope_fm(y):
        lo, hi = y[0:HALF], y[HALF:HEAD_DIM]
        return jnp.concatenate([lo * csf - hi * snf, hi * csf + lo * snf], axis=0)

    ka = fm(FM_KA, N_KV_A * HEAD_DIM)
    ka = jnp.concatenate([rope_fm(ka[c * HEAD_DIM:(c + 1) * HEAD_DIM]) for c in range(N_KV_A)], axis=0)
    kva_ref[0, 0:N_KV_A * HEAD_DIM, :] = ka
    kab_ref[...] = ka.T.astype(BF16)
    va = fm(FM_VA, N_KV_A * HEAD_DIM)
    kva_ref[0, N_KV_A * HEAD_DIM:2 * N_KV_A * HEAD_DIM, :] = va
    ki = rope_fm(fm(FM_KI, IDX_DIM))
    kidx_ref[0] = ki
    kib_ref[...] = jnp.concatenate([ki, ki], axis=0).T.astype(BF16)
    wi = fm(FM_WI, HEAD_DIM)
    wit_ref[0] = (wi[0:N_IDX_HEADS] * (N_IDX_HEADS ** -0.5)) * IDX_SCALE
    kb = fm(FM_KB, Q_B_W)
    kvb_ref[0, 0:Q_B_W, :] = kb
    kbb_ref[...] = kb.T.astype(BF16)
    vb = fm(FM_VB, Q_B_W)
    kvb_ref[0, Q_B_W:2 * Q_B_W, :] = vb
    for q in range(tm // KBLK):
        sl = slice(q * KBLK, (q + 1) * KBLK)
        vat_ref[0, q] = va[:, sl].astype(BF16)
        vbt_ref[0, q] = vb[:, sl].astype(BF16)


def _project(x2d, gain, w_tok, w_feat, cs, sn, csf, snf, groups, tm):
    rows = x2d.shape[0]
    t = rows // groups
    nb = t // tm
    nkb = tm // KBLK
    row_spec = lambda n: pl.BlockSpec((tm, n), lambda i: (i, 0))
    tab_spec = pl.BlockSpec((tm, LANES), lambda i: (i % nb, 0))
    tabf_spec = pl.BlockSpec((HALF, tm), lambda i: (0, i % nb))
    fm_spec = lambda n: pl.BlockSpec((1, n, tm), lambda i: (i // nb, 0, i % nb))
    blk_spec = lambda n: pl.BlockSpec((1, nkb, n, KBLK), lambda i: (i // nb, i % nb, 0, 0))
    out_specs = [fm_spec(KV_A_W), fm_spec(IDX_DIM), fm_spec(KV_B_W),
                 row_spec(Q_A_W), row_spec(Q_I_W), fm_spec(N_IDX_HEADS), row_spec(Q_B_W), row_spec(2 * D_MODEL),
                 row_spec(LANES), row_spec(LANES), row_spec(Q_B_W),
                 blk_spec(N_KV_A * HEAD_DIM), blk_spec(Q_B_W)]
    fm_shape = lambda n, dt: jax.ShapeDtypeStruct((groups, n, t), dt)
    tk_shape = lambda n, dt: jax.ShapeDtypeStruct((rows, n), dt)
    blk_shape = lambda n: jax.ShapeDtypeStruct((groups, t // KBLK, n, KBLK), BF16)
    out_shape = [fm_shape(KV_A_W, F32), fm_shape(IDX_DIM, F32), fm_shape(KV_B_W, F32),
                 tk_shape(Q_A_W, BF16), tk_shape(Q_I_W, BF16), fm_shape(N_IDX_HEADS, F32),
                 tk_shape(Q_B_W, BF16), tk_shape(2 * D_MODEL, BF16),
                 tk_shape(LANES, BF16), tk_shape(LANES, BF16), tk_shape(Q_B_W, BF16),
                 blk_shape(N_KV_A * HEAD_DIM), blk_shape(Q_B_W)]
    return pl.pallas_call(
        _project_kernel,
        grid=(rows // tm,),
        in_specs=[row_spec(D_MODEL), _const_spec((1, D_MODEL)), _const_spec((D_MODEL, TK_W)),
                  _const_spec((FM_W, D_MODEL)), tab_spec, tab_spec, tabf_spec, tabf_spec],
        out_specs=out_specs,
        out_shape=out_shape,
        compiler_params=_cparams("parallel"),
        name="project",
    )(x2d, gain, w_tok, w_feat, cs, sn, csf, snf)


def _key_to_float(k):
    return lax.bitcast_convert_type(k ^ ((k >> 31) & jnp.int32(0x7FFFFFFF)), F32)


def _fold_rows(x, op, rows=SUBLANES):
    n = x.shape[0]
    assert n % rows == 0 and (n // rows) & (n // rows - 1) == 0
    while n > rows:
        n //= 2
        x = op(x[:n], x[n:])
    return x


def _count(key_ref, n_tiles, ts, key_axis, width, pred):
    if key_axis == 1:
        def body(j, acc):
            off = pl.multiple_of(j * ts, ts)
            kt = key_ref[:, pl.ds(off, ts)]
            kpos = j * ts + lax.broadcasted_iota(I32, (width, ts), 1)
            m = jnp.where(pred(kt, kpos, lambda v: v), 1.0, 0.0)
            s = m[:, 0:LANES]
            for c in range(1, ts // LANES):
                s = s + m[:, c * LANES:(c + 1) * LANES]
            return acc + s

        acc = lax.fori_loop(0, n_tiles, body, jnp.zeros((width, LANES), F32))
        return jnp.sum(acc, axis=1, keepdims=True)

    def body(j, acc):
        off = pl.multiple_of(j * ts, ts)
        kpos = j * ts + lax.broadcasted_iota(I32, (ts, LANES), 0)
        parts = []
        for g in range(width // LANES):
            gs = slice(g * LANES, (g + 1) * LANES)
            hit = pred(key_ref[pl.ds(off, ts), gs], kpos, lambda v, gs=gs: v[:, gs])
            parts.append(_fold_rows(jnp.where(hit, 1.0, 0.0), jnp.add))
        return acc + jnp.concatenate(parts, axis=1)

    acc = lax.fori_loop(0, n_tiles, body, jnp.zeros((SUBLANES, width), F32))
    return jnp.sum(acc, axis=0, keepdims=True)


def _topk_threshold(key_ref, n_tiles, ts, key_axis, width, k_sel, idx_bits):
    k_f = float(k_sel)
    vec = (width, 1) if key_axis == 1 else (1, width)
    count = functools.partial(_count, key_ref, n_tiles, ts, key_axis, width)

    def as_float(k):
        zero_band = (k >= -MIN_NORMAL_KEY) & (k < MIN_NORMAL_KEY)
        return _key_to_float(jnp.where(zero_band, 0, k))

    def bis(b, r):
        cand = r + lax.shift_left(jnp.int32(1), jnp.int32(31) - b)
        valid = (cand >= NEGINF_KEY) & (cand <= POSINF_KEY)
        cand_f = as_float(jnp.where(valid, cand, 0))
        return jnp.where(valid & (count(lambda st, kp, q: st >= q(cand_f)) >= k_f), cand, r)

    tau = as_float(lax.fori_loop(0, 32, bis, jnp.full(vec, NEGINF_KEY, I32)))
    need = k_f - count(lambda st, kp, q: st > q(tau))
    tie = (count(lambda st, kp, q: st == q(tau)) > need) & (tau > -jnp.inf)

    def resolve():
        def ib(b, x):
            c = x + lax.shift_left(jnp.int32(1), jnp.int32(idx_bits - 1) - b)
            cnt = count(lambda st, kp, q: (st == q(tau)) & (kp < q(c)))
            return jnp.where(cnt < need, c, x)

        x = lax.fori_loop(0, idx_bits, ib, jnp.zeros(vec, I32))
        return jnp.where(tie, x, IDX_ALL)

    any_tie = jnp.max(jnp.where(tie, 1.0, 0.0)) > 0.5
    tau_i = lax.cond(any_tie, resolve, lambda: jnp.full(vec, IDX_ALL, I32))
    return tau, tau_i


DSA_TS = 256


def _dsa_prompt_kernel(k_sel, idx_bits, qi_ref, wit_ref, qa_ref, ki_ref, ka_ref, vat_ref,
                       oa_ref, key_ref, rel0_ref, rel1_ref, s0_ref, s1_ref, p0_ref, p1_ref):
    i = pl.program_id(1)
    ts = DSA_TS
    n_tiles = (i * QBLK + QBLK + ts - 1) // ts
    lane = lax.broadcasted_iota(I32, (QBLK, LANES), 1)
    left = lane < HEAD_DIM
    krow = lax.broadcasted_iota(I32, (ts, QBLK), 0)
    qpos = i * QBLK + lax.broadcasted_iota(I32, (ts, QBLK), 1)

    zero = jnp.zeros((QBLK, LANES), BF16)
    qi_parts = []
    for p in range(Q_I_W // LANES):
        chunk = qi_ref[:, p * LANES:(p + 1) * LANES]
        qi_parts.append(jnp.where(left, chunk, zero))
        qi_parts.append(jnp.where(left, zero, chunk))
    qi_all = jnp.concatenate(qi_parts, axis=0)
    wt = wit_ref[0]

    last = n_tiles - 1
    n_pairs = (n_tiles + 1) // 2

    def tile_off(j):
        return pl.multiple_of(jnp.minimum(j, last) * ts, ts)

    def rel_tile(j):
        return _dot_nt(ki_ref[pl.ds(tile_off(j), ts), :], qi_all)

    def score_step(j, cur_ref, nxt_ref):
        rel = cur_ref[...]
        nxt_ref[...] = rel_tile(j + 1)
        sc = jnp.zeros((ts, QBLK), F32)
        for h in range(N_IDX_HEADS):
            sc = sc + wt[h:h + 1, :] * jnp.maximum(rel[:, h * QBLK:(h + 1) * QBLK], 0.0)
        sc = jnp.where(j * ts + krow <= qpos, sc, -jnp.inf)
        key_ref[pl.ds(pl.multiple_of(j * ts, ts), ts), :] = sc

    def score_pair(jj, _):
        score_step(2 * jj, rel0_ref, rel1_ref)
        score_step(2 * jj + 1, rel1_ref, rel0_ref)
        return 0

    q4 = []
    for c in range(N_KV_A):
        parts = []
        for g in range(GROUP_A):
            chunk = qa_ref[:, g * LANES:(g + 1) * LANES]
            parts.append(jnp.where(left, chunk, zero) if c == 0 else jnp.where(left, zero, chunk))
        q4.append(jnp.concatenate(parts, axis=0))

    cols4 = GROUP_A * QBLK
    kb_per_tile = ts // KBLK

    def s_tile(j):
        k = ka_ref[pl.ds(tile_off(j), ts), :]
        return [_dot_nt(k, q4[c]) for c in range(N_KV_A)]

    rel0_ref[...] = rel_tile(0)
    first = s_tile(0)
    for c in range(N_KV_A):
        s0_ref[c] = first[c]
    p1_ref[...] = jnp.zeros_like(p1_ref)
    lax.fori_loop(0, n_pairs, score_pair, 0)
    tau, tau_i = _topk_threshold(key_ref, n_pairs, 2 * ts, 0, QBLK, k_sel, idx_bits)

    ones_rows = jnp.ones((SUM_ROWS, ts), BF16)

    def values(j):
        jb = jnp.clip(j, 0, last) * kb_per_tile
        v = jnp.concatenate([vat_ref[0, jb + q] for q in range(kb_per_tile)], axis=1)
        return jnp.concatenate([v, ones_rows], axis=0)

    def pv(vals, p_ref, c):
        return _dot(vals, p_ref[c])

    def attn_step(j, carry, s_cur, s_nxt, p_cur, p_prev):
        kt = key_ref[pl.ds(tile_off(j), ts), :]
        kpos = j * ts + krow
        sel = ((kt > tau) | ((kt == tau) & (kpos <= tau_i))) & (kpos <= qpos)
        bias = jnp.where(sel, 0.0, -jnp.inf)
        bias4 = jnp.concatenate([bias] * GROUP_A, axis=1)
        nxt = s_tile(j + 1)
        vals = values(j - 1)
        out = []
        for c in range(N_KV_A):
            m, acc = carry[c]
            s = s_cur[c] + bias4
            s_nxt[c] = nxt[c]
            m_new = jnp.maximum(m, jnp.max(_fold_rows(s, jnp.maximum), axis=0, keepdims=True))
            m_safe = jnp.where(m_new == -jnp.inf, 0.0, m_new)
            acc = (acc + pv(vals, p_prev, c)) * jnp.exp2(m - m_safe)
            p_cur[c] = jnp.exp2(s - m_safe).astype(BF16)
            out.append((m_new, acc))
        return tuple(out)

    def attn_pair(jj, carry):
        carry = attn_step(2 * jj, carry, s0_ref, s1_ref, p0_ref, p1_ref)
        return attn_step(2 * jj + 1, carry, s1_ref, s0_ref, p1_ref, p0_ref)

    init = tuple((jnp.full((1, cols4), -jnp.inf, F32), jnp.zeros((LANES + SUM_ROWS, cols4), F32))
                 for _ in range(N_KV_A))
    res = lax.fori_loop(0, n_pairs, attn_pair, init)
    vals = values(2 * n_pairs - 1)
    o = []
    for c in range(N_KV_A):
        acc = res[c][1] + pv(vals, p1_ref, c)
        o.append(acc[0:LANES] / acc[LANES:LANES + 1])
    top = lax.broadcasted_iota(I32, (LANES, QBLK), 0) < HEAD_DIM
    for g in range(GROUP_A):
        cs = slice(g * QBLK, (g + 1) * QBLK)
        oa_ref[:, g * LANES:(g + 1) * LANES] = jnp.where(top, o[0][:, cs], o[1][:, cs]).T


def _dsa_prompt(qi, wit, qa, kib, kab, vat, batch, seq):
    nq = seq // QBLK
    assert seq % (2 * DSA_TS) == 0, "key tiles are visited in pairs"
    k_sel = max(1, min(TOPK_MAX, seq // 4))
    idx_bits = max(1, int(seq - 1).bit_length())
    qspec = lambda n: pl.BlockSpec((QBLK, n), lambda b, i: (b * nq + i, 0))
    kspec = pl.BlockSpec((seq, LANES), lambda b, i: (b, 0))
    return pl.pallas_call(
        functools.partial(_dsa_prompt_kernel, k_sel, idx_bits),
        grid=(batch, nq),
        in_specs=[qspec(Q_I_W), pl.BlockSpec((1, N_IDX_HEADS, QBLK), lambda b, i: (b, 0, i)), qspec(Q_A_W),
                  kspec, kspec,
                  pl.BlockSpec((1, seq // KBLK, N_KV_A * HEAD_DIM, KBLK), lambda b, i: (b, 0, 0, 0))],
        out_specs=qspec(Q_A_W),
        out_shape=jax.ShapeDtypeStruct((batch * seq, Q_A_W), F32),
        scratch_shapes=[pltpu.VMEM((seq, QBLK), F32)]
                       + [pltpu.VMEM((DSA_TS, N_IDX_HEADS * QBLK), F32)] * 2
                       + [pltpu.VMEM((N_KV_A, DSA_TS, GROUP_A * QBLK), F32)] * 2
                       + [pltpu.VMEM((N_KV_A, DSA_TS, GROUP_A * QBLK), BF16)] * 2,
        compiler_params=_cparams("parallel", "arbitrary"),
        name="dsa_prompt",
    )(qi, wit, qa, kib, kab, vat)


def _later_matrix(n, key_axis):
    r = lax.broadcasted_iota(I32, (n, n), 0)
    c = lax.broadcasted_iota(I32, (n, n), 1)
    return jnp.where(r > c if key_axis == 1 else c > r, 1.0, 0.0).astype(BF16)


def _sb_tile(z, strict, later_m, carry, key_axis):
    soft = jnp.log(1.0 + jnp.exp(-jnp.abs(z)))
    log_beta = jnp.minimum(z, 0.0) - soft
    log_rest = log_beta - z
    if strict is not None:
        log_rest = jnp.where(strict, log_rest, 0.0)
    hi = log_rest.astype(BF16)
    lo = (log_rest - hi.astype(F32)).astype(BF16)
    if key_axis == 1:
        later = _dot(hi, later_m) + _dot(lo, later_m)
        first_l, first_r = later[:, 0:1], log_rest[:, 0:1]
    else:
        later = _dot(later_m, hi) + _dot(later_m, lo)
        first_l, first_r = later[0:1, :], log_rest[0:1, :]
    a = jnp.exp(log_beta + later + carry)
    if strict is not None:
        a = jnp.where(strict, a, 0.0)
    return a, carry + first_l + first_r


SB_TS = 128
SB_QBLK = 128


def _sb_prompt_kernel(qb_ref, kb_ref, vbt_ref, ob_ref, acc_ref, carry_ref, z_ref):
    i = pl.program_id(1)
    ts = SB_TS
    n_pairs = N_HEADS_B // 2
    qb = SB_QBLK
    pcols = 2 * qb
    cols = n_pairs * pcols
    lane = lax.broadcasted_iota(I32, (qb, LANES), 1)
    left = lane < HEAD_DIM
    zero = jnp.zeros((qb, LANES), BF16)
    later_m = _later_matrix(ts, 0)
    krow = lax.broadcasted_iota(I32, (ts, cols), 0)
    qpos = i * qb + (lax.broadcasted_iota(I32, (ts, cols), 1) & (qb - 1))
    top = lax.broadcasted_iota(I32, (LANES, qb), 0) < HEAD_DIM
    pair = [slice(p * LANES, (p + 1) * LANES) for p in range(n_pairs)]

    q2 = []
    for p in range(n_pairs):
        chunk = qb_ref[:, pair[p]]
        q2.append(jnp.concatenate([jnp.where(left, chunk, zero), jnp.where(left, zero, chunk)], axis=0))
    acc_ref[...] = jnp.zeros_like(acc_ref)
    carry_ref[...] = jnp.zeros_like(carry_ref)

    def cond(state):
        j, go = state
        return jnp.logical_and(j >= 0, go)

    def logits(j):
        off = pl.multiple_of(jnp.maximum(j, 0) * ts, ts)
        return jnp.concatenate([_dot_nt(kb_ref[pl.ds(off, ts), pair[p]], q2[p]) for p in range(n_pairs)],
                               axis=1)

    def body(state):
        j, _ = state
        z = z_ref[...]
        z_next = logits(j - 1)
        strict = (j * ts + krow) < qpos
        a, new_carry = _sb_tile(z, strict, later_m, carry_ref[0:1, :], 0)
        ab = a.astype(BF16)
        for p in range(n_pairs):
            acc_ref[p] += _dot(vbt_ref[0, j, pair[p], :], ab[:, p * pcols:(p + 1) * pcols])
        carry_ref[...] = jnp.broadcast_to(new_carry, carry_ref.shape)
        z_ref[...] = z_next
        return j - 1, jnp.max(new_carry) >= SB_STOP

    j0 = (i * qb + qb) // ts - 1
    z_ref[...] = logits(j0)
    lax.while_loop(cond, body, (j0, jnp.bool_(True)))
    for p in range(n_pairs):
        acc = acc_ref[p]
        ob_ref[:, pair[p]] = jnp.where(top, acc[:, 0:qb], acc[:, qb:pcols]).T


def _sb_prompt(qb, kbb, vbt, batch, seq):
    nq = seq // SB_QBLK
    qspec = pl.BlockSpec((SB_QBLK, Q_B_W), lambda b, i: (b * nq + i, 0))
    return pl.pallas_call(
        _sb_prompt_kernel,
        grid=(batch, nq),
        in_specs=[qspec, pl.BlockSpec((seq, Q_B_W), lambda b, i: (b, 0)),
                  pl.BlockSpec((1, seq // KBLK, Q_B_W, KBLK), lambda b, i: (b, 0, 0, 0))],
        out_specs=qspec,
        out_shape=jax.ShapeDtypeStruct((batch * seq, Q_B_W), F32),
        scratch_shapes=[pltpu.VMEM((N_HEADS_B // 2, LANES, 2 * SB_QBLK), F32),
                        pltpu.VMEM((SUBLANES, N_HEADS_B * SB_QBLK), F32),
                        pltpu.VMEM((SB_TS, N_HEADS_B * SB_QBLK), F32)],
        compiler_params=_cparams("parallel", "arbitrary"),
        name="sb_prompt",
    )(qb, kbb, vbt)


TSLOTS = 8
SROWS = TSLOTS * N_HEADS_A


def _largest_divisor(n, cap):
    return max(x for x in range(1, cap + 1) if n % x == 0)


def _page_copies(pool_hbm, pt_ref, seq, buf, slot, sem, n_pages, start):
    lanes = 2 if n_pages % 2 == 0 else 1

    def body(i, _):
        for k in range(lanes):
            p = i * lanes + k
            cp = pltpu.make_async_copy(pool_hbm.at[pt_ref[seq, p]], buf.at[slot, p], sem.at[slot])
            if start:
                cp.start(priority=k)
            else:
                cp.wait()
        return 0

    lax.fori_loop(0, n_pages // lanes, body, 0)


def _page_ring(pool_hbm, pt_ref, buf, sem, n_pages):
    b = pl.program_id(0)
    slot = b & 1

    @pl.when(b == 0)
    def _():
        _page_copies(pool_hbm, pt_ref, 0, buf, 0, sem, n_pages, True)

    @pl.when(b + 1 < pl.num_programs(0))
    def _():
        _page_copies(pool_hbm, pt_ref, b + 1, buf, 1 - slot, sem, n_pages, True)

    _page_copies(pool_hbm, pt_ref, b, buf, slot, sem, n_pages, False)
    return slot


def _dsa_sample_scores_kernel(n_pages, page, ppt, pt_ref, qi_ref, w_ref, kin_ref, kidx_hbm,
                              key_ref, buf, sem):
    slot = _page_ring(kidx_hbm, pt_ref, buf, sem, n_pages)
    ts = ppt * page
    qi = qi_ref[0]
    w = w_ref[0]

    def scores(kt):
        wrel = w * jnp.maximum(_dot(qi, kt), 0.0)
        return jnp.sum(wrel.reshape(TSLOTS, N_IDX_HEADS, kt.shape[1]), axis=1)

    for t in range(n_pages // ppt):
        kt = jnp.concatenate([buf[slot, t * ppt + q] for q in range(ppt)], axis=1).astype(BF16)
        key_ref[0, :, t * ts:(t + 1) * ts] = scores(kt)
    past = n_pages * page
    qslot = lax.broadcasted_iota(I32, (TSLOTS, page), 0)
    u = lax.broadcasted_iota(I32, (TSLOTS, page), 1)
    key_ref[0, :, past:past + page] = jnp.where(u <= qslot, scores(kin_ref[0]), -jnp.inf)


def _dsa_sample_select_kernel(k_sel, idx_bits, n_tiles, ts, key_ref, tau_ref, taui_ref):
    tau, tau_i = _topk_threshold(key_ref, n_tiles, ts, 1, key_ref.shape[0], k_sel, idx_bits)
    tau_ref[...] = jnp.broadcast_to(tau, tau_ref.shape)
    taui_ref[...] = jnp.broadcast_to(tau_i, taui_ref.shape)


def _dsa_sample_attend_kernel(n_pages, page, ppt, pt_ref, qa_ref, key_ref, tau_ref, taui_ref, kvn_ref, kva_hbm,
                              o_ref, buf, sem, s_ref):
    slot = _page_ring(kva_hbm, pt_ref, buf, sem, n_pages)
    ts = ppt * page
    past = n_pages * page
    kfeat = N_KV_A * HEAD_DIM
    qa = qa_ref[0]
    tau = tau_ref[:, 0:1]
    tau_i = taui_ref[:, 0:1]

    n_t = n_pages // ppt

    def masked_scores(kt, kpos0, k):
        n = kt.shape[1]
        kpos = kpos0 + lax.broadcasted_iota(I32, (TSLOTS, n), 1)
        qpos = past + lax.broadcasted_iota(I32, (TSLOTS, n), 0)
        sel = ((kt > tau) | ((kt == tau) & (kpos <= tau_i))) & (kpos <= qpos)
        bias = jnp.where(sel, 0.0, -jnp.inf)
        bias_rows = jnp.concatenate(
            [jnp.broadcast_to(bias[t:t + 1, :], (N_HEADS_A, n)) for t in range(TSLOTS)], axis=0)
        return _dot(qa, k) + bias_rows

    def pages(t, f0):
        return jnp.concatenate([buf[slot, t * ppt + q, f0:f0 + kfeat, :] for q in range(ppt)],
                               axis=1).astype(BF16)

    mx = jnp.full((SROWS, ts), -jnp.inf, F32)
    for t in range(n_t):
        s = masked_scores(key_ref[0, :, t * ts:(t + 1) * ts], t * ts, pages(t, 0))
        s_ref[:, t * ts:(t + 1) * ts] = s
        mx = jnp.maximum(mx, s)
    s_new = masked_scores(key_ref[0, :, past:past + page], past, kvn_ref[0, 0:kfeat, :])
    m = jnp.maximum(jnp.max(mx, axis=1, keepdims=True), jnp.max(s_new, axis=1, keepdims=True))
    m = jnp.where(m == -jnp.inf, 0.0, m)

    p_new = jnp.exp2(s_new - m)
    acc = _dot_nt(p_new.astype(BF16), kvn_ref[0, kfeat:2 * kfeat, :])
    psum = jnp.zeros((SROWS, ts), F32)
    for t in range(n_t):
        p = jnp.exp2(s_ref[:, t * ts:(t + 1) * ts] - m)
        psum = psum + p
        acc = acc + _dot_nt(p.astype(BF16), pages(t, kfeat))
    l = jnp.sum(psum, axis=1, keepdims=True) + jnp.sum(p_new, axis=1, keepdims=True)
    o_ref[0] = acc / jnp.where(l > 0.0, l, 1.0)


def _dsa_sample(qi_st, w_st, qa_st, kidx_pool, kva_pool, kin_t, kvn_t, page_table, dec_seq):
    n_seq, n_pages = page_table.shape
    page = kidx_pool.shape[2]
    nk = (n_pages + 1) * page
    ppt = _largest_divisor(n_pages, 8)
    sel_ts = _largest_divisor(n_pages + 1, 8) * page
    k_sel = max(1, min(TOPK_MAX, (n_pages * page + dec_seq) // 4))
    idx_bits = max(1, int(nk - 1).bit_length())
    seq_spec = lambda shp: pl.BlockSpec((1,) + shp, lambda b, pt: (b, 0, 0))
    hbm_spec = pl.BlockSpec(memory_space=pl.ANY)
    ring = lambda feat: [pltpu.VMEM((2, n_pages, feat, page), F32), pltpu.SemaphoreType.DMA((2,))]

    keys = pl.pallas_call(
        functools.partial(_dsa_sample_scores_kernel, n_pages, page, ppt),
        grid_spec=pltpu.PrefetchScalarGridSpec(
            num_scalar_prefetch=1, grid=(n_seq,),
            in_specs=[seq_spec((SROWS, IDX_DIM)), seq_spec((SROWS, 1)), seq_spec((IDX_DIM, page)), hbm_spec],
            out_specs=seq_spec((TSLOTS, nk)),
            scratch_shapes=ring(IDX_DIM)),
        out_shape=jax.ShapeDtypeStruct((n_seq, TSLOTS, nk), F32),
        compiler_params=_cparams("arbitrary"),
        name="dsa_sample_scores",
    )(page_table, qi_st, w_st, kin_t, kidx_pool)

    rows = n_seq * dec_seq
    tau, tau_i = pl.pallas_call(
        functools.partial(_dsa_sample_select_kernel, k_sel, idx_bits, nk // sel_ts, sel_ts),
        grid=(1,),
        in_specs=[_const_spec((rows, nk))],
        out_specs=[pl.BlockSpec((rows, LANES), lambda i: (0, 0))] * 2,
        out_shape=[jax.ShapeDtypeStruct((rows, LANES), F32), jax.ShapeDtypeStruct((rows, LANES), I32)],
        compiler_params=_cparams("arbitrary"),
        name="dsa_sample_select",
    )(keys[:, :dec_seq].reshape(rows, nk))
    pad_slots = lambda a: jnp.pad(a.reshape(n_seq, dec_seq, LANES),
                                  ((0, 0), (0, TSLOTS - dec_seq), (0, 0))).reshape(n_seq * TSLOTS, LANES)
    tau, tau_i = pad_slots(tau), pad_slots(tau_i)

    row_spec = pl.BlockSpec((TSLOTS, LANES), lambda b, pt: (b, 0))
    return pl.pallas_call(
        functools.partial(_dsa_sample_attend_kernel, n_pages, page, ppt),
        grid_spec=pltpu.PrefetchScalarGridSpec(
            num_scalar_prefetch=1, grid=(n_seq,),
            in_specs=[seq_spec((SROWS, LANES)), seq_spec((TSLOTS, nk)), row_spec, row_spec,
                      seq_spec((KV_A_W, page)), hbm_spec],
            out_specs=seq_spec((SROWS, LANES)),
            scratch_shapes=ring(KV_A_W) + [pltpu.VMEM((SROWS, n_pages * page), F32)]),
        out_shape=jax.ShapeDtypeStruct((n_seq, SROWS, LANES), F32),
        compiler_params=_cparams("arbitrary"),
        name="dsa_sample_attend",
    )(page_table, qa_st, keys, tau, tau_i, kvn_t, kva_pool)


def _sb_sample_kernel(n_pages, page, pt_ref, q_ref, kvn_ref, kvb_hbm, o_ref, newest, newest_sem, buf, sem,
                      acc_ref, carry_ref):
    b = pl.program_id(0)
    q = q_ref[0]
    later_m = _later_matrix(page, 1)
    last = n_pages - 1
    walk = last - 1

    def newest_copy(seq):
        slot = seq & 1
        return pltpu.make_async_copy(kvb_hbm.at[pt_ref[seq, last]], newest.at[slot], newest_sem.at[slot])

    def page_copy(pg):
        slot = (walk - pg) & 1
        return pltpu.make_async_copy(kvb_hbm.at[pt_ref[b, pg]], buf.at[slot], sem.at[slot])

    def tile(kt, vt, strict):
        a, new_carry = _sb_tile(_dot(q, kt), strict, later_m, carry_ref[:, 0:1], 1)
        acc_ref[...] += _dot_nt(a.astype(BF16), vt)
        carry_ref[...] = jnp.broadcast_to(new_carry, carry_ref.shape)
        return jnp.max(new_carry) >= SB_STOP

    @pl.when(b == 0)
    def _():
        newest_copy(b).start()

    @pl.when(b + 1 < pl.num_programs(0))
    def _():
        newest_copy(b + 1).start()

    if walk >= 0:
        page_copy(walk).start()
    acc_ref[...] = jnp.zeros_like(acc_ref)
    carry_ref[...] = jnp.zeros_like(carry_ref)
    qslot = lax.broadcasted_iota(I32, (SROWS, page), 0) // N_HEADS_B
    u = lax.broadcasted_iota(I32, (SROWS, page), 1)
    tile(kvn_ref[0, 0:Q_B_W, :], kvn_ref[0, Q_B_W:2 * Q_B_W, :], u < qslot)
    newest_copy(b).wait()
    ns = b & 1
    go = tile(newest[ns, 0:Q_B_W, :].astype(BF16), newest[ns, Q_B_W:2 * Q_B_W, :].astype(BF16), None)

    if walk >= 0:
        def cond(state):
            pg, go = state
            return jnp.logical_and(pg >= 0, go)

        def body(state):
            pg, _ = state
            page_copy(pg).wait()

            @pl.when(pg > 0)
            def _():
                page_copy(pg - 1).start()

            slot = (walk - pg) & 1
            go = tile(buf[slot, 0:Q_B_W, :].astype(BF16), buf[slot, Q_B_W:2 * Q_B_W, :].astype(BF16), None)
            return pg - 1, go

        pg_end, _ = lax.while_loop(cond, body, (jnp.int32(walk), go))

        @pl.when(pg_end >= 0)
        def _():
            page_copy(pg_end).wait()

    r = lax.broadcasted_iota(I32, (SROWS, Q_B_W), 0)
    c = lax.broadcasted_iota(I32, (SROWS, Q_B_W), 1)
    own = (r % N_HEADS_B) == (c // HEAD_DIM)
    diag = jnp.where(own, acc_ref[...], 0.0)
    o_ref[0] = jnp.sum(diag.reshape(TSLOTS, N_HEADS_B, Q_B_W), axis=1)


def _sb_sample(qb_bd, kvb_pool, kvn_t, page_table):
    n_seq, n_pages = page_table.shape
    page = kvb_pool.shape[2]
    seq_spec = lambda shp: pl.BlockSpec((1,) + shp, lambda b, pt: (b, 0, 0))
    return pl.pallas_call(
        functools.partial(_sb_sample_kernel, n_pages, page),
        grid_spec=pltpu.PrefetchScalarGridSpec(
            num_scalar_prefetch=1, grid=(n_seq,),
            in_specs=[seq_spec((SROWS, Q_B_W)), seq_spec((KV_B_W, page)), pl.BlockSpec(memory_space=pl.ANY)],
            out_specs=seq_spec((TSLOTS, Q_B_W)),
            scratch_shapes=[pltpu.VMEM((2, KV_B_W, page), F32), pltpu.SemaphoreType.DMA((2,)),
                            pltpu.VMEM((2, KV_B_W, page), F32), pltpu.SemaphoreType.DMA((2,)),
                            pltpu.VMEM((SROWS, Q_B_W), F32), pltpu.VMEM((SROWS, LANES), F32)]),
        out_shape=jax.ShapeDtypeStruct((n_seq, TSLOTS, Q_B_W), F32),
        compiler_params=_cparams("arbitrary"),
        name="sb_sample",
    )(page_table, qb_bd, kvn_t, kvb_pool)


def _out_kernel(oa_ref, ob_ref, gate_ref, x_ref, wa_ref, wb_ref, wo_ref, nf_ref,
                wg_ref, wu_ref, wd_ref, nfin_ref, y_ref):
    ba = _dot(oa_ref[...].astype(BF16), wa_ref[...])
    bb = _dot(ob_ref[...].astype(BF16), wb_ref[...])
    merged = (gate_ref[:, 0:D_MODEL].astype(F32) * ba
              + gate_ref[:, D_MODEL:2 * D_MODEL].astype(F32) * bb)
    h1 = x_ref[...] + _dot(merged.astype(BF16), wo_ref[...])
    ms = jnp.mean(h1 * h1, axis=-1, keepdims=True)
    hn = ((h1 * lax.rsqrt(ms + RMS_EPS)) * nf_ref[...]).astype(BF16)
    acc = jnp.zeros_like(h1)
    for c in range(D_FF // FFN_CHUNK):
        sl = slice(c * FFN_CHUNK, (c + 1) * FFN_CHUNK)
        gg = _dot(hn, wg_ref[:, sl])
        uu = _dot(hn, wu_ref[:, sl])
        act = (gg * (1.0 / (1.0 + jnp.exp(-gg)))) * uu
        acc = acc + _dot(act.astype(BF16), wd_ref[sl, :])
    h2 = h1 + acc
    ms2 = jnp.mean(h2 * h2, axis=-1, keepdims=True)
    y_ref[...] = (h2 * lax.rsqrt(ms2 + RMS_EPS)) * nfin_ref[...]


def _out(oa, ob, gates, x2d, wts, tm):
    rows = x2d.shape[0]
    row_spec = lambda n: pl.BlockSpec((tm, n), lambda i: (i, 0))
    return pl.pallas_call(
        _out_kernel,
        grid=(rows // tm,),
        in_specs=[row_spec(Q_A_W), row_spec(Q_B_W), row_spec(2 * D_MODEL), row_spec(D_MODEL)]
                 + [_const_spec(w.shape) for w in wts],
        out_specs=row_spec(D_MODEL),
        out_shape=jax.ShapeDtypeStruct((rows, D_MODEL), F32),
        compiler_params=_cparams("parallel"),
        name="out",
    )(oa, ob, gates, x2d, *wts)


def _rope_tables(pos):
    inv_freq = ROPE_THETA ** (-jnp.arange(HALF, dtype=F32) / HALF)
    ang = pos.astype(F32)[:, None] * inv_freq[None, :]
    cos, sin = jnp.cos(ang), jnp.sin(ang)
    cs = jnp.concatenate([cos, cos, cos, cos], axis=1)
    sn = jnp.concatenate([-sin, sin, -sin, sin], axis=1)
    return cs, sn, cos.T, sin.T


def _pack_w_in(w):
    d = w.shape[0]
    o = np.cumsum([0, Q_A_W, KV_A_W, Q_I_W, IDX_DIM, N_IDX_HEADS, Q_B_W, KV_B_W, 2 * D_MODEL])
    kw = N_KV_A * HEAD_DIM
    qa = w[:, o[0]:o[1]].reshape(d, N_HEADS_A, HEAD_DIM)[:, np.array(QA_HEAD_ORDER), :].reshape(d, Q_A_W)
    ka, va = w[:, o[1]:o[1] + kw], w[:, o[1] + kw:o[2]]
    qi, ki, wi, qb = w[:, o[2]:o[3]], w[:, o[3]:o[4]], w[:, o[4]:o[5]], w[:, o[5]:o[6]]
    kb, vb = w[:, o[6]:o[6] + Q_B_W], w[:, o[6] + Q_B_W:o[7]]
    g = w[:, o[7]:o[8]]
    w_tok = jnp.concatenate([qa, qi, qb, g], axis=1).astype(BF16)
    wi_pad = jnp.pad(wi, ((0, 0), (0, HEAD_DIM - N_IDX_HEADS)))
    w_feat = jnp.concatenate([ka, va, ki, wi_pad, kb, vb], axis=1).T.astype(BF16)
    return w_tok, w_feat


def kernel(x_prompt, x_sample, cache_kv_a, cache_k_idx, cache_kv_b, page_table, w_in, w_br_a, w_br_b,
           w_o, norm_attn, norm_ffn, w_ffn_gate, w_ffn_up, w_ffn_down, norm_final):
    depth = w_in.shape[0]
    assert depth == 1, "single-layer step"
    batch, seq, d = x_prompt.shape
    assert d == D_MODEL and seq % ROW_TILE == 0
    layer = 0

    w_tok, w_feat = _pack_w_in(w_in[layer])
    wa = w_br_a[layer].reshape(N_HEADS_A, HEAD_DIM, D_MODEL)[np.array(QA_HEAD_ORDER)].reshape(Q_A_W, D_MODEL)
    out_wts = (wa.astype(BF16), w_br_b[layer].astype(BF16), w_o[layer].astype(BF16),
               norm_ffn[layer].reshape(1, D_MODEL),
               w_ffn_gate[layer].astype(BF16), w_ffn_up[layer].astype(BF16), w_ffn_down[layer].astype(BF16),
               norm_final.reshape(1, D_MODEL))
    gain = norm_attn[layer].reshape(1, D_MODEL)

    xp = x_prompt.reshape(batch * seq, D_MODEL)
    (kva_p, kidx_p, kvb_p, qa, qi, wit, qb, gates, kab, kib, kbb, vat, vbt) = _project(
        xp, gain, w_tok, w_feat, *_rope_tables(jnp.arange(seq, dtype=I32)), batch, ROW_TILE)
    oa = _dsa_prompt(qi, wit, qa, kib, kab, vat, batch, seq)
    ob = _sb_prompt(qb, kbb, vbt, batch, seq)
    y_prompt = _out(oa, ob, gates, xp, out_wts, ROW_TILE).reshape(batch, seq, D_MODEL)

    new_kv_a_p = kva_p.reshape(depth, batch, 2, N_KV_A, HEAD_DIM, seq).transpose(0, 1, 5, 2, 3, 4)
    new_k_idx_p = kidx_p.reshape(depth, batch, IDX_DIM, seq).transpose(0, 1, 3, 2)
    new_kv_b_p = kvb_p.reshape(depth, batch, 2, N_HEADS_B, HEAD_DIM, seq).transpose(0, 1, 5, 2, 3, 4)

    n_seq, dec_seq, _ = x_sample.shape
    n_pages = page_table.shape[1]
    n_pool, page = cache_kv_a.shape[1], cache_kv_a.shape[2]
    past = n_pages * page
    rows_s = n_seq * dec_seq
    assert dec_seq <= TSLOTS and rows_s % KBLK == 0 and page == LANES
    xs = x_sample.reshape(rows_s, D_MODEL)
    cs_s, sn_s, csf_s, snf_s = _rope_tables(past + jnp.arange(dec_seq, dtype=I32))
    tabs_s = (jnp.tile(cs_s, (n_seq, 1)), jnp.tile(sn_s, (n_seq, 1)),
              jnp.tile(csf_s, (1, n_seq)), jnp.tile(snf_s, (1, n_seq)))
    (kva_s, kidx_s, kvb_s, qa_s, qi_s, wit_s, qb_s, gates_s, _, _, _, _, _) = _project(
        xs, gain, w_tok, w_feat, *tabs_s, 1, rows_s)

    def slots(a):
        return jnp.pad(a, ((0, 0), (0, TSLOTS - dec_seq)) + ((0, 0),) * (a.ndim - 2))

    def new_tile(a):
        t = a[0].reshape(a.shape[1], n_seq, dec_seq).transpose(1, 0, 2)
        return jnp.pad(t, ((0, 0), (0, 0), (0, page - dec_seq))).astype(BF16)

    qi_st = slots(qi_s.reshape(n_seq, dec_seq, N_IDX_HEADS, IDX_DIM)).reshape(n_seq, SROWS, IDX_DIM)
    w_st = slots(wit_s[0].T.reshape(n_seq, dec_seq, N_IDX_HEADS)).reshape(n_seq, SROWS, 1)
    qa5 = qa_s.reshape(n_seq, dec_seq, GROUP_A, N_KV_A, HEAD_DIM).transpose(0, 1, 3, 2, 4)
    eye2 = jnp.eye(N_KV_A, dtype=qa_s.dtype)
    qa_st = qa5[:, :, :, :, None, :] * eye2[None, None, :, None, :, None]
    qa_st = slots(qa_st.reshape(n_seq, dec_seq, N_HEADS_A, LANES)).reshape(n_seq, SROWS, LANES)
    kidx_pool = cache_k_idx[layer].transpose(0, 2, 1)
    kva_pool = cache_kv_a[layer].transpose(0, 2, 3, 4, 1).reshape(n_pool, KV_A_W, page)
    kvb_pool = cache_kv_b[layer].transpose(0, 2, 3, 4, 1).reshape(n_pool, KV_B_W, page)
    o_st = _dsa_sample(qi_st, w_st, qa_st, kidx_pool, kva_pool, new_tile(kidx_s), new_tile(kva_s),
                       page_table, dec_seq)
    o6 = o_st.reshape(n_seq, TSLOTS, N_KV_A, GROUP_A, N_KV_A, HEAD_DIM)[:, :dec_seq]
    oa_s = jnp.stack([o6[:, :, c, :, c, :] for c in range(N_KV_A)], axis=2).reshape(rows_s, Q_A_W)

    qb4 = qb_s.reshape(n_seq, dec_seq, N_HEADS_B, HEAD_DIM)
    eye8 = jnp.eye(N_HEADS_B, dtype=qb_s.dtype)
    qb_bd = qb4[:, :, :, None, :] * eye8[None, None, :, :, None]
    qb_bd = slots(qb_bd.reshape(n_seq, dec_seq, N_HEADS_B, Q_B_W)).reshape(n_seq, SROWS, Q_B_W)
    ob_st = _sb_sample(qb_bd, kvb_pool, new_tile(kvb_s), page_table)
    ob_s = ob_st[:, :dec_seq].reshape(rows_s, Q_B_W)

    out_wts_s = (w_br_a[layer].astype(BF16),) + out_wts[1:]
    y_sample = _out(oa_s, ob_s, gates_s, xs, out_wts_s, rows_s).reshape(n_seq, dec_seq, D_MODEL)

    return (y_prompt, y_sample, new_kv_a_p, new_k_idx_p, new_kv_b_p,
            kva_s[0].T.reshape(depth, n_seq, dec_seq, 2, N_KV_A, HEAD_DIM),
            kidx_s[0].T.reshape(depth, n_seq, dec_seq, IDX_DIM),
            kvb_s[0].T.reshape(depth, n_seq, dec_seq, 2, N_HEADS_B, HEAD_DIM))
```
